```python
import math
import jax, jax.numpy as jnp
from jax import lax
import numpy as np

D_MODEL = 1024
BATCH = 2
SEQ = 8192
DEPTH = 2

N_A_LAYERS = DEPTH // 2
N_B_LAYERS = DEPTH - N_A_LAYERS

MLSTM_HEADS = 4
MLSTM_INNER = 2 * D_MODEL
MLSTM_HEAD_DIM = MLSTM_INNER // MLSTM_HEADS
CONV_WIDTH = 4
MLSTM_CHUNK = 128

ATTN_HEADS = 16
ATTN_HEAD_DIM = D_MODEL // ATTN_HEADS
MOBA_BLOCK = 256
MOBA_TOPK = 3
MOBA_QCHUNK = 32

N_EXPERTS = 16
N_GROUPS = 4
EXPERTS_PER_GROUP = N_EXPERTS // N_GROUPS
MOE_TOPK = 2
D_FF_EXPERT = 512
MOE_ROW_BLOCK = 256

DEEPNORM_ALPHA = (2.0 * DEPTH) ** 0.25
DEEPNORM_BETA = (8.0 * DEPTH) ** -0.25
LN_EPS = 1e-5
N_MOD_PER_LAYER = 6

kernel_name = 'hybrid_mlstm_moba_groupmoe'


def layer_norm(x, g, b):
    xf = x.astype(jnp.float32)
    mu = xf.mean(-1, keepdims=True)
    var = jnp.square(xf - mu).mean(-1, keepdims=True)
    return ((xf - mu) * lax.rsqrt(var + LN_EPS) * g + b).astype(x.dtype)


def modulate(x, shift, scale):
    return x * (1 + scale[:, None, :]) + shift[:, None, :]


def causal_conv(x, w, b):
    K, C = w.shape
    y = lax.conv_general_dilated(x, w[:, None, :], window_strides=(1,), padding=[(K - 1, 0)],
                                 dimension_numbers=('NWC', 'WIO', 'NWC'), feature_group_count=C)
    return y + b


def mlstm_chunkwise(q, k, v, log_i, log_f):
    B, H, S, Dh = q.shape
    L = MLSTM_CHUNK
    nc = S // L

    def to_chunks(t):
        return jnp.moveaxis(t.reshape(t.shape[:2] + (nc, L) + t.shape[3:]), 2, 0)

    causal = jnp.tril(jnp.ones((L, L), bool))

    def step(carry, inp):
        C, n, m = carry
        qc, kc, vc, ic, fc = inp
        b = jnp.cumsum(fc, axis=-1)
        d = jnp.where(causal, b[..., :, None] - b[..., None, :] + ic[..., None, :], -jnp.inf)
        a = b + m[..., None]
        m_t = jnp.maximum(a, d.max(-1))
        w_intra = jnp.exp(d - m_t[..., None])
        w_inter = jnp.exp(a - m_t)
        s = jnp.einsum('bhtd,bhsd->bhts', qc, kc) * w_intra
        num = jnp.einsum('bhts,bhsd->bhtd', s, vc) + w_inter[..., None] * jnp.einsum('bhvd,bhtd->bhtv', C, qc)
        den = s.sum(-1) + w_inter * jnp.einsum('bhd,bhtd->bht', n, qc)
        h = num / jnp.maximum(jnp.abs(den), jnp.exp(-m_t))[..., None]
        b_last = b[..., -1]
        w_s = b_last[..., None] - b + ic
        m_new = jnp.maximum(b_last + m, w_s.max(-1))
        decay = jnp.exp(b_last + m - m_new)
        w_s = jnp.exp(w_s - m_new[..., None])
        C_new = decay[..., None, None] * C + jnp.einsum('bhs,bhsv,bhsd->bhvd', w_s, vc, kc)
        n_new = decay[..., None] * n + jnp.einsum('bhs,bhsd->bhd', w_s, kc)
        return (C_new, n_new, m_new), h

    init = (jnp.zeros((B, H, Dh, Dh), jnp.float32), jnp.zeros((B, H, Dh), jnp.float32),
            jnp.zeros((B, H), jnp.float32))
    _, hs = lax.scan(step, init, (to_chunks(q), to_chunks(k), to_chunks(v), to_chunks(log_i), to_chunks(log_f)))
    return jnp.moveaxis(hs, 0, 2).reshape(B, H, S, Dh)


def mlstm_layer(h, w_in, conv_w, conv_b, wq, wk, wv, w_if, b_if, gn_g, skip, w_out):
    B, S, _ = h.shape
    NH, DH = MLSTM_HEADS, MLSTM_HEAD_DIM
    f32 = jnp.float32
    xm, z = jnp.split(h @ w_in, 2, axis=-1)
    xc = jax.nn.silu(causal_conv(xm, conv_w, conv_b))
    xc_h = xc.reshape(B, S, NH, DH)
    xm_h = xm.reshape(B, S, NH, DH)
    q = jnp.einsum('bshd,hde->bhse', xc_h, wq).astype(f32)
    k = (jnp.einsum('bshd,hde->bhse', xc_h, wk) * (DH ** -0.5)).astype(f32)
    v = jnp.einsum('bshd,hde->bhse', xm_h, wv).astype(f32)
    gates = jnp.transpose((xc @ w_if + b_if).astype(f32), (0, 2, 1))
    log_i = gates[:, :NH]
    log_f = jax.nn.log_sigmoid(gates[:, NH:])
    hh = mlstm_chunkwise(q, k, v, log_i, log_f)
    mu = hh.mean(-1, keepdims=True)
    var = jnp.square(hh - mu).mean(-1, keepdims=True)
    hh = (hh - mu) * lax.rsqrt(var + LN_EPS)
    hh = hh.transpose(0, 2, 1, 3).reshape(B, S, MLSTM_INNER).astype(h.dtype) * gn_g
    hh = (hh + skip * xc) * jax.nn.silu(z)
    return hh @ w_out


def shared_kv(x, shift, scale, w_kv):
    B, S, D = x.shape
    k, v = jnp.split(modulate(x, shift, scale) @ w_kv, 2, axis=-1)
    s_pad = -(-S // MOBA_BLOCK) * MOBA_BLOCK
    nblk = s_pad // MOBA_BLOCK

    def to_blocks(t):
        t = t.reshape(B, S, ATTN_HEADS, ATTN_HEAD_DIM).transpose(0, 2, 1, 3)
        t = jnp.pad(t, ((0, 0), (0, 0), (0, s_pad - S), (0, 0)))
        return t.reshape(B, ATTN_HEADS, nblk, MOBA_BLOCK, ATTN_HEAD_DIM)

    k_blocks, v_blocks = to_blocks(k), to_blocks(v)
    k_means = k_blocks.astype(jnp.float32).mean(3).astype(k.dtype)
    return k_blocks, v_blocks, k_means


def alibi_slopes(n_heads):
    return jnp.exp2(-8.0 * (jnp.arange(n_heads, dtype=jnp.float32) + 1.0) / n_heads)


def moba_attention(q, k_blocks, v_blocks, k_means, slopes):
    B, H, S, Dh = q.shape
    nblk = k_blocks.shape[2]
    topk = min(MOBA_TOPK, nblk)
    QC, BLK = MOBA_QCHUNK, MOBA_BLOCK
    nq = S // QC
    q_chunks = jnp.moveaxis(q.reshape(B, H, nq, QC, Dh), 2, 0)
    bi = jnp.arange(B)[:, None, None, None]
    hi = jnp.arange(H)[None, :, None, None]
    blk_ids = jnp.arange(nblk)
    offs = jnp.arange(BLK)
    f32 = jnp.float32

    def one_chunk(args):
        ci, qc = args
        t = ci * QC + jnp.arange(QC)
        own = (ci * QC) // BLK
        gate = jnp.einsum('bhtd,bhnd->bhtn', qc, k_means).astype(f32)
        gate = jnp.where(blk_ids < own, gate, -jnp.inf)
        _, sel = lax.top_k(gate, topk)
        valid = sel < own
        k_sel = k_blocks[bi, hi, sel]
        v_sel = v_blocks[bi, hi, sel].reshape(B, H, QC, topk * BLK, Dh)
        s_sel = jnp.einsum('bhtd,bhtjsd->bhtjs', qc, k_sel).astype(f32)
        pos_sel = sel[..., None] * BLK + offs
        dist_sel = (t[:, None, None] - pos_sel).astype(f32)
        s_sel = s_sel - slopes[:, None, None, None] * dist_sel
        s_sel = jnp.where(valid[..., None], s_sel, -jnp.inf).reshape(B, H, QC, topk * BLK)
        k_own = lax.dynamic_index_in_dim(k_blocks, own, axis=2, keepdims=False)
        v_own = lax.dynamic_index_in_dim(v_blocks, own, axis=2, keepdims=False)
        s_own = jnp.einsum('bhtd,bhsd->bhts', qc, k_own).astype(f32)
        pos_own = own * BLK + offs
        dist_own = (t[:, None] - pos_own[None, :]).astype(f32)
        s_own = jnp.where(dist_own >= 0, s_own - slopes[:, None, None] * dist_own, -jnp.inf)
        p = jax.nn.softmax(jnp.concatenate([s_sel, s_own], axis=-1), axis=-1).astype(v_blocks.dtype)
        return (jnp.einsum('bhts,bhtsd->bhtd', p[..., :topk * BLK], v_sel)
                + jnp.einsum('bhts,bhsd->bhtd', p[..., topk * BLK:], v_own))

    outs = lax.map(one_chunk, (jnp.arange(nq), q_chunks))
    return jnp.moveaxis(outs, 0, 2).reshape(B, H, S, Dh)


def moba_layer(h, wq, wo, k_blocks, v_blocks, k_means, slopes):
    B, S, D = h.shape
    q = (h @ wq).reshape(B, S, ATTN_HEADS, ATTN_HEAD_DIM).transpose(0, 2, 1, 3) * (ATTN_HEAD_DIM ** -0.5)
    o = moba_attention(q, k_blocks, v_blocks, k_means, slopes)
    return o.transpose(0, 2, 1, 3).reshape(B, S, D) @ wo


def route(x2, w_router, b_router):
    T = x2.shape[0]
    probs = jax.nn.softmax((x2 @ w_router).astype(jnp.float32), axis=-1)
    sel_scores = (probs + b_router).reshape(T, N_GROUPS, EXPERTS_PER_GROUP)
    grp_score = lax.top_k(sel_scores, MOE_TOPK)[0].sum(-1)
    g = jnp.argmax(grp_score, axis=-1)
    masked = jnp.where(jnp.arange(N_GROUPS)[None, :, None] == g[:, None, None], sel_scores, -jnp.inf)
    _, idx = lax.top_k(masked.reshape(T, N_EXPERTS), MOE_TOPK)
    w = jnp.take_along_axis(probs, idx, axis=-1)
    return idx, w / w.sum(-1, keepdims=True)


def moe_ffn(h, w_router, b_router, w_gate, w_up, w_down):
    B, S, D = h.shape
    T = B * S
    x2 = h.reshape(T, D)
    idx, wts = route(x2, w_router, b_router)
    n_assign = T * MOE_TOPK
    e_flat = idx.reshape(-1)
    tok_flat = jnp.repeat(jnp.arange(T, dtype=jnp.int32), MOE_TOPK)
    w_flat = wts.reshape(-1)
    order = jnp.argsort(e_flat)
    e_sorted = e_flat[order]
    counts = jnp.zeros((N_EXPERTS,), jnp.int32).at[e_flat].add(1)
    starts = jnp.cumsum(counts) - counts
    padded = (counts + MOE_ROW_BLOCK - 1) // MOE_ROW_BLOCK * MOE_ROW_BLOCK
    pends = jnp.cumsum(padded)
    pstarts = pends - padded
    dest = pstarts[e_sorted] + (jnp.arange(n_assign, dtype=jnp.int32) - starts[e_sorted])
    n_blocks = -(-n_assign // MOE_ROW_BLOCK) + N_EXPERTS
    cap = n_blocks * MOE_ROW_BLOCK
    buf_tok = jnp.zeros((cap,), jnp.int32).at[dest].set(tok_flat[order])
    buf_w = jnp.zeros((cap,), jnp.float32).at[dest].set(w_flat[order])
    block_start = jnp.arange(n_blocks, dtype=jnp.int32) * MOE_ROW_BLOCK
    block_expert = jnp.minimum(jnp.searchsorted(pends, block_start, side='right'), N_EXPERTS - 1)
    xs = x2[buf_tok].reshape(n_blocks, MOE_ROW_BLOCK, D)

    def expert_block(args):
        xb, e = args
        return (jax.nn.silu(xb @ w_gate[e]) * (xb @ w_up[e])) @ w_down[e]

    ys = lax.map(expert_block, (xs, block_expert)).reshape(cap, D)
    y = jnp.zeros((T, D), h.dtype).at[buf_tok].add(ys * buf_w[:, None].astype(ys.dtype))
    return y.reshape(B, S, D)


def setup_inputs(seed: int = 0) -> dict:
    key = jax.random.key(seed)
    ks = jax.random.split(key, 27)
    D, DI, NH, DH = D_MODEL, MLSTM_INNER, MLSTM_HEADS, MLSTM_HEAD_DIM
    nA, nB = N_A_LAYERS, N_B_LAYERS

    def nrm(k, shape, s):
        return jax.random.normal(k, shape, jnp.float32) * s

    n_mod = DEPTH * N_MOD_PER_LAYER * D + 2 * D
    x = nrm(ks[0], (BATCH, SEQ, D), 1.0)
    c = nrm(ks[1], (BATCH, D), 1.0)
    w_ada = nrm(ks[2], (D, n_mod), 0.1 * D ** -0.5)
    b_ada = nrm(ks[3], (n_mod,), 0.01)
    ln_g = 1.0 + nrm(ks[4], (DEPTH, 2, D), 0.02)
    ln_b = nrm(ks[5], (DEPTH, 2, D), 0.02)
    a_w_in = nrm(ks[6], (nA, D, 2 * DI), D ** -0.5)
    a_conv_w = nrm(ks[7], (nA, CONV_WIDTH, DI), CONV_WIDTH ** -0.5)
    a_conv_b = nrm(ks[8], (nA, DI), 0.01)
    a_wq = nrm(ks[9], (nA, NH, DH, DH), DH ** -0.5)
    a_wk = nrm(ks[10], (nA, NH, DH, DH), DH ** -0.5)
    a_wv = nrm(ks[11], (nA, NH, DH, DH), DEEPNORM_BETA * DH ** -0.5)
    a_w_if = nrm(ks[12], (nA, DI, 2 * NH), 0.1 * DI ** -0.5)
    a_b_if = jnp.concatenate([nrm(ks[13], (nA, NH), 0.1),
                              jnp.linspace(3.0, 6.0, NH) + nrm(ks[14], (nA, NH), 0.1)], axis=-1)
    a_gn_g = 1.0 + nrm(ks[15], (nA, DI), 0.02)
    a_skip = 1.0 + nrm(ks[16], (nA, DI), 0.02)
    a_w_out = nrm(ks[17], (nA, DI, D), DEEPNORM_BETA * DI ** -0.5)
    b_w_kv = jnp.concatenate([nrm(ks[18], (D, D), D ** -0.5),
                              nrm(ks[19], (D, D), DEEPNORM_BETA * D ** -0.5)], axis=-1)
    b_wq = nrm(ks[20], (nB, D, D), D ** -0.5)
    b_wo = nrm(ks[21], (nB, D, D), DEEPNORM_BETA * D ** -0.5)
    moe_w_router = nrm(ks[22], (D, N_EXPERTS), D ** -0.5)
    moe_b_router = nrm(ks[23], (N_EXPERTS,), 0.01)
    moe_w_gate = nrm(ks[24], (DEPTH, N_EXPERTS, D, D_FF_EXPERT), D ** -0.5)
    moe_w_up = nrm(ks[25], (DEPTH, N_EXPERTS, D, D_FF_EXPERT), D ** -0.5)
    moe_w_down = nrm(ks[26], (DEPTH, N_EXPERTS, D_FF_EXPERT, D), DEEPNORM_BETA * D_FF_EXPERT ** -0.5)
    return {'x': x, 'c': c, 'w_ada': w_ada, 'b_ada': b_ada, 'ln_g': ln_g, 'ln_b': ln_b,
            'a_w_in': a_w_in, 'a_conv_w': a_conv_w, 'a_conv_b': a_conv_b, 'a_wq': a_wq, 'a_wk': a_wk,
            'a_wv': a_wv, 'a_w_if': a_w_if, 'a_b_if': a_b_if, 'a_gn_g': a_gn_g, 'a_skip': a_skip,
            'a_w_out': a_w_out, 'b_w_kv': b_w_kv, 'b_wq': b_wq, 'b_wo': b_wo,
            'moe_w_router': moe_w_router, 'moe_b_router': moe_b_router, 'moe_w_gate': moe_w_gate,
            'moe_w_up': moe_w_up, 'moe_w_down': moe_w_down}


def reference(x, c, w_ada, b_ada, ln_g, ln_b, a_w_in, a_conv_w, a_conv_b, a_wq, a_wk, a_wv, a_w_if,
              a_b_if, a_gn_g, a_skip, a_w_out, b_w_kv, b_wq, b_wo, moe_w_router, moe_b_router,
              moe_w_gate, moe_w_up, moe_w_down):
    B, S, D = x.shape
    n_layer_mod = DEPTH * N_MOD_PER_LAYER * D
    cond = jax.nn.silu(c) @ w_ada + b_ada
    mods = cond[:, :n_layer_mod].reshape(B, DEPTH, 2, 3, D)
    kv_mod = cond[:, n_layer_mod:].reshape(B, 2, D)
    slopes = alibi_slopes(ATTN_HEADS)
    kv = None
    for layer in range(DEPTH):
        shift, scale, gate = mods[:, layer, 0, 0], mods[:, layer, 0, 1], 1.0 + mods[:, layer, 0, 2]
        h = modulate(x, shift, scale)
        if layer < N_A_LAYERS:
            i = layer
            y = mlstm_layer(h, a_w_in[i], a_conv_w[i], a_conv_b[i], a_wq[i], a_wk[i], a_wv[i], a_w_if[i],
                            a_b_if[i], a_gn_g[i], a_skip[i], a_w_out[i])
        else:
            if kv is None:
                kv = shared_kv(x, kv_mod[:, 0], kv_mod[:, 1], b_w_kv)
            j = layer - N_A_LAYERS
            y = moba_layer(h, b_wq[j], b_wo[j], kv[0], kv[1], kv[2], slopes)
        x = layer_norm(DEEPNORM_ALPHA * x + gate[:, None, :] * y, ln_g[layer, 0], ln_b[layer, 0])
        shift, scale, gate = mods[:, layer, 1, 0], mods[:, layer, 1, 1], 1.0 + mods[:, layer, 1, 2]
        h = modulate(x, shift, scale)
        y = moe_ffn(h, moe_w_router, moe_b_router, moe_w_gate[layer], moe_w_up[layer], moe_w_down[layer])
        x = layer_norm(DEEPNORM_ALPHA * x + gate[:, None, :] * y, ln_g[layer, 1], ln_b[layer, 1])
    return x
```

```python
import functools

import jax
import jax.numpy as jnp
from jax import lax
from jax.experimental import pallas as pl
from jax.experimental.pallas import tpu as pltpu

DEPTH = 2
MLSTM_HEADS = 4
CONV_WIDTH = 4
MLSTM_CHUNK = 128
ATTN_HEADS = 16
MOBA_BLOCK = 256
MOBA_TOPK = 3
N_EXPERTS = 16
N_GROUPS = 4
MOE_TOPK = 2
MOE_ROW_BLOCK = 256
DEEPNORM_ALPHA = (2.0 * DEPTH) ** 0.25
LN_EPS = 1e-5
N_MOD_PER_LAYER = 6

LANES = 128
CONV_HALO = 16
NEG = -1e30
VMEM_LIMIT = 56 * 1024 * 1024

F32 = jnp.float32
BF16 = jnp.bfloat16


def _sigmoid(x):
    return 1.0 / (1.0 + jnp.exp(-x))


def _params(sem, vmem=VMEM_LIMIT):
    return pltpu.CompilerParams(dimension_semantics=sem, vmem_limit_bytes=vmem)


def _layer_norm(r, g, b):
    mu = jnp.mean(r, axis=-1, keepdims=True)
    d = r - mu
    var = jnp.mean(d * d, axis=-1, keepdims=True)
    return d * lax.rsqrt(var + LN_EPS) * g + b


def _ada_kernel(c_ref, w_ref, b_ref, o_ref):
    c = c_ref[...]
    s = c * _sigmoid(c)
    o_ref[...] = jnp.dot(s.astype(BF16), w_ref[...].astype(BF16),
                         preferred_element_type=F32) + b_ref[...]


def _ada_cond(c, w_ada, b_ada):
    B, D = c.shape
    N = w_ada.shape[1]
    tn = 2048 if N % 2048 == 0 else N
    cp = jnp.zeros((8, D), F32).at[:B].set(c)
    out = pl.pallas_call(
        _ada_kernel,
        grid=(N // tn,),
        in_specs=[pl.BlockSpec((8, D), lambda j: (0, 0)),
                  pl.BlockSpec((D, tn), lambda j: (0, j)),
                  pl.BlockSpec((1, tn), lambda j: (0, j))],
        out_specs=pl.BlockSpec((8, tn), lambda j: (0, j)),
        out_shape=jax.ShapeDtypeStruct((8, N), F32),
        compiler_params=_params(("arbitrary",)),
        name="ada_cond",
    )(cp, w_ada, b_ada.reshape(1, N))
    return out[:B]


def _inproj_kernel(x_ref, sh_ref, sc_ref, w_ref, xm_ref, z_ref):
    di = xm_ref.shape[-1]
    h = x_ref[...] * (1.0 + sc_ref[0]) + sh_ref[0]
    r = jnp.dot(h.astype(BF16), w_ref[...], preferred_element_type=F32)
    xm_ref[...] = r[:, :di].astype(BF16)
    z_ref[...] = r[:, di:].astype(BF16)


def _inproj(x2, shift, scale, w_in, S):
    T, D = x2.shape
    di = w_in.shape[1] // 2
    tm = min(512, S)
    tpb = S // tm
    return pl.pallas_call(
        _inproj_kernel,
        grid=(T // tm,),
        in_specs=[pl.BlockSpec((tm, D), lambda i: (i, 0)),
                  pl.BlockSpec((1, 1, D), lambda i: (i // tpb, 0, 0)),
                  pl.BlockSpec((1, 1, D), lambda i: (i // tpb, 0, 0)),
                  pl.BlockSpec((D, 2 * di), lambda i: (0, 0))],
        out_specs=[pl.BlockSpec((tm, di), lambda i: (i, 0)),
                   pl.BlockSpec((tm, di), lambda i: (i, 0))],
        out_shape=[jax.ShapeDtypeStruct((T, di), BF16)] * 2,
        compiler_params=_params(("parallel",)),
        name="mlstm_inproj",
    )(x2, shift, scale, w_in)


def _qkv_kernel(xm_ref, halo_ref, cw_ref, cb_ref, wq_ref, wk_ref, wv_ref, wif_ref, bif_ref,
                q_ref, k_ref, v_ref, xc_ref, g_ref, *, tpb, nh, kscale):
    tm, di = xm_ref.shape
    dh = di // nh
    i = pl.program_id(0)
    xm = xm_ref[...]
    xf = xm.astype(F32)
    halo = halo_ref[...].astype(F32)
    halo = jnp.where(i % tpb == 0, 0.0, halo)
    ext = jnp.concatenate([halo, xf], axis=0)
    acc = cb_ref[...] + cw_ref[CONV_WIDTH - 1:CONV_WIDTH, :] * xf
    for s in range(1, CONV_WIDTH):
        acc = acc + cw_ref[CONV_WIDTH - 1 - s:CONV_WIDTH - s, :] * ext[CONV_HALO - s:CONV_HALO - s + tm, :]
    xc = acc * _sigmoid(acc)
    xcb = xc.astype(BF16)
    xc_ref[...] = xcb
    for h in range(nh):
        sl = slice(h * dh, (h + 1) * dh)
        q_ref[:, sl] = jnp.dot(xcb[:, sl], wq_ref[h], preferred_element_type=F32).astype(BF16)
        k_ref[:, sl] = (jnp.dot(xcb[:, sl], wk_ref[h], preferred_element_type=F32) * kscale).astype(BF16)
        v_ref[:, sl] = jnp.dot(xm[:, sl], wv_ref[h], preferred_element_type=F32).astype(BF16)
    g = jnp.dot(xcb, wif_ref[...], preferred_element_type=F32) + bif_ref[...]
    col = lax.broadcasted_iota(jnp.int32, g.shape, 1)
    logsig = jnp.minimum(g, 0.0) - jnp.log(1.0 + jnp.exp(-jnp.abs(g)))
    g = jnp.where(col >= nh, logsig, g)
    g_ref[...] = g[:, :2 * nh]


def _qkv(xm, conv_w, conv_b, wq, wk, wv, w_if, b_if, S):
    T, di = xm.shape
    nh = wq.shape[0]
    dh = di // nh
    tm = min(512, S)
    tpb = S // tm
    hb = tm // CONV_HALO
    wif = jnp.zeros((di, LANES), BF16).at[:, :2 * nh].set(w_if.astype(BF16))
    bif = jnp.zeros((1, LANES), F32).at[0, :2 * nh].set(b_if)
    kern = functools.partial(_qkv_kernel, tpb=tpb, nh=nh, kscale=float(dh) ** -0.5)
    full2 = lambda i: (0, 0)
    full3 = lambda i: (0, 0, 0)
    row = lambda i: (i, 0)
    return pl.pallas_call(
        kern,
        grid=(T // tm,),
        in_specs=[pl.BlockSpec((tm, di), row),
                  pl.BlockSpec((CONV_HALO, di), lambda i: (jnp.maximum(i * hb - 1, 0), 0)),
                  pl.BlockSpec((CONV_WIDTH, di), full2),
                  pl.BlockSpec((1, di), full2),
                  pl.BlockSpec((nh, dh, dh), full3),
                  pl.BlockSpec((nh, dh, dh), full3),
                  pl.BlockSpec((nh, dh, dh), full3),
                  pl.BlockSpec((di, LANES), full2),
                  pl.BlockSpec((1, LANES), full2)],
        out_specs=[pl.BlockSpec((tm, di), row)] * 4 + [pl.BlockSpec((tm, 2 * nh), row)],
        out_shape=[jax.ShapeDtypeStruct((T, di), BF16)] * 4 + [jax.ShapeDtypeStruct((T, 2 * nh), F32)],
        compiler_params=_params(("parallel",)),
        name="mlstm_qkv",
    )(xm, xm, conv_w, conv_b.reshape(1, di), wq, wk, wv, wif, bif)


def _mlstm_kernel(q_ref, k_ref, v_ref, gc_ref, gr_ref, o_ref, c_scr, n_scr, m_scr):
    L = q_ref.shape[0]

    @pl.when(pl.program_id(2) == 0)
    def _():
        c_scr[...] = jnp.zeros_like(c_scr)
        n_scr[...] = jnp.zeros_like(n_scr)
        m_scr[...] = jnp.zeros_like(m_scr)

    q = q_ref[...]
    k = k_ref[...]
    v = v_ref[...]
    gc = gc_ref[0, 0]
    gr = gr_ref[0, 0]
    i_col, f_col = gc[:, 0:1], gc[:, 1:2]
    i_row, f_row = gr[0:1, :], gr[1:2, :]
    t_idx = lax.broadcasted_iota(jnp.int32, (L, L), 0)
    s_idx = lax.broadcasted_iota(jnp.int32, (L, L), 1)
    causal = s_idx <= t_idx
    b_col = jnp.sum(jnp.where(causal, f_row, 0.0), axis=1, keepdims=True)
    b_row = jnp.sum(jnp.where(t_idx <= s_idx, f_col, 0.0), axis=0, keepdims=True)
    m_prev = m_scr[...]
    d = jnp.where(causal, b_col - b_row + i_row, -jnp.inf)
    a_col = b_col + m_prev
    m_t = jnp.maximum(a_col, jnp.max(d, axis=1, keepdims=True))
    w_intra = jnp.exp(d - m_t)
    w_inter = jnp.exp(a_col - m_t)
    qk = lax.dot_general(q, k, (((1,), (1,)), ((), ())), preferred_element_type=F32)
    s_mat = qk * w_intra
    c_b = c_scr[...].astype(BF16)
    inter = lax.dot_general(q, c_b, (((1,), (1,)), ((), ())), preferred_element_type=F32)
    num = jnp.dot(s_mat.astype(BF16), v, preferred_element_type=F32) + w_inter * inter
    qn = jnp.sum(q.astype(F32) * n_scr[...], axis=1, keepdims=True)
    den = jnp.sum(s_mat, axis=1, keepdims=True) + w_inter * qn
    hcap = num / jnp.maximum(jnp.abs(den), jnp.exp(-m_t))
    mu = jnp.mean(hcap, axis=1, keepdims=True)
    dv = hcap - mu
    var = jnp.mean(dv * dv, axis=1, keepdims=True)
    o_ref[...] = (dv * lax.rsqrt(var + LN_EPS)).astype(BF16)

    b_last = jnp.sum(f_row, axis=1, keepdims=True)
    ws_col = b_last - b_col + i_col
    m_new = jnp.maximum(b_last + m_prev, jnp.max(ws_col, axis=0, keepdims=True))
    decay = jnp.exp(b_last + m_prev - m_new)
    ws = jnp.exp(ws_col - m_new)
    vw = (v.astype(F32) * ws).astype(BF16)
    upd = lax.dot_general(vw, k, (((0,), (0,)), ((), ())), preferred_element_type=F32)
    c_scr[...] = decay * c_scr[...] + upd
    n_scr[...] = decay * n_scr[...] + jnp.sum(k.astype(F32) * ws, axis=0, keepdims=True)
    m_scr[...] = m_new


def _mlstm(q, k, v, gcol, grow, B, S):
    T, di = q.shape
    nh = gcol.shape[1]
    dh = di // nh
    L = MLSTM_CHUNK
    nc = S // L
    blk = pl.BlockSpec((L, dh), lambda b, h, c: (b * nc + c, h))
    return pl.pallas_call(
        _mlstm_kernel,
        grid=(B, nh, nc),
        in_specs=[blk, blk, blk,
                  pl.BlockSpec((1, 1, L, 2), lambda b, h, c: (b, h, c, 0)),
                  pl.BlockSpec((1, 1, 2, L), lambda b, h, c: (b, h, 0, c))],
        out_specs=blk,
        out_shape=jax.ShapeDtypeStruct((T, di), BF16),
        scratch_shapes=[pltpu.VMEM((dh, dh), F32), pltpu.VMEM((1, dh), F32), pltpu.VMEM((1, 1), F32)],
        compiler_params=_params(("parallel", "parallel", "arbitrary")),
        name="mlstm_scan",
    )(q, k, v, gcol, grow)


def _mlstm_out_kernel(hn_ref, xc_ref, z_ref, x_ref, gn_ref, skip_ref, w_ref, gate_ref, lg_ref, lb_ref, o_ref):
    z = z_ref[...].astype(F32)
    u = (hn_ref[...].astype(F32) * gn_ref[...] + skip_ref[...] * xc_ref[...].astype(F32)) * (z * _sigmoid(z))
    y = jnp.dot(u.astype(BF16), w_ref[...], preferred_element_type=F32)
    r = DEEPNORM_ALPHA * x_ref[...] + gate_ref[0] * y
    o_ref[...] = _layer_norm(r, lg_ref[...], lb_ref[...])


def _mlstm_out(hn, xc, z, x2, gn_g, skip, w_out, gate, ln_g, ln_b, S):
    T, D = x2.shape
    di = hn.shape[1]
    tm = min(512, S)
    tpb = S // tm
    row = lambda i: (i, 0)
    full2 = lambda i: (0, 0)
    return pl.pallas_call(
        _mlstm_out_kernel,
        grid=(T // tm,),
        in_specs=[pl.BlockSpec((tm, di), row), pl.BlockSpec((tm, di), row), pl.BlockSpec((tm, di), row),
                  pl.BlockSpec((tm, D), row),
                  pl.BlockSpec((1, di), full2), pl.BlockSpec((1, di), full2),
                  pl.BlockSpec((di, D), full2),
                  pl.BlockSpec((1, 1, D), lambda i: (i // tpb, 0, 0)),
                  pl.BlockSpec((1, D), full2), pl.BlockSpec((1, D), full2)],
        out_specs=pl.BlockSpec((tm, D), row),
        out_shape=jax.ShapeDtypeStruct((T, D), F32),
        compiler_params=_params(("parallel",)),
        name="mlstm_out_ln",
    )(hn, xc, z, x2, gn_g.reshape(1, di), skip.reshape(1, di), w_out, gate, ln_g.reshape(1, D), ln_b.reshape(1, D))


def _proj_ln_kernel(a_ref, x_ref, w_ref, gate_ref, lg_ref, lb_ref, o_ref):
    y = jnp.dot(a_ref[...], w_ref[...], preferred_element_type=F32)
    r = DEEPNORM_ALPHA * x_ref[...] + gate_ref[0] * y
    o_ref[...] = _layer_norm(r, lg_ref[...], lb_ref[...])


def _proj_ln(a, x2, w, gate, ln_g, ln_b, S):
    T, D = x2.shape
    K = a.shape[1]
    tm = min(512, S)
    tpb = S // tm
    row = lambda i: (i, 0)
    full2 = lambda i: (0, 0)
    return pl.pallas_call(
        _proj_ln_kernel,
        grid=(T // tm,),
        in_specs=[pl.BlockSpec((tm, K), row), pl.BlockSpec((tm, D), row),
                  pl.BlockSpec((K, D), full2),
                  pl.BlockSpec((1, 1, D), lambda i: (i // tpb, 0, 0)),
                  pl.BlockSpec((1, D), full2), pl.BlockSpec((1, D), full2)],
        out_specs=pl.BlockSpec((tm, D), row),
        out_shape=jax.ShapeDtypeStruct((T, D), F32),
        compiler_params=_params(("parallel",)),
        name="attn_out_ln",
    )(a, x2, w, gate, ln_g.reshape(1, D), ln_b.reshape(1, D))


def _route_kernel(x_ref, sh_ref, sc_ref, wrt_ref, br_ref, tri_ref, out_ref, cnt_ref, carry):
    @pl.when(pl.program_id(0) == 0)
    def _():
        carry[...] = jnp.zeros_like(carry)

    E = wrt_ref.shape[0]
    epg = E // N_GROUPS
    h = x_ref[...] * (1.0 + sc_ref[0]) + sh_ref[0]
    lt = lax.dot_general(wrt_ref[...], h, (((1,), (1,)), ((), ())),
                         precision=lax.Precision.HIGHEST, preferred_element_type=F32)
    ex = jnp.exp(lt - jnp.max(lt, axis=0, keepdims=True))
    probs = ex / jnp.sum(ex, axis=0, keepdims=True)
    sel = probs + br_ref[...]
    srow = [sel[e:e + 1, :] for e in range(E)]
    prow = [probs[e:e + 1, :] for e in range(E)]
    gscore = []
    for g in range(N_GROUPS):
        r = srow[g * epg:(g + 1) * epg]
        best = None
        for a in range(epg):
            for b in range(a + 1, epg):
                pair = r[a] + r[b]
                best = pair if best is None else jnp.maximum(best, pair)
        gscore.append(best)
    gmax = functools.reduce(jnp.maximum, gscore)
    chosen, taken = [], None
    for g in range(N_GROUPS):
        c = gscore[g] == gmax
        if taken is not None:
            c = c & jnp.logical_not(taken)
        taken = c if taken is None else (taken | c)
        chosen.append(c)
    zero = jnp.zeros_like(srow[0])
    e1 = zero
    e2 = zero
    p1 = zero
    p2 = zero
    firsts, seconds = [], []
    for e in range(E):
        g = e // epg
        rank = zero
        for o in range(g * epg, (g + 1) * epg):
            if o == e:
                continue
            beats = (srow[o] > srow[e]) | ((srow[o] == srow[e]) & (o < e))
            rank = rank + jnp.where(beats, 1.0, 0.0)
        is1 = chosen[g] & (rank == 0.0)
        is2 = chosen[g] & (rank == 1.0)
        firsts.append(is1)
        seconds.append(is2)
        e1 = e1 + jnp.where(is1, float(e), 0.0)
        e2 = e2 + jnp.where(is2, float(e), 0.0)
        p1 = p1 + jnp.where(is1, prow[e], 0.0)
        p2 = p2 + jnp.where(is2, prow[e], 0.0)
    mask = jnp.concatenate([jnp.where(firsts[e] | seconds[e], 1.0, 0.0) for e in range(E)], axis=0)
    prefix = jnp.dot(mask.astype(BF16), tri_ref[...], preferred_element_type=F32) + carry[...]
    r1 = zero
    r2 = zero
    for e in range(E):
        pe = prefix[e:e + 1, :]
        r1 = r1 + jnp.where(firsts[e], pe, 0.0)
        r2 = r2 + jnp.where(seconds[e], pe, 0.0)
    psum = p1 + p2
    out_ref[...] = jnp.concatenate([e1, e2, p1 / psum, p2 / psum, r1, r2, zero, zero], axis=0)
    carry[...] = carry[...] + jnp.sum(mask, axis=1, keepdims=True)
    cnt_ref[...] = jnp.broadcast_to(carry[...], cnt_ref.shape)


def _route(x2, shift, scale, w_router, b_router, S):
    T, D = x2.shape
    E = w_router.shape[1]
    tm = min(512, S)
    tpb = S // tm
    tri = (jnp.arange(tm)[:, None] < jnp.arange(tm)[None, :]).astype(BF16)
    return pl.pallas_call(
        _route_kernel,
        grid=(T // tm,),
        in_specs=[pl.BlockSpec((tm, D), lambda i: (i, 0)),
                  pl.BlockSpec((1, 1, D), lambda i: (i // tpb, 0, 0)),
                  pl.BlockSpec((1, 1, D), lambda i: (i // tpb, 0, 0)),
                  pl.BlockSpec((E, D), lambda i: (0, 0)),
                  pl.BlockSpec((E, 1), lambda i: (0, 0)),
                  pl.BlockSpec((tm, tm), lambda i: (0, 0))],
        out_specs=[pl.BlockSpec((8, tm), lambda i: (0, i)),
                   pl.BlockSpec((E, LANES), lambda i: (0, 0))],
        out_shape=[jax.ShapeDtypeStruct((8, T), F32), jax.ShapeDtypeStruct((E, LANES), F32)],
        scratch_shapes=[pltpu.VMEM((E, 1), F32)],
        compiler_params=_params(("arbitrary",)),
        name="moe_route",
    )(x2, shift, scale, w_router.T, b_router.reshape(E, 1), tri)


def _dispatch_kernel(d1_ref, d2_ref, x_ref, sh_ref, sc_ref, xs_in_ref, xs_ref, hbuf, sem):
    del xs_in_ref
    tr = x_ref.shape[0]
    base = pl.program_id(0) * tr
    hbuf[...] = x_ref[...] * (1.0 + sc_ref[0]) + sh_ref[0]

    def row_copies(r):
        src = hbuf.at[pl.ds(r, 1)]
        return (pltpu.make_async_copy(src, xs_ref.at[pl.ds(d1_ref[base + r], 1)], sem),
                pltpu.make_async_copy(src, xs_ref.at[pl.ds(d2_ref[base + r], 1)], sem))

    def start(r, c):
        a, b = row_copies(r)
        a.start()
        b.start()
        return c

    def wait(r, c):
        a, b = row_copies(r)
        a.wait()
        b.wait()
        return c

    lax.fori_loop(0, tr, start, 0)
    lax.fori_loop(0, tr, wait, 0)


def _dispatch(x2, shift, scale, d1, d2, cap, S):
    T, D = x2.shape
    tr = min(256, S)
    tpb = S // tr
    grid_spec = pltpu.PrefetchScalarGridSpec(
        num_scalar_prefetch=2,
        grid=(T // tr,),
        in_specs=[pl.BlockSpec((tr, D), lambda i, d1, d2: (i, 0)),
                  pl.BlockSpec((1, 1, D), lambda i, d1, d2: (i // tpb, 0, 0)),
                  pl.BlockSpec((1, 1, D), lambda i, d1, d2: (i // tpb, 0, 0)),
                  pl.BlockSpec(memory_space=pl.ANY)],
        out_specs=pl.BlockSpec(memory_space=pl.ANY),
        scratch_shapes=[pltpu.VMEM((tr, D), F32), pltpu.SemaphoreType.DMA(())],
    )
    return pl.pallas_call(
        _dispatch_kernel,
        grid_spec=grid_spec,
        out_shape=jax.ShapeDtypeStruct((cap, D), F32),
        input_output_aliases={5: 0},
        compiler_params=_params(("arbitrary",)),
        name="moe_dispatch",
    )(d1, d2, x2, shift, scale, jnp.zeros((cap, D), F32))


def _expert_kernel(be_ref, na_ref, xs_ref, wg_ref, wu_ref, wd_ref, ys_ref):
    del be_ref

    @pl.when(pl.program_id(0) < na_ref[0])
    def _():
        xb = xs_ref[...].astype(BF16)
        g = jnp.dot(xb, wg_ref[0], preferred_element_type=F32)
        u = jnp.dot(xb, wu_ref[0], preferred_element_type=F32)
        a = (g * _sigmoid(g)) * u
        ys_ref[...] = jnp.dot(a.astype(BF16), wd_ref[0], preferred_element_type=F32)

    @pl.when(pl.program_id(0) >= na_ref[0])
    def _():
        ys_ref[...] = jnp.zeros_like(ys_ref)


def _experts(xs, block_expert, n_active, w_gate, w_up, w_down):
    cap, D = xs.shape
    E, _, F = w_gate.shape
    rb = MOE_ROW_BLOCK
    nb = cap // rb

    def blk(i, be, na):
        return jnp.minimum(i, na[0] - 1)

    grid_spec = pltpu.PrefetchScalarGridSpec(
        num_scalar_prefetch=2,
        grid=(nb,),
        in_specs=[pl.BlockSpec((rb, D), lambda i, be, na: (blk(i, be, na), 0)),
                  pl.BlockSpec((1, D, F), lambda i, be, na: (be[blk(i, be, na)], 0, 0)),
                  pl.BlockSpec((1, D, F), lambda i, be, na: (be[blk(i, be, na)], 0, 0)),
                  pl.BlockSpec((1, F, D), lambda i, be, na: (be[blk(i, be, na)], 0, 0))],
        out_specs=pl.BlockSpec((rb, D), lambda i, be, na: (i, 0)),
    )
    return pl.pallas_call(
        _expert_kernel,
        grid_spec=grid_spec,
        out_shape=jax.ShapeDtypeStruct((cap, D), F32),
        compiler_params=_params(("arbitrary",)),
        name="moe_experts",
    )(block_expert, n_active, xs, w_gate, w_up, w_down)


def _combine_kernel(d1_ref, d2_ref, ys_ref, rt_ref, x_ref, gate_ref, lg_ref, lb_ref, o_ref, buf, sem):
    tr = x_ref.shape[0]
    base = pl.program_id(0) * tr

    def row_copies(r):
        return (pltpu.make_async_copy(ys_ref.at[pl.ds(d1_ref[base + r], 1)], buf.at[0, pl.ds(r, 1)], sem),
                pltpu.make_async_copy(ys_ref.at[pl.ds(d2_ref[base + r], 1)], buf.at[1, pl.ds(r, 1)], sem))

    def start(r, c):
        a, b = row_copies(r)
        a.start()
        b.start()
        return c

    def wait(r, c):
        a, b = row_copies(r)
        a.wait()
        b.wait()
        return c

    lax.fori_loop(0, tr, start, 0)
    lax.fori_loop(0, tr, wait, 0)
    rt = rt_ref[...]
    y = rt[:, 2:3] * buf[0] + rt[:, 3:4] * buf[1]
    r = DEEPNORM_ALPHA * x_ref[...] + gate_ref[0] * y
    o_ref[...] = _layer_norm(r, lg_ref[...], lb_ref[...])


def _combine(ys, route_t, x2, gate, ln_g, ln_b, d1, d2, S):
    T, D = x2.shape
    tr = min(256, S)
    tpb = S // tr
    grid_spec = pltpu.PrefetchScalarGridSpec(
        num_scalar_prefetch=2,
        grid=(T // tr,),
        in_specs=[pl.BlockSpec(memory_space=pl.ANY),
                  pl.BlockSpec((tr, 8), lambda i, d1, d2: (i, 0)),
                  pl.BlockSpec((tr, D), lambda i, d1, d2: (i, 0)),
                  pl.BlockSpec((1, 1, D), lambda i, d1, d2: (i // tpb, 0, 0)),
                  pl.BlockSpec((1, D), lambda i, d1, d2: (0, 0)),
                  pl.BlockSpec((1, D), lambda i, d1, d2: (0, 0))],
        out_specs=pl.BlockSpec((tr, D), lambda i, d1, d2: (i, 0)),
        scratch_shapes=[pltpu.VMEM((2, tr, D), F32), pltpu.SemaphoreType.DMA(())],
    )
    return pl.pallas_call(
        _combine_kernel,
        grid_spec=grid_spec,
        out_shape=jax.ShapeDtypeStruct((T, D), F32),
        compiler_params=_params(("arbitrary",)),
        name="moe_combine_ln",
    )(d1, d2, ys, route_t, x2, gate, ln_g.reshape(1, D), ln_b.reshape(1, D))


def _moe_layer(x2, shift, scale, gate, ln_g, ln_b, w_router, b_router, w_gate, w_up, w_down, S):
    T, D = x2.shape
    E = w_router.shape[1]
    rb = MOE_ROW_BLOCK
    route, cnt = _route(x2, shift, scale, w_router, b_router, S)
    counts = cnt[:, 0].astype(jnp.int32)
    padded = (counts + rb - 1) // rb * rb
    pends = jnp.cumsum(padded)
    pstarts = pends - padded
    e1 = route[0].astype(jnp.int32)
    e2 = route[1].astype(jnp.int32)
    d1 = pstarts[e1] + route[4].astype(jnp.int32)
    d2 = pstarts[e2] + route[5].astype(jnp.int32)
    nb = -(-(T * MOE_TOPK) // rb) + E
    cap = nb * rb
    block_start = jnp.arange(nb, dtype=jnp.int32) * rb
    block_expert = jnp.minimum(jnp.searchsorted(pends, block_start, side='right'), E - 1).astype(jnp.int32)
    n_active = (pends[-1:] // rb).astype(jnp.int32)
    xs = _dispatch(x2, shift, scale, d1, d2, cap, S)
    ys = _experts(xs, block_expert, n_active, w_gate, w_up, w_down)
    return _combine(ys, route.T, x2, gate, ln_g, ln_b, d1, d2, S)


def _moba_proj_kernel(x_ref, shq_ref, scq_ref, shkv_ref, sckv_ref, wq_ref, wkv_ref,
                      q_ref, k_ref, vt_ref, km_ref, *, qscale):
    tm, D = x_ref.shape
    x = x_ref[...]
    hq = x * (1.0 + scq_ref[0]) + shq_ref[0]
    hkv = x * (1.0 + sckv_ref[0]) + shkv_ref[0]
    q_ref[...] = (jnp.dot(hq.astype(BF16), wq_ref[...], preferred_element_type=F32) * qscale).astype(BF16)
    kv = jnp.dot(hkv.astype(BF16), wkv_ref[...], preferred_element_type=F32)
    k = kv[:, :D]
    k_ref[...] = k.astype(BF16)
    vt_ref[0] = kv[:, D:].T.astype(BF16)
    for j in range(tm // MOBA_BLOCK):
        km_ref[j] = jnp.mean(k[j * MOBA_BLOCK:(j + 1) * MOBA_BLOCK, :], axis=0, keepdims=True)


def _moba_proj(x2, shq, scq, shkv, sckv, wq, wkv, B, S):
    T, D = x2.shape
    tm = min(512, S)
    tpb = S // tm
    nbt = tm // MOBA_BLOCK
    mod = pl.BlockSpec((1, 1, D), lambda i: (i // tpb, 0, 0))
    kern = functools.partial(_moba_proj_kernel, qscale=float(D // ATTN_HEADS) ** -0.5)
    return pl.pallas_call(
        kern,
        grid=(T // tm,),
        in_specs=[pl.BlockSpec((tm, D), lambda i: (i, 0)), mod, mod, mod, mod,
                  pl.BlockSpec((D, D), lambda i: (0, 0)),
                  pl.BlockSpec((D, 2 * D), lambda i: (0, 0))],
        out_specs=[pl.BlockSpec((tm, D), lambda i: (i, 0)),
                   pl.BlockSpec((tm, D), lambda i: (i, 0)),
                   pl.BlockSpec((1, D, tm), lambda i: (i // tpb, 0, i % tpb)),
                   pl.BlockSpec((nbt, 1, D), lambda i: (i, 0, 0))],
        out_shape=[jax.ShapeDtypeStruct((T, D), BF16), jax.ShapeDtypeStruct((T, D), BF16),
                   jax.ShapeDtypeStruct((B, D, S), BF16),
                   jax.ShapeDtypeStruct((T // MOBA_BLOCK, 1, D), F32)],
        compiler_params=_params(("parallel",)),
        name="moba_proj",
    )(x2, shq, scq, shkv, sckv, wq, wkv)


def _moba_kernel(slopes_ref, q_ref, k_ref, vt_ref, km_ref, o_ref, pen_scr):
    BLK = q_ref.shape[0]
    nblk = km_ref.shape[1]
    hd = q_ref.shape[1] // 2
    hp = pl.program_id(1)
    own = pl.program_id(2)
    qb = q_ref[...]
    lane = lax.broadcasted_iota(jnp.int32, (1, 2 * hd), 1)
    blk_id = lax.broadcasted_iota(jnp.int32, (nblk, BLK), 0)
    k_off = lax.broadcasted_iota(jnp.int32, (BLK, 1), 0)
    k_off_f = k_off.astype(F32)
    q_off = lax.broadcasted_iota(jnp.int32, (1, BLK), 1)
    outs = []
    for j in range(2):
        slope = slopes_ref[2 * hp + j]
        hmask = (lane >= j * hd) & (lane < (j + 1) * hd)
        qh = jnp.where(hmask, qb, jnp.zeros_like(qb))
        kmh = jnp.where(hmask, km_ref[0], 0.0)
        gate = lax.dot_general(kmh, qh.astype(F32), (((1,), (1,)), ((), ())),
                               precision=lax.Precision.HIGHEST, preferred_element_type=F32)
        g = jnp.where(blk_id < own, gate, -jnp.inf)
        sel = blk_id < 0
        for _ in range(MOBA_TOPK):
            mx = jnp.max(g, axis=0, keepdims=True)
            first = jnp.min(jnp.where(g == mx, blk_id, nblk), axis=0, keepdims=True)
            pick = blk_id == first
            sel = sel | pick
            g = jnp.where(pick, -jnp.inf, g)
        pen_scr[j] = jnp.where(sel & (blk_id < own), 0.0, NEG)

        def step(n, carry, qh=qh, slope=slope, j=j):
            m, l, acc = carry
            start = pl.multiple_of(n * BLK, BLK)
            kn = k_ref[pl.ds(start, BLK), :]
            st = lax.dot_general(kn, qh, (((1,), (1,)), ((), ())), preferred_element_type=F32)
            bias = slope * (k_off_f + ((n - own) * BLK).astype(F32))
            st = st + bias + pen_scr[j, pl.ds(n, 1), :]
            m_new = jnp.maximum(m, jnp.max(st, axis=0, keepdims=True))
            alpha = jnp.exp(m - m_new)
            p = jnp.exp(st - m_new)
            l = alpha * l + jnp.sum(p, axis=0, keepdims=True)
            vtn = vt_ref[0, j * hd:(j + 1) * hd, pl.ds(start, BLK)]
            acc = alpha * acc + jnp.dot(vtn, p.astype(BF16), preferred_element_type=F32)
            return m_new, l, acc

        init = (jnp.full((1, BLK), NEG, F32), jnp.zeros((1, BLK), F32), jnp.zeros((hd, BLK), F32))
        m, l, acc = lax.fori_loop(0, own, step, init)
        start = pl.multiple_of(own * BLK, BLK)
        kn = k_ref[pl.ds(start, BLK), :]
        st = lax.dot_general(kn, qh, (((1,), (1,)), ((), ())), preferred_element_type=F32)
        st = jnp.where(k_off <= q_off, st + slope * k_off_f, NEG)
        m_new = jnp.maximum(m, jnp.max(st, axis=0, keepdims=True))
        alpha = jnp.exp(m - m_new)
        p = jnp.exp(st - m_new)
        l = alpha * l + jnp.sum(p, axis=0, keepdims=True)
        vtn = vt_ref[0, j * hd:(j + 1) * hd, pl.ds(start, BLK)]
        acc = alpha * acc + jnp.dot(vtn, p.astype(BF16), preferred_element_type=F32)
        outs.append(acc / l)
    o_ref[...] = jnp.concatenate(outs, axis=0).T.astype(BF16)


def _moba_attn(q, k, vt, km, slopes, B, S):
    T, D = q.shape
    BLK = MOBA_BLOCK
    nblk = S // BLK
    hd = D // ATTN_HEADS
    grid_spec = pltpu.PrefetchScalarGridSpec(
        num_scalar_prefetch=1,
        grid=(B, ATTN_HEADS // 2, nblk),
        in_specs=[pl.BlockSpec((BLK, 2 * hd), lambda b, hp, qi, s: (b * nblk + qi, hp)),
                  pl.BlockSpec((S, 2 * hd), lambda b, hp, qi, s: (b, hp)),
                  pl.BlockSpec((1, 2 * hd, S), lambda b, hp, qi, s: (b, hp, 0)),
                  pl.BlockSpec((1, nblk, 2 * hd), lambda b, hp, qi, s: (b, 0, hp))],
        out_specs=pl.BlockSpec((BLK, 2 * hd), lambda b, hp, qi, s: (b * nblk + qi, hp)),
        scratch_shapes=[pltpu.VMEM((2, nblk, BLK), F32)],
    )
    return pl.pallas_call(
        _moba_kernel,
        grid_spec=grid_spec,
        out_shape=jax.ShapeDtypeStruct((T, D), BF16),
        compiler_params=_params(("parallel", "parallel", "arbitrary")),
        name="moba_attn",
    )(slopes, q, k, vt, km)


def kernel(x, c, w_ada, b_ada, ln_g, ln_b, a_w_in, a_conv_w, a_conv_b, a_wq, a_wk, a_wv, a_w_if, a_b_if,
           a_gn_g, a_skip, a_w_out, b_w_kv, b_wq, b_wo, moe_w_router, moe_b_router, moe_w_gate, moe_w_up,
           moe_w_down):
    B, S, D = x.shape
    T = B * S
    assert DEPTH == 2 and S % MOBA_BLOCK == 0 and S % MLSTM_CHUNK == 0
    n_layer_mod = DEPTH * N_MOD_PER_LAYER * D
    cond = _ada_cond(c, w_ada, b_ada)
    mods = cond[:, :n_layer_mod].reshape(B, DEPTH, 2, 3, 1, D)
    kv_mod = cond[:, n_layer_mod:].reshape(B, 2, 1, D)

    def mod3(layer, sub):
        m = mods[:, layer, sub]
        return m[:, 0], m[:, 1], 1.0 + m[:, 2]

    xf = x.reshape(T, D)

    shift, scale, gate = mod3(0, 0)
    nh = MLSTM_HEADS
    xm, z = _inproj(xf, shift, scale, a_w_in[0].astype(BF16), S)
    q, k, v, xc, gts = _qkv(xm, a_conv_w[0], a_conv_b[0], a_wq[0].astype(BF16), a_wk[0].astype(BF16),
                            a_wv[0].astype(BF16), a_w_if[0], a_b_if[0], S)
    g4 = gts.reshape(B, S, 2, nh)
    gcol = jnp.transpose(g4, (0, 3, 1, 2))
    grow = jnp.transpose(g4, (0, 3, 2, 1))
    hn = _mlstm(q, k, v, gcol, grow, B, S)
    xf = _mlstm_out(hn, xc, z, xf, a_gn_g[0], a_skip[0], a_w_out[0].astype(BF16), gate, ln_g[0, 0], ln_b[0, 0], S)
    shift, scale, gate = mod3(0, 1)
    xf = _moe_layer(xf, shift, scale, gate, ln_g[0, 1], ln_b[0, 1], moe_w_router, moe_b_router,
                    moe_w_gate[0].astype(BF16), moe_w_up[0].astype(BF16), moe_w_down[0].astype(BF16), S)

    shift, scale, gate = mod3(1, 0)
    q, k, vt, km = _moba_proj(xf, shift, scale, kv_mod[:, 0], kv_mod[:, 1], b_wq[0].astype(BF16),
                              b_w_kv.astype(BF16), B, S)
    km = km.reshape(B, S // MOBA_BLOCK, D)
    slopes = jnp.exp2(-8.0 * (jnp.arange(ATTN_HEADS, dtype=F32) + 1.0) / ATTN_HEADS)
    attn = _moba_attn(q, k, vt, km, slopes, B, S)
    xf = _proj_ln(attn, xf, b_wo[0].astype(BF16), gate, ln_g[1, 0], ln_b[1, 0], S)
    shift, scale, gate = mod3(1, 1)
    xf = _moe_layer(xf, shift, scale, gate, ln_g[1, 1], ln_b[1, 1], moe_w_router, moe_b_router,
                    moe_w_gate[1].astype(BF16), moe_w_up[1].astype(BF16), moe_w_down[1].astype(BF16), S)
    return xf.reshape(B, S, D)
```

```python
import functools

import jax
import jax.numpy as jnp
from jax import lax
from jax.experimental import pallas as pl
from jax.experimental.pallas import tpu as pltpu

DEPTH = 2
MLSTM_HEADS = 4
CONV_WIDTH = 4
MLSTM_CHUNK = 128
ATTN_HEADS = 16
MOBA_BLOCK = 256
MOBA_TOPK = 3
N_EXPERTS = 16
N_GROUPS = 4
MOE_TOPK = 2
MOE_ROW_BLOCK = 256
DEEPNORM_ALPHA = (2.0 * DEPTH) ** 0.25
LN_EPS = 1e-5
N_MOD_PER_LAYER = 6

HEAD_DIM = 64
MOBA_HEADS_PER_STEP = 4
BIAS_LANE = 32
LOG2E = 1.4426950408889634

LANES = 128
CONV_HALO = 16
NEG = -1e30
VMEM_LIMIT = 56 * 1024 * 1024

F32 = jnp.float32
BF16 = jnp.bfloat16


def _sigmoid(x):
    return 1.0 / (1.0 + jnp.exp(-x))


def _params(sem, vmem=VMEM_LIMIT):
    return pltpu.CompilerParams(dimension_semantics=sem, vmem_limit_bytes=vmem)


def _layer_norm(r, g, b):
    mu = jnp.mean(r, axis=-1, keepdims=True)
    d = r - mu
    var = jnp.mean(d * d, axis=-1, keepdims=True)
    return d * lax.rsqrt(var + LN_EPS) * g + b


def _ada_kernel(c_ref, w_ref, b_ref, o_ref):
    c = c_ref[...]
    s = c * _sigmoid(c)
    o_ref[...] = jnp.dot(s.astype(BF16), w_ref[...].astype(BF16),
                         preferred_element_type=F32) + b_ref[...]


def _ada_cond(c, w_ada, b_ada):
    B, D = c.shape
    N = w_ada.shape[1]
    tn = 2048 if N % 2048 == 0 else N
    cp = jnp.zeros((8, D), F32).at[:B].set(c)
    out = pl.pallas_call(
        _ada_kernel,
        grid=(N // tn,),
        in_specs=[pl.BlockSpec((8, D), lambda j: (0, 0)),
                  pl.BlockSpec((D, tn), lambda j: (0, j)),
                  pl.BlockSpec((1, tn), lambda j: (0, j))],
        out_specs=pl.BlockSpec((8, tn), lambda j: (0, j)),
        out_shape=jax.ShapeDtypeStruct((8, N), F32),
        compiler_params=_params(("arbitrary",)),
        name="ada_cond",
    )(cp, w_ada, b_ada.reshape(1, N))
    return out[:B]


def _inproj_kernel(x_ref, sh_ref, sc_ref, w_ref, xm_ref, z_ref):
    di = xm_ref.shape[-1]
    h = x_ref[...] * (1.0 + sc_ref[0]) + sh_ref[0]
    r = jnp.dot(h.astype(BF16), w_ref[...], preferred_element_type=F32)
    xm_ref[...] = r[:, :di].astype(BF16)
    z_ref[...] = r[:, di:].astype(BF16)


def _inproj(x2, shift, scale, w_in, S):
    T, D = x2.shape
    di = w_in.shape[1] // 2
    tm = min(512, S)
    tpb = S // tm
    return pl.pallas_call(
        _inproj_kernel,
        grid=(T // tm,),
        in_specs=[pl.BlockSpec((tm, D), lambda i: (i, 0)),
                  pl.BlockSpec((1, 1, D), lambda i: (i // tpb, 0, 0)),
                  pl.BlockSpec((1, 1, D), lambda i: (i // tpb, 0, 0)),
                  pl.BlockSpec((D, 2 * di), lambda i: (0, 0))],
        out_specs=[pl.BlockSpec((tm, di), lambda i: (i, 0)),
                   pl.BlockSpec((tm, di), lambda i: (i, 0))],
        out_shape=[jax.ShapeDtypeStruct((T, di), BF16)] * 2,
        compiler_params=_params(("parallel",)),
        name="mlstm_inproj",
    )(x2, shift, scale, w_in)


def _qkv_kernel(xm_ref, halo_ref, cw_ref, cb_ref, wq_ref, wk_ref, wv_ref, wif_ref, bif_ref,
                q_ref, k_ref, v_ref, xc_ref, g_ref, *, tpb, nh, kscale):
    tm, di = xm_ref.shape
    dh = di // nh
    i = pl.program_id(0)
    xm = xm_ref[...]
    xf = xm.astype(F32)
    halo = halo_ref[...].astype(F32)
    halo = jnp.where(i % tpb == 0, 0.0, halo)
    ext = jnp.concatenate([halo, xf], axis=0)
    acc = cb_ref[...] + cw_ref[CONV_WIDTH - 1:CONV_WIDTH, :] * xf
    for s in range(1, CONV_WIDTH):
        acc = acc + cw_ref[CONV_WIDTH - 1 - s:CONV_WIDTH - s, :] * ext[CONV_HALO - s:CONV_HALO - s + tm, :]
    xc = acc * _sigmoid(acc)
    xcb = xc.astype(BF16)
    xc_ref[...] = xcb
    for h in range(nh):
        sl = slice(h * dh, (h + 1) * dh)
        q_ref[:, sl] = jnp.dot(xcb[:, sl], wq_ref[h], preferred_element_type=F32).astype(BF16)
        k_ref[:, sl] = (jnp.dot(xcb[:, sl], wk_ref[h], preferred_element_type=F32) * kscale).astype(BF16)
        v_ref[:, sl] = jnp.dot(xm[:, sl], wv_ref[h], preferred_element_type=F32).astype(BF16)
    g = jnp.dot(xcb, wif_ref[...], preferred_element_type=F32) + bif_ref[...]
    col = lax.broadcasted_iota(jnp.int32, g.shape, 1)
    logsig = jnp.minimum(g, 0.0) - jnp.log(1.0 + jnp.exp(-jnp.abs(g)))
    g = jnp.where(col >= nh, logsig, g)
    g_ref[...] = g[:, :2 * nh]


def _qkv(xm, conv_w, conv_b, wq, wk, wv, w_if, b_if, S):
    T, di = xm.shape
    nh = wq.shape[0]
    dh = di // nh
    tm = min(512, S)
    tpb = S // tm
    hb = tm // CONV_HALO
    wif = jnp.zeros((di, LANES), BF16).at[:, :2 * nh].set(w_if.astype(BF16))
    bif = jnp.zeros((1, LANES), F32).at[0, :2 * nh].set(b_if)
    kern = functools.partial(_qkv_kernel, tpb=tpb, nh=nh, kscale=float(dh) ** -0.5)
    full2 = lambda i: (0, 0)
    full3 = lambda i: (0, 0, 0)
    row = lambda i: (i, 0)
    return pl.pallas_call(
        kern,
        grid=(T // tm,),
        in_specs=[pl.BlockSpec((tm, di), row),
                  pl.BlockSpec((CONV_HALO, di), lambda i: (jnp.maximum(i * hb - 1, 0), 0)),
                  pl.BlockSpec((CONV_WIDTH, di), full2),
                  pl.BlockSpec((1, di), full2),
                  pl.BlockSpec((nh, dh, dh), full3),
                  pl.BlockSpec((nh, dh, dh), full3),
                  pl.BlockSpec((nh, dh, dh), full3),
                  pl.BlockSpec((di, LANES), full2),
                  pl.BlockSpec((1, LANES), full2)],
        out_specs=[pl.BlockSpec((tm, di), row)] * 4 + [pl.BlockSpec((tm, 2 * nh), row)],
        out_shape=[jax.ShapeDtypeStruct((T, di), BF16)] * 4 + [jax.ShapeDtypeStruct((T, 2 * nh), F32)],
        compiler_params=_params(("parallel",)),
        name="mlstm_qkv",
    )(xm, xm, conv_w, conv_b.reshape(1, di), wq, wk, wv, wif, bif)


def _mlstm_kernel(q_ref, k_ref, v_ref, gc_ref, gr_ref, o_ref, c_scr, n_scr, m_scr):
    L = q_ref.shape[0]

    @pl.when(pl.program_id(2) == 0)
    def _():
        c_scr[...] = jnp.zeros_like(c_scr)
        n_scr[...] = jnp.zeros_like(n_scr)
        m_scr[...] = jnp.zeros_like(m_scr)

    q = q_ref[...]
    k = k_ref[...]
    v = v_ref[...]
    gc = gc_ref[0, 0]
    gr = gr_ref[0, 0]
    i_col, f_col = gc[:, 0:1], gc[:, 1:2]
    i_row, f_row = gr[0:1, :], gr[1:2, :]
    t_idx = lax.broadcasted_iota(jnp.int32, (L, L), 0)
    s_idx = lax.broadcasted_iota(jnp.int32, (L, L), 1)
    causal = s_idx <= t_idx
    b_col = jnp.sum(jnp.where(causal, f_row, 0.0), axis=1, keepdims=True)
    b_row = jnp.sum(jnp.where(t_idx <= s_idx, f_col, 0.0), axis=0, keepdims=True)
    m_prev = m_scr[...]
    d = jnp.where(causal, b_col - b_row + i_row, -jnp.inf)
    a_col = b_col + m_prev
    m_t = jnp.maximum(a_col, jnp.max(d, axis=1, keepdims=True))
    w_intra = jnp.exp(d - m_t)
    w_inter = jnp.exp(a_col - m_t)
    qk = lax.dot_general(q, k, (((1,), (1,)), ((), ())), preferred_element_type=F32)
    s_mat = qk * w_intra
    c_b = c_scr[...].astype(BF16)
    inter = lax.dot_general(q, c_b, (((1,), (1,)), ((), ())), preferred_element_type=F32)
    num = jnp.dot(s_mat.astype(BF16), v, preferred_element_type=F32) + w_inter * inter
    qn = jnp.sum(q.astype(F32) * n_scr[...], axis=1, keepdims=True)
    den = jnp.sum(s_mat, axis=1, keepdims=True) + w_inter * qn
    hcap = num / jnp.maximum(jnp.abs(den), jnp.exp(-m_t))
    mu = jnp.mean(hcap, axis=1, keepdims=True)
    dv = hcap - mu
    var = jnp.mean(dv * dv, axis=1, keepdims=True)
    o_ref[...] = (dv * lax.rsqrt(var + LN_EPS)).astype(BF16)

    b_last = jnp.sum(f_row, axis=1, keepdims=True)
    ws_col = b_last - b_col + i_col
    m_new = jnp.maximum(b_last + m_prev, jnp.max(ws_col, axis=0, keepdims=True))
    decay = jnp.exp(b_last + m_prev - m_new)
    ws = jnp.exp(ws_col - m_new)
    vw = (v.astype(F32) * ws).astype(BF16)
    upd = lax.dot_general(vw, k, (((0,), (0,)), ((), ())), preferred_element_type=F32)
    c_scr[...] = decay * c_scr[...] + upd
    n_scr[...] = decay * n_scr[...] + jnp.sum(k.astype(F32) * ws, axis=0, keepdims=True)
    m_scr[...] = m_new


def _mlstm(q, k, v, gcol, grow, B, S):
    T, di = q.shape
    nh = gcol.shape[1]
    dh = di // nh
    L = MLSTM_CHUNK
    nc = S // L
    blk = pl.BlockSpec((L, dh), lambda b, h, c: (b * nc + c, h))
    return pl.pallas_call(
        _mlstm_kernel,
        grid=(B, nh, nc),
        in_specs=[blk, blk, blk,
                  pl.BlockSpec((1, 1, L, 2), lambda b, h, c: (b, h, c, 0)),
                  pl.BlockSpec((1, 1, 2, L), lambda b, h, c: (b, h, 0, c))],
        out_specs=blk,
        out_shape=jax.ShapeDtypeStruct((T, di), BF16),
        scratch_shapes=[pltpu.VMEM((dh, dh), F32), pltpu.VMEM((1, dh), F32), pltpu.VMEM((1, 1), F32)],
        compiler_params=_params(("parallel", "parallel", "arbitrary")),
        name="mlstm_scan",
    )(q, k, v, gcol, grow)


def _mlstm_out_kernel(hn_ref, xc_ref, z_ref, x_ref, gn_ref, skip_ref, w_ref, gate_ref, lg_ref, lb_ref, o_ref):
    z = z_ref[...].astype(F32)
    u = (hn_ref[...].astype(F32) * gn_ref[...] + skip_ref[...] * xc_ref[...].astype(F32)) * (z * _sigmoid(z))
    y = jnp.dot(u.astype(BF16), w_ref[...], preferred_element_type=F32)
    r = DEEPNORM_ALPHA * x_ref[...] + gate_ref[0] * y
    o_ref[...] = _layer_norm(r, lg_ref[...], lb_ref[...])


def _mlstm_out(hn, xc, z, x2, gn_g, skip, w_out, gate, ln_g, ln_b, S):
    T, D = x2.shape
    di = hn.shape[1]
    tm = min(512, S)
    tpb = S // tm
    row = lambda i: (i, 0)
    full2 = lambda i: (0, 0)
    return pl.pallas_call(
        _mlstm_out_kernel,
        grid=(T // tm,),
        in_specs=[pl.BlockSpec((tm, di), row), pl.BlockSpec((tm, di), row), pl.BlockSpec((tm, di), row),
                  pl.BlockSpec((tm, D), row),
                  pl.BlockSpec((1, di), full2), pl.BlockSpec((1, di), full2),
                  pl.BlockSpec((di, D), full2),
                  pl.BlockSpec((1, 1, D), lambda i: (i // tpb, 0, 0)),
                  pl.BlockSpec((1, D), full2), pl.BlockSpec((1, D), full2)],
        out_specs=pl.BlockSpec((tm, D), row),
        out_shape=jax.ShapeDtypeStruct((T, D), F32),
        compiler_params=_params(("parallel",)),
        name="mlstm_out_ln",
    )(hn, xc, z, x2, gn_g.reshape(1, di), skip.reshape(1, di), w_out, gate, ln_g.reshape(1, D), ln_b.reshape(1, D))


def _route_kernel(x_ref, sh_ref, sc_ref, wrt_ref, br_ref, tri_ref, out_ref, cnt_ref, carry):
    @pl.when(pl.program_id(0) == 0)
    def _():
        carry[...] = jnp.zeros_like(carry)

    E = wrt_ref.shape[0]
    epg = E // N_GROUPS
    h = x_ref[...] * (1.0 + sc_ref[0]) + sh_ref[0]
    lt = lax.dot_general(wrt_ref[...], h, (((1,), (1,)), ((), ())),
                         precision=lax.Precision.HIGHEST, preferred_element_type=F32)
    ex = jnp.exp(lt - jnp.max(lt, axis=0, keepdims=True))
    probs = ex / jnp.sum(ex, axis=0, keepdims=True)
    sel = probs + br_ref[...]
    srow = [sel[e:e + 1, :] for e in range(E)]
    prow = [probs[e:e + 1, :] for e in range(E)]
    gscore = []
    for g in range(N_GROUPS):
        r = srow[g * epg:(g + 1) * epg]
        best = None
        for a in range(epg):
            for b in range(a + 1, epg):
                pair = r[a] + r[b]
                best = pair if best is None else jnp.maximum(best, pair)
        gscore.append(best)
    gmax = functools.reduce(jnp.maximum, gscore)
    chosen, taken = [], None
    for g in range(N_GROUPS):
        c = gscore[g] == gmax
        if taken is not None:
            c = c & jnp.logical_not(taken)
        taken = c if taken is None else (taken | c)
        chosen.append(c)
    zero = jnp.zeros_like(srow[0])
    e1 = zero
    e2 = zero
    p1 = zero
    p2 = zero
    firsts, seconds = [], []
    for e in range(E):
        g = e // epg
        rank = zero
        for o in range(g * epg, (g + 1) * epg):
            if o == e:
                continue
            beats = (srow[o] > srow[e]) | ((srow[o] == srow[e]) & (o < e))
            rank = rank + jnp.where(beats, 1.0, 0.0)
        is1 = chosen[g] & (rank == 0.0)
        is2 = chosen[g] & (rank == 1.0)
        firsts.append(is1)
        seconds.append(is2)
        e1 = e1 + jnp.where(is1, float(e), 0.0)
        e2 = e2 + jnp.where(is2, float(e), 0.0)
        p1 = p1 + jnp.where(is1, prow[e], 0.0)
        p2 = p2 + jnp.where(is2, prow[e], 0.0)
    mask = jnp.concatenate([jnp.where(firsts[e] | seconds[e], 1.0, 0.0) for e in range(E)], axis=0)
    prefix = jnp.dot(mask.astype(BF16), tri_ref[...], preferred_element_type=F32) + carry[...]
    r1 = zero
    r2 = zero
    for e in range(E):
        pe = prefix[e:e + 1, :]
        r1 = r1 + jnp.where(firsts[e], pe, 0.0)
        r2 = r2 + jnp.where(seconds[e], pe, 0.0)
    psum = p1 + p2
    out_ref[...] = jnp.concatenate([e1, e2, p1 / psum, p2 / psum, r1, r2, zero, zero], axis=0)
    carry[...] = carry[...] + jnp.sum(mask, axis=1, keepdims=True)
    cnt_ref[...] = jnp.broadcast_to(carry[...], cnt_ref.shape)


def _route(x2, shift, scale, w_router, b_router, S):
    T, D = x2.shape
    E = w_router.shape[1]
    tm = min(512, S)
    tpb = S // tm
    tri = (jnp.arange(tm)[:, None] < jnp.arange(tm)[None, :]).astype(BF16)
    return pl.pallas_call(
        _route_kernel,
        grid=(T // tm,),
        in_specs=[pl.BlockSpec((tm, D), lambda i: (i, 0)),
                  pl.BlockSpec((1, 1, D), lambda i: (i // tpb, 0, 0)),
                  pl.BlockSpec((1, 1, D), lambda i: (i // tpb, 0, 0)),
                  pl.BlockSpec((E, D), lambda i: (0, 0)),
                  pl.BlockSpec((E, 1), lambda i: (0, 0)),
                  pl.BlockSpec((tm, tm), lambda i: (0, 0))],
        out_specs=[pl.BlockSpec((8, tm), lambda i: (0, i)),
                   pl.BlockSpec((E, LANES), lambda i: (0, 0))],
        out_shape=[jax.ShapeDtypeStruct((8, T), F32), jax.ShapeDtypeStruct((E, LANES), F32)],
        scratch_shapes=[pltpu.VMEM((E, 1), F32)],
        compiler_params=_params(("arbitrary",)),
        name="moe_route",
    )(x2, shift, scale, w_router.T, b_router.reshape(E, 1), tri)


def _dispatch_kernel(d1_ref, d2_ref, x_ref, sh_ref, sc_ref, xs_in_ref, xs_ref, hbuf, sem):
    del xs_in_ref
    tr = x_ref.shape[0]
    base = pl.program_id(0) * tr
    hbuf[...] = x_ref[...] * (1.0 + sc_ref[0]) + sh_ref[0]

    def row_copies(r):
        src = hbuf.at[pl.ds(r, 1)]
        return (pltpu.make_async_copy(src, xs_ref.at[pl.ds(d1_ref[base + r], 1)], sem),
                pltpu.make_async_copy(src, xs_ref.at[pl.ds(d2_ref[base + r], 1)], sem))

    def start(r, c):
        a, b = row_copies(r)
        a.start(priority=0)
        b.start(priority=1)
        return c

    def wait(r, c):
        a, b = row_copies(r)
        a.wait()
        b.wait()
        return c

    lax.fori_loop(0, tr, start, 0, unroll=8)
    lax.fori_loop(0, tr, wait, 0)


def _dispatch(x2, shift, scale, d1, d2, cap, S):
    T, D = x2.shape
    tr = min(256, S)
    tpb = S // tr
    grid_spec = pltpu.PrefetchScalarGridSpec(
        num_scalar_prefetch=2,
        grid=(T // tr,),
        in_specs=[pl.BlockSpec((tr, D), lambda i, d1, d2: (i, 0)),
                  pl.BlockSpec((1, 1, D), lambda i, d1, d2: (i // tpb, 0, 0)),
                  pl.BlockSpec((1, 1, D), lambda i, d1, d2: (i // tpb, 0, 0)),
                  pl.BlockSpec(memory_space=pl.ANY)],
        out_specs=pl.BlockSpec(memory_space=pl.ANY),
        scratch_shapes=[pltpu.VMEM((tr, D), F32), pltpu.SemaphoreType.DMA(())],
    )
    return pl.pallas_call(
        _dispatch_kernel,
        grid_spec=grid_spec,
        out_shape=jax.ShapeDtypeStruct((cap, D), F32),
        input_output_aliases={5: 0},
        compiler_params=_params(("arbitrary",)),
        name="moe_dispatch",
    )(d1, d2, x2, shift, scale, jnp.zeros((cap, D), F32))


def _expert_kernel(be_ref, na_ref, xs_ref, wg_ref, wu_ref, wd_ref, ys_ref):
    del be_ref

    @pl.when(pl.program_id(0) < na_ref[0])
    def _():
        xb = xs_ref[...].astype(BF16)
        g = jnp.dot(xb, wg_ref[0], preferred_element_type=F32)
        u = jnp.dot(xb, wu_ref[0], preferred_element_type=F32)
        a = (g * _sigmoid(g)) * u
        ys_ref[...] = jnp.dot(a.astype(BF16), wd_ref[0], preferred_element_type=F32)

    @pl.when(pl.program_id(0) >= na_ref[0])
    def _():
        ys_ref[...] = jnp.zeros_like(ys_ref)


def _experts(xs, block_expert, n_active, w_gate, w_up, w_down):
    cap, D = xs.shape
    E, _, F = w_gate.shape
    rb = MOE_ROW_BLOCK
    nb = cap // rb

    def blk(i, be, na):
        return jnp.minimum(i, na[0] - 1)

    grid_spec = pltpu.PrefetchScalarGridSpec(
        num_scalar_prefetch=2,
        grid=(nb,),
        in_specs=[pl.BlockSpec((rb, D), lambda i, be, na: (blk(i, be, na), 0)),
                  pl.BlockSpec((1, D, F), lambda i, be, na: (be[blk(i, be, na)], 0, 0)),
                  pl.BlockSpec((1, D, F), lambda i, be, na: (be[blk(i, be, na)], 0, 0)),
                  pl.BlockSpec((1, F, D), lambda i, be, na: (be[blk(i, be, na)], 0, 0))],
        out_specs=pl.BlockSpec((rb, D), lambda i, be, na: (i, 0)),
    )
    return pl.pallas_call(
        _expert_kernel,
        grid_spec=grid_spec,
        out_shape=jax.ShapeDtypeStruct((cap, D), F32),
        compiler_params=_params(("arbitrary",)),
        name="moe_experts",
    )(block_expert, n_active, xs, w_gate, w_up, w_down)


def _combine_kernel(d1_ref, d2_ref, ys_ref, rt_ref, x_ref, gate_ref, lg_ref, lb_ref, o_ref, buf, sem):
    tr = x_ref.shape[0]
    base = pl.program_id(0) * tr

    def row_copies(r):
        return (pltpu.make_async_copy(ys_ref.at[pl.ds(d1_ref[base + r], 1)], buf.at[0, pl.ds(r, 1)], sem),
                pltpu.make_async_copy(ys_ref.at[pl.ds(d2_ref[base + r], 1)], buf.at[1, pl.ds(r, 1)], sem))

    def start(r, c):
        a, b = row_copies(r)
        a.start(priority=0)
        b.start(priority=1)
        return c

    def wait(r, c):
        a, b = row_copies(r)
        a.wait()
        b.wait()
        return c

    lax.fori_loop(0, tr, start, 0, unroll=8)
    lax.fori_loop(0, tr, wait, 0)
    rt = rt_ref[...]
    y = rt[:, 2:3] * buf[0] + rt[:, 3:4] * buf[1]
    r = DEEPNORM_ALPHA * x_ref[...] + gate_ref[0] * y
    o_ref[...] = _layer_norm(r, lg_ref[...], lb_ref[...])


def _combine(ys, route_t, x2, gate, ln_g, ln_b, d1, d2, S):
    T, D = x2.shape
    tr = min(256, S)
    tpb = S // tr
    grid_spec = pltpu.PrefetchScalarGridSpec(
        num_scalar_prefetch=2,
        grid=(T // tr,),
        in_specs=[pl.BlockSpec(memory_space=pl.ANY),
                  pl.BlockSpec((tr, 8), lambda i, d1, d2: (i, 0)),
                  pl.BlockSpec((tr, D), lambda i, d1, d2: (i, 0)),
                  pl.BlockSpec((1, 1, D), lambda i, d1, d2: (i // tpb, 0, 0)),
                  pl.BlockSpec((1, D), lambda i, d1, d2: (0, 0)),
                  pl.BlockSpec((1, D), lambda i, d1, d2: (0, 0))],
        out_specs=pl.BlockSpec((tr, D), lambda i, d1, d2: (i, 0)),
        scratch_shapes=[pltpu.VMEM((2, tr, D), F32), pltpu.SemaphoreType.DMA(())],
    )
    return pl.pallas_call(
        _combine_kernel,
        grid_spec=grid_spec,
        out_shape=jax.ShapeDtypeStruct((T, D), F32),
        compiler_params=_params(("arbitrary",)),
        name="moe_combine_ln",
    )(d1, d2, ys, route_t, x2, gate, ln_g.reshape(1, D), ln_b.reshape(1, D))


def _moe_layer(x2, shift, scale, gate, ln_g, ln_b, w_router, b_router, w_gate, w_up, w_down, S):
    T, D = x2.shape
    E = w_router.shape[1]
    rb = MOE_ROW_BLOCK
    route, cnt = _route(x2, shift, scale, w_router, b_router, S)
    counts = cnt[:, 0].astype(jnp.int32)
    padded = (counts + rb - 1) // rb * rb
    pends = jnp.cumsum(padded)
    pstarts = pends - padded
    e1 = route[0].astype(jnp.int32)
    e2 = route[1].astype(jnp.int32)
    d1 = pstarts[e1] + route[4].astype(jnp.int32)
    d2 = pstarts[e2] + route[5].astype(jnp.int32)
    nb = -(-(T * MOE_TOPK) // rb) + E
    cap = nb * rb
    block_start = jnp.arange(nb, dtype=jnp.int32) * rb
    block_expert = jnp.minimum(jnp.sum(block_start[:, None] >= pends[None, :], axis=1), E - 1).astype(jnp.int32)
    n_active = (pends[-1:] // rb).astype(jnp.int32)
    xs = _dispatch(x2, shift, scale, d1, d2, cap, S)
    ys = _experts(xs, block_expert, n_active, w_gate, w_up, w_down)
    return _combine(ys, route.T, x2, gate, ln_g, ln_b, d1, d2, S)


def _moba_proj_kernel(x_ref, shq_ref, scq_ref, shkv_ref, sckv_ref, wq_ref, wkv_ref,
                      q_ref, k_ref, vt_ref, km_ref, *, qscale, tpb, slopes):
    tm, D = x_ref.shape
    hd = HEAD_DIM
    nh = D // hd
    x = x_ref[...]
    hq = x * (1.0 + scq_ref[0]) + shq_ref[0]
    hkv = x * (1.0 + sckv_ref[0]) + shkv_ref[0]
    qf = jnp.dot(hq.astype(BF16), wq_ref[...], preferred_element_type=F32) * qscale
    kv = jnp.dot(hkv.astype(BF16), wkv_ref[...], preferred_element_type=F32)
    kf = kv[:, :D]
    vt_ref[0] = kv[:, D:].T.astype(BF16)
    lane = lax.broadcasted_iota(jnp.int32, (1, LANES), 1)
    head_lanes = lane >= hd
    pos = (pl.program_id(0) % tpb) * tm + lax.broadcasted_iota(jnp.int32, (tm, 1), 0)
    pos_f = pos.astype(F32)
    onehot = jnp.where(lane == pos // MOBA_BLOCK, 1.0, 0.0)
    q_fill = jnp.where((lane >= BIAS_LANE) & (lane < BIAS_LANE + 3), 1.0, 0.0)
    nbt = tm // MOBA_BLOCK
    means = [jnp.mean(kf[j * MOBA_BLOCK:(j + 1) * MOBA_BLOCK, :], axis=0, keepdims=True) for j in range(nbt)]
    km = jnp.concatenate(means + [jnp.zeros((8 - nbt, D), F32)], axis=0)
    for j in range(nh // 2):
        cs = slice(j * LANES, (j + 1) * LANES)
        qb, kb, mb = qf[:, cs], kf[:, cs], km[:, cs]
        q_pair = (pltpu.roll(qb, hd, axis=1), qb)
        k_pair = (pltpu.roll(kb, hd, axis=1), kb)
        m_pair = (pltpu.roll(mb, hd, axis=1), mb)
        for e in range(2):
            g = 2 * j + e
            gs = slice(g * LANES, (g + 1) * LANES)
            q_ref[:, gs] = jnp.where(head_lanes, q_pair[e], q_fill).astype(BF16)
            c = (slopes[g] * LOG2E) * pos_f
            hi = c.astype(BF16).astype(F32)
            mid = (c - hi).astype(BF16).astype(F32)
            lo = c - hi - mid
            aug = jnp.where(lane == BIAS_LANE, hi,
                            jnp.where(lane == BIAS_LANE + 1, mid,
                                      jnp.where(lane == BIAS_LANE + 2, lo, onehot)))
            k_ref[:, gs] = jnp.where(head_lanes, k_pair[e], aug).astype(BF16)
            mg = jnp.where(head_lanes, m_pair[e], 0.0)
            for jb in range(nbt):
                km_ref[jb, :, gs] = mg[jb:jb + 1, :]


def _moba_proj(x2, shq, scq, shkv, sckv, wq, wkv, B, S):
    T, D = x2.shape
    nh = D // HEAD_DIM
    tm = min(512, S)
    tpb = S // tm
    nbt = tm // MOBA_BLOCK
    assert S // MOBA_BLOCK <= BIAS_LANE and nbt <= 8
    slopes = tuple(2.0 ** (-8.0 * (h + 1.0) / nh) for h in range(nh))
    mod = pl.BlockSpec((1, 1, D), lambda i: (i // tpb, 0, 0))
    kern = functools.partial(_moba_proj_kernel, qscale=float(HEAD_DIM) ** -0.5 * LOG2E, tpb=tpb, slopes=slopes)
    return pl.pallas_call(
        kern,
        grid=(T // tm,),
        in_specs=[pl.BlockSpec((tm, D), lambda i: (i, 0)), mod, mod, mod, mod,
                  pl.BlockSpec((D, D), lambda i: (0, 0)),
                  pl.BlockSpec((D, 2 * D), lambda i: (0, 0))],
        out_specs=[pl.BlockSpec((tm, nh * LANES), lambda i: (i, 0)),
                   pl.BlockSpec((tm, nh * LANES), lambda i: (i, 0)),
                   pl.BlockSpec((1, D, tm), lambda i: (i // tpb, 0, i % tpb)),
                   pl.BlockSpec((nbt, 1, nh * LANES), lambda i: (i, 0, 0))],
        out_shape=[jax.ShapeDtypeStruct((T, nh * LANES), BF16), jax.ShapeDtypeStruct((T, nh * LANES), BF16),
                   jax.ShapeDtypeStruct((B, D, S), BF16),
                   jax.ShapeDtypeStruct((T // MOBA_BLOCK, 1, nh * LANES), F32)],
        compiler_params=_params(("parallel",)),
        name="moba_proj",
    )(x2, shq, scq, shkv, sckv, wq, wkv)


def _moba_kernel(q_ref, k_ref, vt_ref, km_ref, o_ref, qat_scr, st_scr, p_scr, alpha_scr, m_scr, l_scr, acc_scr):
    BLK = q_ref.shape[0]
    G = q_ref.shape[1] // LANES
    nblk = km_ref.shape[1]
    hd = HEAD_DIM
    nhalf = BLK // LANES
    own = pl.program_id(2)
    blk_id = lax.broadcasted_iota(jnp.int32, (nblk, BLK), 0)
    k_off = lax.broadcasted_iota(jnp.int32, (BLK, 1), 0)
    q_off = lax.broadcasted_iota(jnp.int32, (1, LANES), 1)

    for g in range(G):
        gs = slice(g * LANES, (g + 1) * LANES)
        qgt = q_ref[:, gs].astype(F32).T
        gate = jnp.dot(km_ref[0, :, gs], qgt, precision=lax.Precision.HIGHEST,
                       preferred_element_type=F32)
        gsc = jnp.where(blk_id < own, gate, -jnp.inf)
        picked = blk_id < 0
        for _ in range(MOBA_TOPK):
            mx = jnp.max(gsc, axis=0, keepdims=True)
            first = jnp.min(jnp.where(gsc == mx, blk_id, nblk), axis=0, keepdims=True)
            pick = blk_id == first
            picked = picked | pick
            gsc = jnp.where(pick, -jnp.inf, gsc)
        visible = (picked & (blk_id < own)) | (blk_id == own)
        pen = jnp.where(visible, 0.0, NEG)
        qat_scr[g] = (qgt + jnp.concatenate([pen, jnp.zeros((LANES - nblk, BLK), F32)], axis=0)).astype(BF16)

    m_scr[...] = jnp.full(m_scr.shape, NEG, F32)
    l_scr[...] = jnp.zeros_like(l_scr)
    acc_scr[...] = jnp.zeros_like(acc_scr)

    def scores(n, slot):
        start = pl.multiple_of(n * BLK, BLK)
        for g in range(G):
            kn = k_ref[pl.ds(start, BLK), g * LANES:(g + 1) * LANES]
            st_scr[slot, g] = jnp.dot(kn, qat_scr[g], preferred_element_type=F32)

    def softmax(slot, causal):
        for g in range(G):
            for hf in range(nhalf):
                c = g * nhalf + hf
                ls = slice(hf * LANES, (hf + 1) * LANES)
                st = st_scr[slot, g, :, ls]
                if causal:
                    st = jnp.where(k_off <= q_off + hf * LANES, st, NEG)
                m = m_scr[c]
                m_new = jnp.maximum(m, jnp.max(st, axis=0, keepdims=True))
                alpha = jnp.exp2(m - m_new)
                p = jnp.exp2(st - m_new)
                l_scr[c] = alpha * l_scr[c] + jnp.sum(p, axis=0, keepdims=True)
                p_scr[slot, g, :, ls] = p.astype(BF16)
                alpha_scr[slot, g, :, ls] = alpha
                m_scr[c] = m_new

    def accumulate(n, slot):
        start = pl.multiple_of(n * BLK, BLK)
        for g in range(G):
            vtn = vt_ref[0, g * hd:(g + 1) * hd, pl.ds(start, BLK)]
            pv = jnp.dot(vtn, p_scr[slot, g], preferred_element_type=F32)
            acc_scr[g] = alpha_scr[slot, g] * acc_scr[g] + pv

    scores(0, 0)

    def step(n, carry):
        slot = n % 2
        softmax(slot, False)
        accumulate(n, slot)
        scores(n + 1, 1 - slot)
        return carry

    lax.fori_loop(0, own, step, 0)
    last = own % 2
    softmax(last, True)
    accumulate(own, last)
    for g in range(G):
        inv = jnp.concatenate([1.0 / l_scr[g * nhalf + hf] for hf in range(nhalf)], axis=1)
        o_ref[0, g * hd:(g + 1) * hd, :] = (acc_scr[g] * inv).astype(BF16)


def _moba_attn(q, k, vt, km, B, S):
    T = q.shape[0]
    nh = q.shape[1] // LANES
    G = MOBA_HEADS_PER_STEP
    BLK = MOBA_BLOCK
    nblk = S // BLK
    hd = HEAD_DIM
    return pl.pallas_call(
        _moba_kernel,
        grid=(B, nh // G, nblk),
        in_specs=[pl.BlockSpec((BLK, G * LANES), lambda b, hp, qi: (b * nblk + qi, hp)),
                  pl.BlockSpec((S, G * LANES), lambda b, hp, qi: (b, hp)),
                  pl.BlockSpec((1, G * hd, S), lambda b, hp, qi: (b, hp, 0)),
                  pl.BlockSpec((1, nblk, G * LANES), lambda b, hp, qi: (b, 0, hp))],
        out_specs=pl.BlockSpec((1, G * hd, BLK), lambda b, hp, qi: (b, hp, qi)),
        out_shape=jax.ShapeDtypeStruct((B, nh * hd, S), BF16),
        scratch_shapes=[pltpu.VMEM((G, LANES, BLK), BF16),
                        pltpu.VMEM((2, G, BLK, BLK), F32),
                        pltpu.VMEM((2, G, BLK, BLK), BF16),
                        pltpu.VMEM((2, G, 1, BLK), F32),
                        pltpu.VMEM((G * (BLK // LANES), 1, LANES), F32),
                        pltpu.VMEM((G * (BLK // LANES), 1, LANES), F32),
                        pltpu.VMEM((G, hd, BLK), F32)],
        compiler_params=_params(("parallel", "parallel", "arbitrary")),
        name="moba_attn",
    )(q, k, vt, km)


def _attn_out_kernel(a_ref, x_ref, w_ref, gate_ref, lg_ref, lb_ref, o_ref):
    y = lax.dot_general(a_ref[0], w_ref[...], (((0,), (0,)), ((), ())), preferred_element_type=F32)
    r = DEEPNORM_ALPHA * x_ref[...] + gate_ref[0] * y
    o_ref[...] = _layer_norm(r, lg_ref[...], lb_ref[...])


def _attn_out(at, x2, w, gate, ln_g, ln_b, S):
    T, D = x2.shape
    K = at.shape[1]
    tm = min(512, S)
    tpb = S // tm
    row = lambda i: (i, 0)
    full2 = lambda i: (0, 0)
    return pl.pallas_call(
        _attn_out_kernel,
        grid=(T // tm,),
        in_specs=[pl.BlockSpec((1, K, tm), lambda i: (i // tpb, 0, i % tpb)), pl.BlockSpec((tm, D), row),
                  pl.BlockSpec((K, D), full2),
                  pl.BlockSpec((1, 1, D), lambda i: (i // tpb, 0, 0)),
                  pl.BlockSpec((1, D), full2), pl.BlockSpec((1, D), full2)],
        out_specs=pl.BlockSpec((tm, D), row),
        out_shape=jax.ShapeDtypeStruct((T, D), F32),
        compiler_params=_params(("parallel",)),
        name="attn_out_ln",
    )(at, x2, w, gate, ln_g.reshape(1, D), ln_b.reshape(1, D))


def kernel(x, c, w_ada, b_ada, ln_g, ln_b, a_w_in, a_conv_w, a_conv_b, a_wq, a_wk, a_wv, a_w_if, a_b_if,
           a_gn_g, a_skip, a_w_out, b_w_kv, b_wq, b_wo, moe_w_router, moe_b_router, moe_w_gate, moe_w_up,
           moe_w_down):
    B, S, D = x.shape
    T = B * S
    assert DEPTH == 2 and S % MOBA_BLOCK == 0 and S % MLSTM_CHUNK == 0
    n_layer_mod = DEPTH * N_MOD_PER_LAYER * D
    cond = _ada_cond(c, w_ada, b_ada)
    mods = cond[:, :n_layer_mod].reshape(B, DEPTH, 2, 3, 1, D)
    kv_mod = cond[:, n_layer_mod:].reshape(B, 2, 1, D)

    def mod3(layer, sub):
        m = mods[:, layer, sub]
        return m[:, 0], m[:, 1], 1.0 + m[:, 2]

    xf = x.reshape(T, D)

    shift, scale, gate = mod3(0, 0)
    nh = MLSTM_HEADS
    xm, z = _inproj(xf, shift, scale, a_w_in[0].astype(BF16), S)
    q, k, v, xc, gts = _qkv(xm, a_conv_w[0], a_conv_b[0], a_wq[0].astype(BF16), a_wk[0].astype(BF16),
                            a_wv[0].astype(BF16), a_w_if[0], a_b_if[0], S)
    g4 = gts.reshape(B, S, 2, nh)
    gcol = jnp.transpose(g4, (0, 3, 1, 2))
    grow = jnp.transpose(g4, (0, 3, 2, 1))
    hn = _mlstm(q, k, v, gcol, grow, B, S)
    xf = _mlstm_out(hn, xc, z, xf, a_gn_g[0], a_skip[0], a_w_out[0].astype(BF16), gate, ln_g[0, 0], ln_b[0, 0], S)
    shift, scale, gate = mod3(0, 1)
    xf = _moe_layer(xf, shift, scale, gate, ln_g[0, 1], ln_b[0, 1], moe_w_router, moe_b_router,
                    moe_w_gate[0].astype(BF16), moe_w_up[0].astype(BF16), moe_w_down[0].astype(BF16), S)

    shift, scale, gate = mod3(1, 0)
    q, k, vt, km = _moba_proj(xf, shift, scale, kv_mod[:, 0], kv_mod[:, 1], b_wq[0].astype(BF16),
                              b_w_kv.astype(BF16), B, S)
    km = km.reshape(B, S // MOBA_BLOCK, km.shape[-1])
    attn_t = _moba_attn(q, k, vt, km, B, S)
    xf = _attn_out(attn_t, xf, b_wo[0].astype(BF16), gate, ln_g[1, 0], ln_b[1, 0], S)
    shift, scale, gate = mod3(1, 1)
    xf = _moe_layer(xf, shift, scale, gate, ln_g[1, 1], ln_b[1, 1], moe_w_router, moe_b_router,
                    moe_w_gate[1].astype(BF16), moe_w_up[1].astype(BF16), moe_w_down[1].astype(BF16), S)
    return xf.reshape(B, S, D)
```

```python
import functools

import jax
import jax.numpy as jnp
from jax import lax
from jax.experimental import pallas as pl
from jax.experimental.pallas import tpu as pltpu

DEPTH = 2
MLSTM_HEADS = 4
CONV_WIDTH = 4
MLSTM_CHUNK = 128
ATTN_HEADS = 16
MOBA_BLOCK = 256
MOBA_TOPK = 3
N_EXPERTS = 16
N_GROUPS = 4
MOE_TOPK = 2
MOE_ROW_BLOCK = 256
DEEPNORM_ALPHA = (2.0 * DEPTH) ** 0.25
LN_EPS = 1e-5
N_MOD_PER_LAYER = 6

HEAD_DIM = 64
V_ROWS = 80
MOBA_HEADS_PER_STEP = 4
BIAS_LANE = 32
LOG2E = 1.4426950408889634

LANES = 128
CONV_HALO = 16
NEG = -1e30
VMEM_LIMIT = 56 * 1024 * 1024

F32 = jnp.float32
BF16 = jnp.bfloat16


def _sigmoid(x):
    return 1.0 / (1.0 + jnp.exp(-x))


def _params(sem, vmem=VMEM_LIMIT):
    return pltpu.CompilerParams(dimension_semantics=sem, vmem_limit_bytes=vmem)


def _layer_norm(r, g, b):
    mu = jnp.mean(r, axis=-1, keepdims=True)
    d = r - mu
    var = jnp.mean(d * d, axis=-1, keepdims=True)
    return d * lax.rsqrt(var + LN_EPS) * g + b


def _ada_kernel(c_ref, w_ref, b_ref, o_ref):
    c = c_ref[...]
    s = c * _sigmoid(c)
    o_ref[...] = jnp.dot(s.astype(BF16), w_ref[...].astype(BF16),
                         preferred_element_type=F32) + b_ref[...]


def _ada_cond(c, w_ada, b_ada):
    B, D = c.shape
    N = w_ada.shape[1]
    tn = 2048 if N % 2048 == 0 else N
    cp = jnp.zeros((8, D), F32).at[:B].set(c)
    out = pl.pallas_call(
        _ada_kernel,
        grid=(N // tn,),
        in_specs=[pl.BlockSpec((8, D), lambda j: (0, 0)),
                  pl.BlockSpec((D, tn), lambda j: (0, j)),
                  pl.BlockSpec((1, tn), lambda j: (0, j))],
        out_specs=pl.BlockSpec((8, tn), lambda j: (0, j)),
        out_shape=jax.ShapeDtypeStruct((8, N), F32),
        compiler_params=_params(("arbitrary",)),
        name="ada_cond",
    )(cp, w_ada, b_ada.reshape(1, N))
    return out[:B]


def _inproj_kernel(x_ref, sh_ref, sc_ref, w_ref, xm_ref, z_ref):
    di = xm_ref.shape[-1]
    h = x_ref[...] * (1.0 + sc_ref[0]) + sh_ref[0]
    r = jnp.dot(h.astype(BF16), w_ref[...], preferred_element_type=F32)
    xm_ref[...] = r[:, :di].astype(BF16)
    z_ref[...] = r[:, di:].astype(BF16)


def _inproj(x2, shift, scale, w_in, S):
    T, D = x2.shape
    di = w_in.shape[1] // 2
    tm = min(512, S)
    tpb = S // tm
    return pl.pallas_call(
        _inproj_kernel,
        grid=(T // tm,),
        in_specs=[pl.BlockSpec((tm, D), lambda i: (i, 0)),
                  pl.BlockSpec((1, 1, D), lambda i: (i // tpb, 0, 0)),
                  pl.BlockSpec((1, 1, D), lambda i: (i // tpb, 0, 0)),
                  pl.BlockSpec((D, 2 * di), lambda i: (0, 0))],
        out_specs=[pl.BlockSpec((tm, di), lambda i: (i, 0)),
                   pl.BlockSpec((tm, di), lambda i: (i, 0))],
        out_shape=[jax.ShapeDtypeStruct((T, di), BF16)] * 2,
        compiler_params=_params(("parallel",)),
        name="mlstm_inproj",
    )(x2, shift, scale, w_in)


def _qkv_kernel(xm_ref, halo_ref, cw_ref, cb_ref, wq_ref, wk_ref, wv_ref, wif_ref, bif_ref,
                q_ref, k_ref, v_ref, xc_ref, g_ref, *, tpb, nh, kscale):
    tm, di = xm_ref.shape
    dh = di // nh
    i = pl.program_id(0)
    xm = xm_ref[...]
    xf = xm.astype(F32)
    halo = halo_ref[...].astype(F32)
    halo = jnp.where(i % tpb == 0, 0.0, halo)
    ext = jnp.concatenate([halo, xf], axis=0)
    acc = cb_ref[...] + cw_ref[CONV_WIDTH - 1:CONV_WIDTH, :] * xf
    for s in range(1, CONV_WIDTH):
        acc = acc + cw_ref[CONV_WIDTH - 1 - s:CONV_WIDTH - s, :] * ext[CONV_HALO - s:CONV_HALO - s + tm, :]
    xc = acc * _sigmoid(acc)
    xcb = xc.astype(BF16)
    xc_ref[...] = xcb
    for h in range(nh):
        sl = slice(h * dh, (h + 1) * dh)
        q_ref[:, sl] = jnp.dot(xcb[:, sl], wq_ref[h], preferred_element_type=F32).astype(BF16)
        k_ref[:, sl] = (jnp.dot(xcb[:, sl], wk_ref[h], preferred_element_type=F32) * kscale).astype(BF16)
        v_ref[:, sl] = jnp.dot(xm[:, sl], wv_ref[h], preferred_element_type=F32).astype(BF16)
    g = jnp.dot(xcb, wif_ref[...], preferred_element_type=F32) + bif_ref[...]
    col = lax.broadcasted_iota(jnp.int32, g.shape, 1)
    logsig = jnp.minimum(g, 0.0) - jnp.log(1.0 + jnp.exp(-jnp.abs(g)))
    g = jnp.where(col >= nh, logsig, g)
    g_ref[...] = g[:, :2 * nh]


def _qkv(xm, conv_w, conv_b, wq, wk, wv, w_if, b_if, S):
    T, di = xm.shape
    nh = wq.shape[0]
    dh = di // nh
    tm = min(512, S)
    tpb = S // tm
    hb = tm // CONV_HALO
    wif = jnp.zeros((di, LANES), BF16).at[:, :2 * nh].set(w_if.astype(BF16))
    bif = jnp.zeros((1, LANES), F32).at[0, :2 * nh].set(b_if)
    kern = functools.partial(_qkv_kernel, tpb=tpb, nh=nh, kscale=float(dh) ** -0.5)
    full2 = lambda i: (0, 0)
    full3 = lambda i: (0, 0, 0)
    row = lambda i: (i, 0)
    return pl.pallas_call(
        kern,
        grid=(T // tm,),
        in_specs=[pl.BlockSpec((tm, di), row),
                  pl.BlockSpec((CONV_HALO, di), lambda i: (jnp.maximum(i * hb - 1, 0), 0)),
                  pl.BlockSpec((CONV_WIDTH, di), full2),
                  pl.BlockSpec((1, di), full2),
                  pl.BlockSpec((nh, dh, dh), full3),
                  pl.BlockSpec((nh, dh, dh), full3),
                  pl.BlockSpec((nh, dh, dh), full3),
                  pl.BlockSpec((di, LANES), full2),
                  pl.BlockSpec((1, LANES), full2)],
        out_specs=[pl.BlockSpec((tm, di), row)] * 4 + [pl.BlockSpec((tm, 2 * nh), row)],
        out_shape=[jax.ShapeDtypeStruct((T, di), BF16)] * 4 + [jax.ShapeDtypeStruct((T, 2 * nh), F32)],
        compiler_params=_params(("parallel",)),
        name="mlstm_qkv",
    )(xm, xm, conv_w, conv_b.reshape(1, di), wq, wk, wv, wif, bif)


def _mlstm_kernel(q_ref, k_ref, v_ref, gc_ref, gr_ref, o_ref, c_scr, n_scr, m_scr):
    L = q_ref.shape[0]
    nh = c_scr.shape[0]
    dh = c_scr.shape[1]

    @pl.when(pl.program_id(1) == 0)
    def _():
        c_scr[...] = jnp.zeros_like(c_scr)
        n_scr[...] = jnp.zeros_like(n_scr)
        m_scr[...] = jnp.zeros_like(m_scr)

    t_idx = lax.broadcasted_iota(jnp.int32, (L, L), 0)
    s_idx = lax.broadcasted_iota(jnp.int32, (L, L), 1)
    causal = s_idx <= t_idx
    for h in range(nh):
        hs = slice(h * dh, (h + 1) * dh)
        _mlstm_head(q_ref[:, hs], k_ref[:, hs], v_ref[:, hs], gc_ref[0, h], gr_ref[0, h],
                    o_ref.at[:, hs], c_scr.at[h], n_scr.at[h], m_scr.at[h], causal, t_idx, s_idx)


def _mlstm_head(q, k, v, gc, gr, o_ref, c_scr, n_scr, m_scr, causal, t_idx, s_idx):
    i_col, f_col = gc[:, 0:1], gc[:, 1:2]
    i_row, f_row = gr[0:1, :], gr[1:2, :]
    b_col = jnp.sum(jnp.where(causal, f_row, 0.0), axis=1, keepdims=True)
    b_row = jnp.sum(jnp.where(t_idx <= s_idx, f_col, 0.0), axis=0, keepdims=True)
    m_prev = m_scr[...]
    d = jnp.where(causal, b_col - b_row + i_row, -jnp.inf)
    a_col = b_col + m_prev
    m_t = jnp.maximum(a_col, jnp.max(d, axis=1, keepdims=True))
    w_intra = jnp.exp(d - m_t)
    w_inter = jnp.exp(a_col - m_t)
    qk = lax.dot_general(q, k, (((1,), (1,)), ((), ())), preferred_element_type=F32)
    s_mat = qk * w_intra
    c_b = c_scr[...].astype(BF16)
    inter = lax.dot_general(q, c_b, (((1,), (1,)), ((), ())), preferred_element_type=F32)
    num = jnp.dot(s_mat.astype(BF16), v, preferred_element_type=F32) + w_inter * inter
    qn = jnp.sum(q.astype(F32) * n_scr[...], axis=1, keepdims=True)
    den = jnp.sum(s_mat, axis=1, keepdims=True) + w_inter * qn
    hcap = num / jnp.maximum(jnp.abs(den), jnp.exp(-m_t))
    mu = jnp.mean(hcap, axis=1, keepdims=True)
    dv = hcap - mu
    var = jnp.mean(dv * dv, axis=1, keepdims=True)
    o_ref[...] = (dv * lax.rsqrt(var + LN_EPS)).astype(BF16)

    b_last = jnp.sum(f_row, axis=1, keepdims=True)
    ws_col = b_last - b_col + i_col
    m_new = jnp.maximum(b_last + m_prev, jnp.max(ws_col, axis=0, keepdims=True))
    decay = jnp.exp(b_last + m_prev - m_new)
    ws = jnp.exp(ws_col - m_new)
    vw = (v.astype(F32) * ws).astype(BF16)
    upd = lax.dot_general(vw, k, (((0,), (0,)), ((), ())), preferred_element_type=F32)
    c_scr[...] = decay * c_scr[...] + upd
    n_scr[...] = decay * n_scr[...] + jnp.sum(k.astype(F32) * ws, axis=0, keepdims=True)
    m_scr[...] = m_new


def _mlstm(q, k, v, gcol, grow, B, S):
    T, di = q.shape
    nh = gcol.shape[1]
    dh = di // nh
    L = MLSTM_CHUNK
    nc = S // L
    blk = pl.BlockSpec((L, di), lambda b, c: (b * nc + c, 0))
    return pl.pallas_call(
        _mlstm_kernel,
        grid=(B, nc),
        in_specs=[blk, blk, blk,
                  pl.BlockSpec((1, nh, L, 2), lambda b, c: (b, 0, c, 0)),
                  pl.BlockSpec((1, nh, 2, L), lambda b, c: (b, 0, 0, c))],
        out_specs=blk,
        out_shape=jax.ShapeDtypeStruct((T, di), BF16),
        scratch_shapes=[pltpu.VMEM((nh, dh, dh), F32), pltpu.VMEM((nh, 1, dh), F32), pltpu.VMEM((nh, 1, 1), F32)],
        compiler_params=_params(("parallel", "arbitrary")),
        name="mlstm_scan",
    )(q, k, v, gcol, grow)


def _mlstm_out_kernel(hn_ref, xc_ref, z_ref, x_ref, gn_ref, skip_ref, w_ref, gate_ref, lg_ref, lb_ref, o_ref):
    z = z_ref[...].astype(F32)
    u = (hn_ref[...].astype(F32) * gn_ref[...] + skip_ref[...] * xc_ref[...].astype(F32)) * (z * _sigmoid(z))
    y = jnp.dot(u.astype(BF16), w_ref[...], preferred_element_type=F32)
    r = DEEPNORM_ALPHA * x_ref[...] + gate_ref[0] * y
    o_ref[...] = _layer_norm(r, lg_ref[...], lb_ref[...])


def _mlstm_out(hn, xc, z, x2, gn_g, skip, w_out, gate, ln_g, ln_b, S):
    T, D = x2.shape
    di = hn.shape[1]
    tm = min(512, S)
    tpb = S // tm
    row = lambda i: (i, 0)
    full2 = lambda i: (0, 0)
    return pl.pallas_call(
        _mlstm_out_kernel,
        grid=(T // tm,),
        in_specs=[pl.BlockSpec((tm, di), row), pl.BlockSpec((tm, di), row), pl.BlockSpec((tm, di), row),
                  pl.BlockSpec((tm, D), row),
                  pl.BlockSpec((1, di), full2), pl.BlockSpec((1, di), full2),
                  pl.BlockSpec((di, D), full2),
                  pl.BlockSpec((1, 1, D), lambda i: (i // tpb, 0, 0)),
                  pl.BlockSpec((1, D), full2), pl.BlockSpec((1, D), full2)],
        out_specs=pl.BlockSpec((tm, D), row),
        out_shape=jax.ShapeDtypeStruct((T, D), F32),
        compiler_params=_params(("parallel",)),
        name="mlstm_out_ln",
    )(hn, xc, z, x2, gn_g.reshape(1, di), skip.reshape(1, di), w_out, gate, ln_g.reshape(1, D), ln_b.reshape(1, D))


def _route_kernel(x_ref, sh_ref, sc_ref, wrt_ref, br_ref, tri_ref, out_ref, cnt_ref, carry):
    @pl.when(pl.program_id(0) == 0)
    def _():
        carry[...] = jnp.zeros_like(carry)

    E = wrt_ref.shape[0]
    epg = E // N_GROUPS
    h = x_ref[...] * (1.0 + sc_ref[0]) + sh_ref[0]
    lt = lax.dot_general(wrt_ref[...], h, (((1,), (1,)), ((), ())),
                         precision=lax.Precision.HIGHEST, preferred_element_type=F32)
    ex = jnp.exp(lt - jnp.max(lt, axis=0, keepdims=True))
    probs = ex / jnp.sum(ex, axis=0, keepdims=True)
    sel = probs + br_ref[...]
    srow = [sel[e:e + 1, :] for e in range(E)]
    prow = [probs[e:e + 1, :] for e in range(E)]
    gscore = []
    for g in range(N_GROUPS):
        r = srow[g * epg:(g + 1) * epg]
        best = None
        for a in range(epg):
            for b in range(a + 1, epg):
                pair = r[a] + r[b]
                best = pair if best is None else jnp.maximum(best, pair)
        gscore.append(best)
    gmax = functools.reduce(jnp.maximum, gscore)
    chosen, taken = [], None
    for g in range(N_GROUPS):
        c = gscore[g] == gmax
        if taken is not None:
            c = c & jnp.logical_not(taken)
        taken = c if taken is None else (taken | c)
        chosen.append(c)
    zero = jnp.zeros_like(srow[0])
    e1 = zero
    e2 = zero
    p1 = zero
    p2 = zero
    firsts, seconds = [], []
    for e in range(E):
        g = e // epg
        rank = zero
        for o in range(g * epg, (g + 1) * epg):
            if o == e:
                continue
            beats = (srow[o] > srow[e]) | ((srow[o] == srow[e]) & (o < e))
            rank = rank + jnp.where(beats, 1.0, 0.0)
        is1 = chosen[g] & (rank == 0.0)
        is2 = chosen[g] & (rank == 1.0)
        firsts.append(is1)
        seconds.append(is2)
        e1 = e1 + jnp.where(is1, float(e), 0.0)
        e2 = e2 + jnp.where(is2, float(e), 0.0)
        p1 = p1 + jnp.where(is1, prow[e], 0.0)
        p2 = p2 + jnp.where(is2, prow[e], 0.0)
    mask = jnp.concatenate([jnp.where(firsts[e] | seconds[e], 1.0, 0.0) for e in range(E)], axis=0)
    prefix = jnp.dot(mask.astype(BF16), tri_ref[...], preferred_element_type=F32) + carry[...]
    r1 = zero
    r2 = zero
    for e in range(E):
        pe = prefix[e:e + 1, :]
        r1 = r1 + jnp.where(firsts[e], pe, 0.0)
        r2 = r2 + jnp.where(seconds[e], pe, 0.0)
    psum = p1 + p2
    out_ref[...] = jnp.concatenate([e1, e2, p1 / psum, p2 / psum, r1, r2, zero, zero], axis=0)
    carry[...] = carry[...] + jnp.sum(mask, axis=1, keepdims=True)
    cnt_ref[...] = jnp.broadcast_to(carry[...], cnt_ref.shape)


def _route(x2, shift, scale, w_router, b_router, S):
    T, D = x2.shape
    E = w_router.shape[1]
    tm = min(512, S)
    tpb = S // tm
    tri = (jnp.arange(tm)[:, None] < jnp.arange(tm)[None, :]).astype(BF16)
    return pl.pallas_call(
        _route_kernel,
        grid=(T // tm,),
        in_specs=[pl.BlockSpec((tm, D), lambda i: (i, 0)),
                  pl.BlockSpec((1, 1, D), lambda i: (i // tpb, 0, 0)),
                  pl.BlockSpec((1, 1, D), lambda i: (i // tpb, 0, 0)),
                  pl.BlockSpec((E, D), lambda i: (0, 0)),
                  pl.BlockSpec((E, 1), lambda i: (0, 0)),
                  pl.BlockSpec((tm, tm), lambda i: (0, 0))],
        out_specs=[pl.BlockSpec((8, tm), lambda i: (0, i)),
                   pl.BlockSpec((E, LANES), lambda i: (0, 0))],
        out_shape=[jax.ShapeDtypeStruct((8, T), F32), jax.ShapeDtypeStruct((E, LANES), F32)],
        scratch_shapes=[pltpu.VMEM((E, 1), F32)],
        compiler_params=_params(("arbitrary",)),
        name="moe_route",
    )(x2, shift, scale, w_router.T, b_router.reshape(E, 1), tri)


def _dispatch_kernel(d1_ref, d2_ref, pe_ref, x_ref, sh_ref, sc_ref, xs_ref, hbuf, zbuf, sem, zsem):
    tr = x_ref.shape[0]
    rb = zbuf.shape[0]
    i = pl.program_id(0)
    slot = i % 2

    n_exp = pe_ref.shape[0] - 1
    n_blocks = xs_ref.shape[0] // rb

    def pad_copy(e):
        return pltpu.make_async_copy(zbuf, xs_ref.at[pl.ds(pl.multiple_of(pe_ref[e + 1] - rb, rb), rb)], zsem)

    def tail_copy(j):
        return pltpu.make_async_copy(zbuf, xs_ref.at[pl.ds(pl.multiple_of(pe_ref[n_exp] + j * rb, rb), rb)], zsem)

    def tail_live(j):
        return pe_ref[n_exp] // rb + j < n_blocks

    @pl.when(i == 0)
    def _():
        zbuf[...] = jnp.zeros_like(zbuf)
        for e in range(n_exp):
            @pl.when(pe_ref[e + 1] > pe_ref[e])
            def _():
                pad_copy(e).start()
        for j in range(n_exp):
            @pl.when(tail_live(j))
            def _():
                tail_copy(j).start()
        for e in range(n_exp):
            @pl.when(pe_ref[e + 1] > pe_ref[e])
            def _():
                pad_copy(e).wait()
        for j in range(n_exp):
            @pl.when(tail_live(j))
            def _():
                tail_copy(j).wait()

    hbuf[slot] = x_ref[...] * (1.0 + sc_ref[0]) + sh_ref[0]

    def row_copies(step, s, r):
        src = hbuf.at[s, pl.ds(r, 1)]
        t = step * tr + r
        return (pltpu.make_async_copy(src, xs_ref.at[pl.ds(d1_ref[t], 1)], sem.at[s]),
                pltpu.make_async_copy(src, xs_ref.at[pl.ds(d2_ref[t], 1)], sem.at[s]))

    def start(r, c):
        a, b = row_copies(i, slot, r)
        a.start(priority=0)
        b.start(priority=1)
        return c

    lax.fori_loop(0, tr, start, 0, unroll=8)

    def wait_step(step, s):
        def wait(r, c):
            a, b = row_copies(step, s, r)
            a.wait()
            b.wait()
            return c
        lax.fori_loop(0, tr, wait, 0)

    @pl.when(i > 0)
    def _():
        wait_step(i - 1, 1 - slot)

    @pl.when(i == pl.num_programs(0) - 1)
    def _():
        wait_step(i, slot)


def _dispatch(x2, shift, scale, d1, d2, pends0, cap, S):
    T, D = x2.shape
    tr = min(256, S)
    tpb = S // tr
    grid_spec = pltpu.PrefetchScalarGridSpec(
        num_scalar_prefetch=3,
        grid=(T // tr,),
        in_specs=[pl.BlockSpec((tr, D), lambda i, d1, d2, pe: (i, 0)),
                  pl.BlockSpec((1, 1, D), lambda i, d1, d2, pe: (i // tpb, 0, 0)),
                  pl.BlockSpec((1, 1, D), lambda i, d1, d2, pe: (i // tpb, 0, 0))],
        out_specs=pl.BlockSpec(memory_space=pl.ANY),
        scratch_shapes=[pltpu.VMEM((2, tr, D), F32), pltpu.VMEM((MOE_ROW_BLOCK, D), F32),
                        pltpu.SemaphoreType.DMA((2,)), pltpu.SemaphoreType.DMA(())],
    )
    return pl.pallas_call(
        _dispatch_kernel,
        grid_spec=grid_spec,
        out_shape=jax.ShapeDtypeStruct((cap, D), F32),
        compiler_params=_params(("arbitrary",)),
        name="moe_dispatch",
    )(d1, d2, pends0, x2, shift, scale)


def _expert_kernel(be_ref, na_ref, xs_ref, wg_ref, wu_ref, wd_ref, ys_ref):
    del be_ref

    @pl.when(pl.program_id(0) < na_ref[0])
    def _():
        xb = xs_ref[...].astype(BF16)
        g = jnp.dot(xb, wg_ref[0], preferred_element_type=F32)
        u = jnp.dot(xb, wu_ref[0], preferred_element_type=F32)
        a = (g * _sigmoid(g)) * u
        ys_ref[...] = jnp.dot(a.astype(BF16), wd_ref[0], preferred_element_type=F32)

    @pl.when(pl.program_id(0) >= na_ref[0])
    def _():
        ys_ref[...] = jnp.zeros_like(ys_ref)


def _experts(xs, block_expert, n_active, w_gate, w_up, w_down):
    cap, D = xs.shape
    E, _, F = w_gate.shape
    rb = MOE_ROW_BLOCK
    nb = cap // rb

    def blk(i, be, na):
        return jnp.maximum(jnp.minimum(i, na[0] - 1), 0)

    grid_spec = pltpu.PrefetchScalarGridSpec(
        num_scalar_prefetch=2,
        grid=(nb,),
        in_specs=[pl.BlockSpec((rb, D), lambda i, be, na: (blk(i, be, na), 0)),
                  pl.BlockSpec((1, D, F), lambda i, be, na: (be[blk(i, be, na)], 0, 0)),
                  pl.BlockSpec((1, D, F), lambda i, be, na: (be[blk(i, be, na)], 0, 0)),
                  pl.BlockSpec((1, F, D), lambda i, be, na: (be[blk(i, be, na)], 0, 0))],
        out_specs=pl.BlockSpec((rb, D), lambda i, be, na: (i, 0)),
    )
    return pl.pallas_call(
        _expert_kernel,
        grid_spec=grid_spec,
        out_shape=jax.ShapeDtypeStruct((cap, D), F32),
        compiler_params=_params(("arbitrary",)),
        name="moe_experts",
    )(block_expert, n_active, xs, w_gate, w_up, w_down)


def _combine_kernel(d1_ref, d2_ref, ys_ref, rt_ref, x_ref, gate_ref, lg_ref, lb_ref, o_ref, buf, sem):
    tr = x_ref.shape[0]
    i = pl.program_id(0)
    slot = i % 2

    def row_copies(step, s, r):
        t = step * tr + r
        return (pltpu.make_async_copy(ys_ref.at[pl.ds(d1_ref[t], 1)], buf.at[s, 0, pl.ds(r, 1)], sem.at[s]),
                pltpu.make_async_copy(ys_ref.at[pl.ds(d2_ref[t], 1)], buf.at[s, 1, pl.ds(r, 1)], sem.at[s]))

    def start_step(step, s):
        def start(r, c):
            a, b = row_copies(step, s, r)
            a.start(priority=0)
            b.start(priority=1)
            return c
        lax.fori_loop(0, tr, start, 0, unroll=8)

    @pl.when(i == 0)
    def _():
        start_step(i, slot)

    @pl.when(i + 1 < pl.num_programs(0))
    def _():
        start_step(i + 1, 1 - slot)

    def wait(r, c):
        a, b = row_copies(i, slot, r)
        a.wait()
        b.wait()
        return c

    lax.fori_loop(0, tr, wait, 0)
    rt = rt_ref[...]
    y = rt[:, 2:3] * buf[slot, 0] + rt[:, 3:4] * buf[slot, 1]
    r = DEEPNORM_ALPHA * x_ref[...] + gate_ref[0] * y
    o_ref[...] = _layer_norm(r, lg_ref[...], lb_ref[...])


def _combine(ys, route_t, x2, gate, ln_g, ln_b, d1, d2, S):
    T, D = x2.shape
    tr = min(256, S)
    tpb = S // tr
    grid_spec = pltpu.PrefetchScalarGridSpec(
        num_scalar_prefetch=2,
        grid=(T // tr,),
        in_specs=[pl.BlockSpec(memory_space=pl.ANY),
                  pl.BlockSpec((tr, 8), lambda i, d1, d2: (i, 0)),
                  pl.BlockSpec((tr, D), lambda i, d1, d2: (i, 0)),
                  pl.BlockSpec((1, 1, D), lambda i, d1, d2: (i // tpb, 0, 0)),
                  pl.BlockSpec((1, D), lambda i, d1, d2: (0, 0)),
                  pl.BlockSpec((1, D), lambda i, d1, d2: (0, 0))],
        out_specs=pl.BlockSpec((tr, D), lambda i, d1, d2: (i, 0)),
        scratch_shapes=[pltpu.VMEM((2, 2, tr, D), F32), pltpu.SemaphoreType.DMA((2,))],
    )
    return pl.pallas_call(
        _combine_kernel,
        grid_spec=grid_spec,
        out_shape=jax.ShapeDtypeStruct((T, D), F32),
        compiler_params=_params(("arbitrary",)),
        name="moe_combine_ln",
    )(d1, d2, ys, route_t, x2, gate, ln_g.reshape(1, D), ln_b.reshape(1, D))


def _moe_layer(x2, shift, scale, gate, ln_g, ln_b, w_router, b_router, w_gate, w_up, w_down, S):
    T, D = x2.shape
    E = w_router.shape[1]
    rb = MOE_ROW_BLOCK
    route, cnt = _route(x2, shift, scale, w_router, b_router, S)
    counts = cnt[:, 0].astype(jnp.int32)
    padded = (counts + rb - 1) // rb * rb
    pends = jnp.cumsum(padded)
    pstarts = pends - padded
    e1 = route[0].astype(jnp.int32)
    e2 = route[1].astype(jnp.int32)
    d1 = pstarts[e1] + route[4].astype(jnp.int32)
    d2 = pstarts[e2] + route[5].astype(jnp.int32)
    nb = -(-(T * MOE_TOPK) // rb) + E
    cap = nb * rb
    block_start = jnp.arange(nb, dtype=jnp.int32) * rb
    block_expert = jnp.minimum(jnp.sum(block_start[:, None] >= pends[None, :], axis=1), E - 1).astype(jnp.int32)
    n_active = (pends[-1:] // rb).astype(jnp.int32)
    pends0 = jnp.concatenate([jnp.zeros((1,), jnp.int32), pends.astype(jnp.int32)])
    xs = _dispatch(x2, shift, scale, d1, d2, pends0, cap, S)
    ys = _experts(xs, block_expert, n_active, w_gate, w_up, w_down)
    return _combine(ys, route.T, x2, gate, ln_g, ln_b, d1, d2, S)


def _moba_proj_kernel(x_ref, shq_ref, scq_ref, shkv_ref, sckv_ref, wq_ref, wkv_ref,
                      q_ref, k_ref, vt_ref, km_ref, *, qscale, tpb, slopes):
    tm, D = x_ref.shape
    hd = HEAD_DIM
    nh = D // hd
    x = x_ref[...]
    hq = x * (1.0 + scq_ref[0]) + shq_ref[0]
    hkv = x * (1.0 + sckv_ref[0]) + shkv_ref[0]
    qf = jnp.dot(hq.astype(BF16), wq_ref[...], preferred_element_type=F32) * qscale
    kv = jnp.dot(hkv.astype(BF16), wkv_ref[...], preferred_element_type=F32)
    kf = kv[:, :D]
    vt = kv[:, D:].T
    ones_pad = jnp.where(lax.broadcasted_iota(jnp.int32, (V_ROWS - hd, tm), 0) == 0, 1.0, 0.0)
    for g in range(nh):
        vt_ref[0, g * V_ROWS:(g + 1) * V_ROWS, :] = jnp.concatenate(
            [vt[g * hd:(g + 1) * hd, :], ones_pad], axis=0).astype(BF16)
    lane = lax.broadcasted_iota(jnp.int32, (1, LANES), 1)
    head_lanes = lane >= hd
    pos = (pl.program_id(0) % tpb) * tm + lax.broadcasted_iota(jnp.int32, (tm, 1), 0)
    pos_f = pos.astype(F32)
    onehot = jnp.where(lane == pos // MOBA_BLOCK, 1.0, 0.0)
    q_fill = jnp.where((lane >= BIAS_LANE) & (lane < BIAS_LANE + 3), 1.0, 0.0)
    nbt = tm // MOBA_BLOCK
    means = [jnp.mean(kf[j * MOBA_BLOCK:(j + 1) * MOBA_BLOCK, :], axis=0, keepdims=True) for j in range(nbt)]
    km = jnp.concatenate(means + [jnp.zeros((8 - nbt, D), F32)], axis=0)
    for j in range(nh // 2):
        cs = slice(j * LANES, (j + 1) * LANES)
        qb, kb, mb = qf[:, cs], kf[:, cs], km[:, cs]
        q_pair = (pltpu.roll(qb, hd, axis=1), qb)
        k_pair = (pltpu.roll(kb, hd, axis=1), kb)
        m_pair = (pltpu.roll(mb, hd, axis=1), mb)
        for e in range(2):
            g = 2 * j + e
            gs = slice(g * LANES, (g + 1) * LANES)
            q_ref[:, gs] = jnp.where(head_lanes, q_pair[e], q_fill).astype(BF16)
            c = (slopes[g] * LOG2E) * pos_f
            hi = c.astype(BF16).astype(F32)
            mid = (c - hi).astype(BF16).astype(F32)
            lo = c - hi - mid
            aug = jnp.where(lane == BIAS_LANE, hi,
                            jnp.where(lane == BIAS_LANE + 1, mid,
                                      jnp.where(lane == BIAS_LANE + 2, lo, onehot)))
            k_ref[:, gs] = jnp.where(head_lanes, k_pair[e], aug).astype(BF16)
            mg = jnp.where(head_lanes, m_pair[e], 0.0)
            for jb in range(nbt):
                km_ref[jb, :, gs] = mg[jb:jb + 1, :]


def _moba_proj(x2, shq, scq, shkv, sckv, wq, wkv, B, S):
    T, D = x2.shape
    nh = D // HEAD_DIM
    tm = min(512, S)
    tpb = S // tm
    nbt = tm // MOBA_BLOCK
    assert S // MOBA_BLOCK <= BIAS_LANE and nbt <= 8
    slopes = tuple(2.0 ** (-8.0 * (h + 1.0) / nh) for h in range(nh))
    mod = pl.BlockSpec((1, 1, D), lambda i: (i // tpb, 0, 0))
    kern = functools.partial(_moba_proj_kernel, qscale=float(HEAD_DIM) ** -0.5 * LOG2E, tpb=tpb, slopes=slopes)
    return pl.pallas_call(
        kern,
        grid=(T // tm,),
        in_specs=[pl.BlockSpec((tm, D), lambda i: (i, 0)), mod, mod, mod, mod,
                  pl.BlockSpec((D, D), lambda i: (0, 0)),
                  pl.BlockSpec((D, 2 * D), lambda i: (0, 0))],
        out_specs=[pl.BlockSpec((tm, nh * LANES), lambda i: (i, 0)),
                   pl.BlockSpec((tm, nh * LANES), lambda i: (i, 0)),
                   pl.BlockSpec((1, nh * V_ROWS, tm), lambda i: (i // tpb, 0, i % tpb)),
                   pl.BlockSpec((nbt, 1, nh * LANES), lambda i: (i, 0, 0))],
        out_shape=[jax.ShapeDtypeStruct((T, nh * LANES), BF16), jax.ShapeDtypeStruct((T, nh * LANES), BF16),
                   jax.ShapeDtypeStruct((B, nh * V_ROWS, S), BF16),
                   jax.ShapeDtypeStruct((T // MOBA_BLOCK, 1, nh * LANES), F32)],
        compiler_params=_params(("parallel",)),
        name="moba_proj",
    )(x2, shq, scq, shkv, sckv, wq, wkv)


def _moba_select_kernel(q_ref, km_ref, qat_ref):
    tq = q_ref.shape[0]
    G = q_ref.shape[1] // LANES
    nblk = km_ref.shape[1]
    blk_id = lax.broadcasted_iota(jnp.int32, (nblk, tq), 0)
    own = (pl.program_id(2) * tq + lax.broadcasted_iota(jnp.int32, (1, tq), 1)) // MOBA_BLOCK
    for g in range(G):
        gs = slice(g * LANES, (g + 1) * LANES)
        qgt = q_ref[:, gs].astype(F32).T
        gate = jnp.dot(km_ref[0, :, gs], qgt, precision=lax.Precision.HIGHEST,
                       preferred_element_type=F32)
        gsc = jnp.where(blk_id < own, gate, -jnp.inf)
        picked = blk_id < 0
        for _ in range(MOBA_TOPK):
            mx = jnp.max(gsc, axis=0, keepdims=True)
            first = jnp.min(jnp.where(gsc == mx, blk_id, nblk), axis=0, keepdims=True)
            pick = blk_id == first
            picked = picked | pick
            gsc = jnp.where(pick, -jnp.inf, gsc)
        visible = (picked & (blk_id < own)) | (blk_id == own)
        pen = jnp.where(visible, 0.0, NEG)
        qat_ref[0, g] = (qgt + jnp.concatenate([pen, jnp.zeros((LANES - nblk, tq), F32)], axis=0)).astype(BF16)


def _moba_select(q, km, B, S):
    nh = q.shape[1] // LANES
    G = MOBA_HEADS_PER_STEP
    tq = min(512, S)
    nblk = S // MOBA_BLOCK
    return pl.pallas_call(
        _moba_select_kernel,
        grid=(B, nh // G, S // tq),
        in_specs=[pl.BlockSpec((tq, G * LANES), lambda b, hp, i: (b * (S // tq) + i, hp)),
                  pl.BlockSpec((1, nblk, G * LANES), lambda b, hp, i: (b, 0, hp))],
        out_specs=pl.BlockSpec((1, G, LANES, tq), lambda b, hp, i: (b, hp, 0, i)),
        out_shape=jax.ShapeDtypeStruct((B, nh, LANES, S), BF16),
        compiler_params=_params(("parallel", "parallel", "parallel")),
        name="moba_select",
    )(q, km)


def _moba_kernel(qat_ref, k_ref, vt_ref, o_ref, st_scr, p_scr, m_scr, acc_scr):
    G = qat_ref.shape[1]
    BLK = qat_ref.shape[3]
    hd = HEAD_DIM
    nhalf = BLK // LANES
    own = pl.program_id(2)
    k_off = lax.broadcasted_iota(jnp.int32, (BLK, 1), 0)
    q_off = lax.broadcasted_iota(jnp.int32, (1, LANES), 1)

    m_scr[...] = jnp.full(m_scr.shape, NEG, F32)
    acc_scr[...] = jnp.zeros_like(acc_scr)

    def scores(n, slot):
        start = pl.multiple_of(n * BLK, BLK)
        for g in range(G):
            kn = k_ref[pl.ds(start, BLK), g * LANES:(g + 1) * LANES]
            st_scr[slot, g] = jnp.dot(kn, qat_ref[0, g], preferred_element_type=F32)

    def update(n, slot, causal):
        start = pl.multiple_of(n * BLK, BLK)
        for g in range(G):
            alphas = []
            for hf in range(nhalf):
                c = g * nhalf + hf
                ls = slice(hf * LANES, (hf + 1) * LANES)
                st = st_scr[slot, g, :, ls]
                if causal:
                    st = jnp.where(k_off <= q_off + hf * LANES, st, NEG)
                m = m_scr[c]
                m_new = jnp.maximum(m, jnp.max(st, axis=0, keepdims=True))
                alpha = jnp.exp2(m - m_new)
                p = jnp.exp2(st - m_new)
                p_scr[g, :, ls] = p.astype(BF16)
                alphas.append(alpha)
                m_scr[c] = m_new
            vtn = vt_ref[0, g * V_ROWS:(g + 1) * V_ROWS, pl.ds(start, BLK)]
            pv = jnp.dot(vtn, p_scr[g], preferred_element_type=F32)
            acc_scr[g] = jnp.concatenate(alphas, axis=1) * acc_scr[g] + pv

    scores(0, 0)

    def step(i, carry):
        n = 2 * i
        scores(n + 1, 1)
        update(n, 0, False)
        scores(n + 2, 0)
        update(n + 1, 1, False)
        return carry

    lax.fori_loop(0, own // 2, step, 0)

    @pl.when(own % 2 == 0)
    def _():
        update(own, 0, True)

    @pl.when(own % 2 == 1)
    def _():
        scores(own, 1)
        update(own - 1, 0, False)
        update(own, 1, True)
    for g in range(G):
        acc = acc_scr[g]
        o_ref[0, g * hd:(g + 1) * hd, :] = (acc[:hd, :] * (1.0 / acc[hd:hd + 1, :])).astype(BF16)


def _moba_attn(qat, k, vt, B, S):
    nh = qat.shape[1]
    G = MOBA_HEADS_PER_STEP
    BLK = MOBA_BLOCK
    nblk = S // BLK
    hd = HEAD_DIM
    return pl.pallas_call(
        _moba_kernel,
        grid=(B, nh // G, nblk),
        in_specs=[pl.BlockSpec((1, G, LANES, BLK), lambda b, hp, qi: (b, hp, 0, qi)),
                  pl.BlockSpec((S, G * LANES), lambda b, hp, qi: (b, hp)),
                  pl.BlockSpec((1, G * V_ROWS, S), lambda b, hp, qi: (b, hp, 0))],
        out_specs=pl.BlockSpec((1, G * hd, BLK), lambda b, hp, qi: (b, hp, qi)),
        out_shape=jax.ShapeDtypeStruct((B, nh * hd, S), BF16),
        scratch_shapes=[pltpu.VMEM((2, G, BLK, BLK), F32),
                        pltpu.VMEM((G, BLK, BLK), BF16),
                        pltpu.VMEM((G * (BLK // LANES), 1, LANES), F32),
                        pltpu.VMEM((G, V_ROWS, BLK), F32)],
        compiler_params=_params(("parallel", "parallel", "arbitrary")),
        name="moba_attn",
    )(qat, k, vt)


def _attn_out_kernel(a_ref, x_ref, w_ref, gate_ref, lg_ref, lb_ref, o_ref):
    y = lax.dot_general(a_ref[0], w_ref[...], (((0,), (0,)), ((), ())), preferred_element_type=F32)
    r = DEEPNORM_ALPHA * x_ref[...] + gate_ref[0] * y
    o_ref[...] = _layer_norm(r, lg_ref[...], lb_ref[...])


def _attn_out(at, x2, w, gate, ln_g, ln_b, S):
    T, D = x2.shape
    K = at.shape[1]
    tm = min(512, S)
    tpb = S // tm
    row = lambda i: (i, 0)
    full2 = lambda i: (0, 0)
    return pl.pallas_call(
        _attn_out_kernel,
        grid=(T // tm,),
        in_specs=[pl.BlockSpec((1, K, tm), lambda i: (i // tpb, 0, i % tpb)), pl.BlockSpec((tm, D), row),
                  pl.BlockSpec((K, D), full2),
                  pl.BlockSpec((1, 1, D), lambda i: (i // tpb, 0, 0)),
                  pl.BlockSpec((1, D), full2), pl.BlockSpec((1, D), full2)],
        out_specs=pl.BlockSpec((tm, D), row),
        out_shape=jax.ShapeDtypeStruct((T, D), F32),
        compiler_params=_params(("parallel",)),
        name="attn_out_ln",
    )(at, x2, w, gate, ln_g.reshape(1, D), ln_b.reshape(1, D))


def kernel(x, c, w_ada, b_ada, ln_g, ln_b, a_w_in, a_conv_w, a_conv_b, a_wq, a_wk, a_wv, a_w_if, a_b_if,
           a_gn_g, a_skip, a_w_out, b_w_kv, b_wq, b_wo, moe_w_router, moe_b_router, moe_w_gate, moe_w_up,
           moe_w_down):
    B, S, D = x.shape
    T = B * S
    assert DEPTH == 2 and S % MOBA_BLOCK == 0 and S % MLSTM_CHUNK == 0
    n_layer_mod = DEPTH * N_MOD_PER_LAYER * D
    cond = _ada_cond(c, w_ada, b_ada)
    mods = cond[:, :n_layer_mod].reshape(B, DEPTH, 2, 3, 1, D)
    kv_mod = cond[:, n_layer_mod:].reshape(B, 2, 1, D)

    def mod3(layer, sub):
        m = mods[:, layer, sub]
        return m[:, 0], m[:, 1], 1.0 + m[:, 2]

    xf = x.reshape(T, D)

    shift, scale, gate = mod3(0, 0)
    nh = MLSTM_HEADS
    xm, z = _inproj(xf, shift, scale, a_w_in[0].astype(BF16), S)
    q, k, v, xc, gts = _qkv(xm, a_conv_w[0], a_conv_b[0], a_wq[0].astype(BF16), a_wk[0].astype(BF16),
                            a_wv[0].astype(BF16), a_w_if[0], a_b_if[0], S)
    g4 = gts.reshape(B, S, 2, nh)
    gcol = jnp.transpose(g4, (0, 3, 1, 2))
    grow = jnp.transpose(g4, (0, 3, 2, 1))
    hn = _mlstm(q, k, v, gcol, grow, B, S)
    xf = _mlstm_out(hn, xc, z, xf, a_gn_g[0], a_skip[0], a_w_out[0].astype(BF16), gate, ln_g[0, 0], ln_b[0, 0], S)
    shift, scale, gate = mod3(0, 1)
    xf = _moe_layer(xf, shift, scale, gate, ln_g[0, 1], ln_b[0, 1], moe_w_router, moe_b_router,
                    moe_w_gate[0].astype(BF16), moe_w_up[0].astype(BF16), moe_w_down[0].astype(BF16), S)

    shift, scale, gate = mod3(1, 0)
    q, k, vt, km = _moba_proj(xf, shift, scale, kv_mod[:, 0], kv_mod[:, 1], b_wq[0].astype(BF16),
                              b_w_kv.astype(BF16), B, S)
    km = km.reshape(B, S // MOBA_BLOCK, km.shape[-1])
    attn_t = _moba_attn(_moba_select(q, km, B, S), k, vt, B, S)
    xf = _attn_out(attn_t, xf, b_wo[0].astype(BF16), gate, ln_g[1, 0], ln_b[1, 0], S)
    shift, scale, gate = mod3(1, 1)
    xf = _moe_layer(xf, shift, scale, gate, ln_g[1, 1], ln_b[1, 1], moe_w_router, moe_b_router,
                    moe_w_gate[1].astype(BF16), moe_w_up[1].astype(BF16), moe_w_down[1].astype(BF16), S)
    return xf.reshape(B, S, D)
```

```python
import functools

import jax
import jax.numpy as jnp
from jax import lax
from jax.experimental import pallas as pl
from jax.experimental.pallas import tpu as pltpu

DEPTH = 2
MLSTM_HEADS = 4
CONV_WIDTH = 4
MLSTM_CHUNK = 128
ATTN_HEADS = 16
MOBA_BLOCK = 256
MOBA_TOPK = 3
N_EXPERTS = 16
N_GROUPS = 4
MOE_TOPK = 2
MOE_ROW_BLOCK = 256
MOE_TILE = 512
CHUNK_BITS = 10
RUN_ALIGN = 8
DEEPNORM_ALPHA = (2.0 * DEPTH) ** 0.25
LN_EPS = 1e-5
N_MOD_PER_LAYER = 6

HEAD_DIM = 64
V_ROWS = 80
MOBA_HEADS_PER_STEP = 4
BIAS_LANE = 32
LOG2E = 1.4426950408889634

LANES = 128
CONV_HALO = 16
NEG = -1e30
VMEM_LIMIT = 56 * 1024 * 1024

F32 = jnp.float32
BF16 = jnp.bfloat16


def _sigmoid(x):
    return 1.0 / (1.0 + jnp.exp(-x))


def _params(sem, vmem=VMEM_LIMIT):
    return pltpu.CompilerParams(dimension_semantics=sem, vmem_limit_bytes=vmem)


def _layer_norm(r, g, b):
    mu = jnp.mean(r, axis=-1, keepdims=True)
    d = r - mu
    var = jnp.mean(d * d, axis=-1, keepdims=True)
    return d * lax.rsqrt(var + LN_EPS) * g + b


def _ada_kernel(c_ref, w_ref, b_ref, o_ref):
    c = c_ref[...]
    s = c * _sigmoid(c)
    o_ref[...] = jnp.dot(s.astype(BF16), w_ref[...].astype(BF16),
                         preferred_element_type=F32) + b_ref[...]


def _ada_cond(c, w_ada, b_ada):
    B, D = c.shape
    N = w_ada.shape[1]
    tn = 2048 if N % 2048 == 0 else N
    cp = jnp.zeros((8, D), F32).at[:B].set(c)
    out = pl.pallas_call(
        _ada_kernel,
        grid=(N // tn,),
        in_specs=[pl.BlockSpec((8, D), lambda j: (0, 0)),
                  pl.BlockSpec((D, tn), lambda j: (0, j)),
                  pl.BlockSpec((1, tn), lambda j: (0, j))],
        out_specs=pl.BlockSpec((8, tn), lambda j: (0, j)),
        out_shape=jax.ShapeDtypeStruct((8, N), F32),
        compiler_params=_params(("arbitrary",)),
        name="ada_cond",
    )(cp, w_ada, b_ada.reshape(1, N))
    return out[:B]


def _inproj_kernel(x_ref, sh_ref, sc_ref, w_ref, xm_ref, z_ref):
    di = xm_ref.shape[-1]
    h = x_ref[...] * (1.0 + sc_ref[0]) + sh_ref[0]
    r = jnp.dot(h.astype(BF16), w_ref[...], preferred_element_type=F32)
    xm_ref[...] = r[:, :di].astype(BF16)
    z_ref[...] = r[:, di:].astype(BF16)


def _inproj(x2, shift, scale, w_in, S):
    T, D = x2.shape
    di = w_in.shape[1] // 2
    tm = min(512, S)
    tpb = S // tm
    return pl.pallas_call(
        _inproj_kernel,
        grid=(T // tm,),
        in_specs=[pl.BlockSpec((tm, D), lambda i: (i, 0)),
                  pl.BlockSpec((1, 1, D), lambda i: (i // tpb, 0, 0)),
                  pl.BlockSpec((1, 1, D), lambda i: (i // tpb, 0, 0)),
                  pl.BlockSpec((D, 2 * di), lambda i: (0, 0))],
        out_specs=[pl.BlockSpec((tm, di), lambda i: (i, 0)),
                   pl.BlockSpec((tm, di), lambda i: (i, 0))],
        out_shape=[jax.ShapeDtypeStruct((T, di), BF16)] * 2,
        compiler_params=_params(("parallel",)),
        name="mlstm_inproj",
    )(x2, shift, scale, w_in)


def _qkv_kernel(xm_ref, halo_ref, cw_ref, cb_ref, wq_ref, wk_ref, wv_ref, wif_ref, bif_ref,
                q_ref, k_ref, v_ref, xc_ref, g_ref, *, tpb, nh, kscale):
    tm, di = xm_ref.shape
    dh = di // nh
    i = pl.program_id(0)
    xm = xm_ref[...]
    xf = xm.astype(F32)
    halo = halo_ref[...].astype(F32)
    halo = jnp.where(i % tpb == 0, 0.0, halo)
    ext = jnp.concatenate([halo, xf], axis=0)
    acc = cb_ref[...] + cw_ref[CONV_WIDTH - 1:CONV_WIDTH, :] * xf
    for s in range(1, CONV_WIDTH):
        acc = acc + cw_ref[CONV_WIDTH - 1 - s:CONV_WIDTH - s, :] * ext[CONV_HALO - s:CONV_HALO - s + tm, :]
    xc = acc * _sigmoid(acc)
    xcb = xc.astype(BF16)
    xc_ref[...] = xcb
    for h in range(nh):
        sl = slice(h * dh, (h + 1) * dh)
        q_ref[:, sl] = jnp.dot(xcb[:, sl], wq_ref[h], preferred_element_type=F32).astype(BF16)
        k_ref[:, sl] = (jnp.dot(xcb[:, sl], wk_ref[h], preferred_element_type=F32) * kscale).astype(BF16)
        v_ref[:, sl] = jnp.dot(xm[:, sl], wv_ref[h], preferred_element_type=F32).astype(BF16)
    g = jnp.dot(xcb, wif_ref[...], preferred_element_type=F32) + bif_ref[...]
    col = lax.broadcasted_iota(jnp.int32, g.shape, 1)
    logsig = jnp.minimum(g, 0.0) - jnp.log(1.0 + jnp.exp(-jnp.abs(g)))
    g = jnp.where(col >= nh, logsig, g)
    g_ref[...] = g[:, :2 * nh]


def _qkv(xm, conv_w, conv_b, wq, wk, wv, w_if, b_if, S):
    T, di = xm.shape
    nh = wq.shape[0]
    dh = di // nh
    tm = min(512, S)
    tpb = S // tm
    hb = tm // CONV_HALO
    wif = jnp.zeros((di, LANES), BF16).at[:, :2 * nh].set(w_if.astype(BF16))
    bif = jnp.zeros((1, LANES), F32).at[0, :2 * nh].set(b_if)
    kern = functools.partial(_qkv_kernel, tpb=tpb, nh=nh, kscale=float(dh) ** -0.5)
    full2 = lambda i: (0, 0)
    full3 = lambda i: (0, 0, 0)
    row = lambda i: (i, 0)
    return pl.pallas_call(
        kern,
        grid=(T // tm,),
        in_specs=[pl.BlockSpec((tm, di), row),
                  pl.BlockSpec((CONV_HALO, di), lambda i: (jnp.maximum(i * hb - 1, 0), 0)),
                  pl.BlockSpec((CONV_WIDTH, di), full2),
                  pl.BlockSpec((1, di), full2),
                  pl.BlockSpec((nh, dh, dh), full3),
                  pl.BlockSpec((nh, dh, dh), full3),
                  pl.BlockSpec((nh, dh, dh), full3),
                  pl.BlockSpec((di, LANES), full2),
                  pl.BlockSpec((1, LANES), full2)],
        out_specs=[pl.BlockSpec((tm, di), row)] * 4 + [pl.BlockSpec((tm, 2 * nh), row)],
        out_shape=[jax.ShapeDtypeStruct((T, di), BF16)] * 4 + [jax.ShapeDtypeStruct((T, 2 * nh), F32)],
        compiler_params=_params(("parallel",)),
        name="mlstm_qkv",
    )(xm, xm, conv_w, conv_b.reshape(1, di), wq, wk, wv, wif, bif)


def _mlstm_kernel(q_ref, k_ref, v_ref, gc_ref, gr_ref, o_ref, c_scr, n_scr, m_scr):
    L = q_ref.shape[0]
    nh = c_scr.shape[0]
    dh = c_scr.shape[1]

    @pl.when(pl.program_id(1) == 0)
    def _():
        c_scr[...] = jnp.zeros_like(c_scr)
        n_scr[...] = jnp.zeros_like(n_scr)
        m_scr[...] = jnp.zeros_like(m_scr)

    t_idx = lax.broadcasted_iota(jnp.int32, (L, L), 0)
    s_idx = lax.broadcasted_iota(jnp.int32, (L, L), 1)
    causal = s_idx <= t_idx
    for h in range(nh):
        hs = slice(h * dh, (h + 1) * dh)
        _mlstm_head(q_ref[:, hs], k_ref[:, hs], v_ref[:, hs], gc_ref[0, h], gr_ref[0, h],
                    o_ref.at[:, hs], c_scr.at[h], n_scr.at[h], m_scr.at[h], causal, t_idx, s_idx)


def _mlstm_head(q, k, v, gc, gr, o_ref, c_scr, n_scr, m_scr, causal, t_idx, s_idx):
    i_col, f_col = gc[:, 0:1], gc[:, 1:2]
    i_row, f_row = gr[0:1, :], gr[1:2, :]
    b_col = jnp.sum(jnp.where(causal, f_row, 0.0), axis=1, keepdims=True)
    b_row = jnp.sum(jnp.where(t_idx <= s_idx, f_col, 0.0), axis=0, keepdims=True)
    m_prev = m_scr[...]
    d = jnp.where(causal, b_col - b_row + i_row, -jnp.inf)
    a_col = b_col + m_prev
    m_t = jnp.maximum(a_col, jnp.max(d, axis=1, keepdims=True))
    w_intra = jnp.exp(d - m_t)
    w_inter = jnp.exp(a_col - m_t)
    qk = lax.dot_general(q, k, (((1,), (1,)), ((), ())), preferred_element_type=F32)
    s_mat = qk * w_intra
    c_b = c_scr[...].astype(BF16)
    inter = lax.dot_general(q, c_b, (((1,), (1,)), ((), ())), preferred_element_type=F32)
    num = jnp.dot(s_mat.astype(BF16), v, preferred_element_type=F32) + w_inter * inter
    qn = jnp.sum(q.astype(F32) * n_scr[...], axis=1, keepdims=True)
    den = jnp.sum(s_mat, axis=1, keepdims=True) + w_inter * qn
    hcap = num / jnp.maximum(jnp.abs(den), jnp.exp(-m_t))
    mu = jnp.mean(hcap, axis=1, keepdims=True)
    dv = hcap - mu
    var = jnp.mean(dv * dv, axis=1, keepdims=True)
    o_ref[...] = (dv * lax.rsqrt(var + LN_EPS)).astype(BF16)

    b_last = jnp.sum(f_row, axis=1, keepdims=True)
    ws_col = b_last - b_col + i_col
    m_new = jnp.maximum(b_last + m_prev, jnp.max(ws_col, axis=0, keepdims=True))
    decay = jnp.exp(b_last + m_prev - m_new)
    ws = jnp.exp(ws_col - m_new)
    vw = (v.astype(F32) * ws).astype(BF16)
    upd = lax.dot_general(vw, k, (((0,), (0,)), ((), ())), preferred_element_type=F32)
    c_scr[...] = decay * c_scr[...] + upd
    n_scr[...] = decay * n_scr[...] + jnp.sum(k.astype(F32) * ws, axis=0, keepdims=True)
    m_scr[...] = m_new


def _mlstm(q, k, v, gcol, grow, B, S):
    T, di = q.shape
    nh = gcol.shape[1]
    dh = di // nh
    L = MLSTM_CHUNK
    nc = S // L
    blk = pl.BlockSpec((L, di), lambda b, c: (b * nc + c, 0))
    return pl.pallas_call(
        _mlstm_kernel,
        grid=(B, nc),
        in_specs=[blk, blk, blk,
                  pl.BlockSpec((1, nh, L, 2), lambda b, c: (b, 0, c, 0)),
                  pl.BlockSpec((1, nh, 2, L), lambda b, c: (b, 0, 0, c))],
        out_specs=blk,
        out_shape=jax.ShapeDtypeStruct((T, di), BF16),
        scratch_shapes=[pltpu.VMEM((nh, dh, dh), F32), pltpu.VMEM((nh, 1, dh), F32), pltpu.VMEM((nh, 1, 1), F32)],
        compiler_params=_params(("parallel", "arbitrary")),
        name="mlstm_scan",
    )(q, k, v, gcol, grow)


def _mlstm_out_kernel(hn_ref, xc_ref, z_ref, x_ref, gn_ref, skip_ref, w_ref, gate_ref, lg_ref, lb_ref, o_ref):
    z = z_ref[...].astype(F32)
    u = (hn_ref[...].astype(F32) * gn_ref[...] + skip_ref[...] * xc_ref[...].astype(F32)) * (z * _sigmoid(z))
    y = jnp.dot(u.astype(BF16), w_ref[...], preferred_element_type=F32)
    r = DEEPNORM_ALPHA * x_ref[...] + gate_ref[0] * y
    o_ref[...] = _layer_norm(r, lg_ref[...], lb_ref[...])


def _mlstm_out(hn, xc, z, x2, gn_g, skip, w_out, gate, ln_g, ln_b, S):
    T, D = x2.shape
    di = hn.shape[1]
    tm = min(512, S)
    tpb = S // tm
    row = lambda i: (i, 0)
    full2 = lambda i: (0, 0)
    return pl.pallas_call(
        _mlstm_out_kernel,
        grid=(T // tm,),
        in_specs=[pl.BlockSpec((tm, di), row), pl.BlockSpec((tm, di), row), pl.BlockSpec((tm, di), row),
                  pl.BlockSpec((tm, D), row),
                  pl.BlockSpec((1, di), full2), pl.BlockSpec((1, di), full2),
                  pl.BlockSpec((di, D), full2),
                  pl.BlockSpec((1, 1, D), lambda i: (i // tpb, 0, 0)),
                  pl.BlockSpec((1, D), full2), pl.BlockSpec((1, D), full2)],
        out_specs=pl.BlockSpec((tm, D), row),
        out_shape=jax.ShapeDtypeStruct((T, D), F32),
        compiler_params=_params(("parallel",)),
        name="mlstm_out_ln",
    )(hn, xc, z, x2, gn_g.reshape(1, di), skip.reshape(1, di), w_out, gate, ln_g.reshape(1, D), ln_b.reshape(1, D))


def _route_kernel(x_ref, sh_ref, sc_ref, wrt_ref, br_ref, tri_ref, out_ref, cnt_ref):
    E = wrt_ref.shape[0]
    epg = E // N_GROUPS
    h = x_ref[...] * (1.0 + sc_ref[0]) + sh_ref[0]
    lt = lax.dot_general(wrt_ref[...], h, (((1,), (1,)), ((), ())),
                         precision=lax.Precision.HIGHEST, preferred_element_type=F32)
    ex = jnp.exp(lt - jnp.max(lt, axis=0, keepdims=True))
    probs = ex / jnp.sum(ex, axis=0, keepdims=True)
    sel = probs + br_ref[...]
    srow = [sel[e:e + 1, :] for e in range(E)]
    prow = [probs[e:e + 1, :] for e in range(E)]
    gscore = []
    for g in range(N_GROUPS):
        r = srow[g * epg:(g + 1) * epg]
        best = None
        for a in range(epg):
            for b in range(a + 1, epg):
                pair = r[a] + r[b]
                best = pair if best is None else jnp.maximum(best, pair)
        gscore.append(best)
    gmax = functools.reduce(jnp.maximum, gscore)
    chosen, taken = [], None
    for g in range(N_GROUPS):
        c = gscore[g] == gmax
        if taken is not None:
            c = c & jnp.logical_not(taken)
        taken = c if taken is None else (taken | c)
        chosen.append(c)
    zero = jnp.zeros_like(srow[0])
    e1 = zero
    e2 = zero
    p1 = zero
    p2 = zero
    firsts, seconds = [], []
    for e in range(E):
        g = e // epg
        rank = zero
        for o in range(g * epg, (g + 1) * epg):
            if o == e:
                continue
            beats = (srow[o] > srow[e]) | ((srow[o] == srow[e]) & (o < e))
            rank = rank + jnp.where(beats, 1.0, 0.0)
        is1 = chosen[g] & (rank == 0.0)
        is2 = chosen[g] & (rank == 1.0)
        firsts.append(is1)
        seconds.append(is2)
        e1 = e1 + jnp.where(is1, float(e), 0.0)
        e2 = e2 + jnp.where(is2, float(e), 0.0)
        p1 = p1 + jnp.where(is1, prow[e], 0.0)
        p2 = p2 + jnp.where(is2, prow[e], 0.0)
    mask = jnp.concatenate([jnp.where(firsts[e] | seconds[e], 1.0, 0.0) for e in range(E)], axis=0)
    prefix = jnp.dot(mask.astype(BF16), tri_ref[...], preferred_element_type=F32)
    r1 = zero
    r2 = zero
    for e in range(E):
        pe = prefix[e:e + 1, :]
        r1 = r1 + jnp.where(firsts[e], pe, 0.0)
        r2 = r2 + jnp.where(seconds[e], pe, 0.0)
    psum = p1 + p2
    out_ref[...] = jnp.concatenate([e1, e2, p1 / psum, p2 / psum, r1, r2, zero, zero], axis=0)
    cnt_ref[...] = jnp.broadcast_to(jnp.sum(mask, axis=1, keepdims=True), cnt_ref.shape)


def _route(x2, shift, scale, w_router, b_router, S):
    T, D = x2.shape
    E = w_router.shape[1]
    tm = min(MOE_TILE, S)
    tpb = S // tm
    tri = (jnp.arange(tm)[:, None] < jnp.arange(tm)[None, :]).astype(BF16)
    return pl.pallas_call(
        _route_kernel,
        grid=(T // tm,),
        in_specs=[pl.BlockSpec((tm, D), lambda i: (i, 0)),
                  pl.BlockSpec((1, 1, D), lambda i: (i // tpb, 0, 0)),
                  pl.BlockSpec((1, 1, D), lambda i: (i // tpb, 0, 0)),
                  pl.BlockSpec((E, D), lambda i: (0, 0)),
                  pl.BlockSpec((E, 1), lambda i: (0, 0)),
                  pl.BlockSpec((tm, tm), lambda i: (0, 0))],
        out_specs=[pl.BlockSpec((8, tm), lambda i: (0, i)),
                   pl.BlockSpec((E, LANES), lambda i: (0, i))],
        out_shape=[jax.ShapeDtypeStruct((8, T), F32), jax.ShapeDtypeStruct((E, (T // tm) * LANES), F32)],
        compiler_params=_params(("parallel",)),
        name="moe_route",
    )(x2, shift, scale, w_router.T, b_router.reshape(E, 1), tri)


def _stage_rows(tm, n_exp):
    return MOE_TOPK * tm + n_exp * RUN_ALIGN


def _chunk_copies(tc_ref, ts_ref, ds_ref, step, n_exp, make):
    out = []
    for e in range(n_exp):
        cnt = tc_ref[step * n_exp + e]
        off = ts_ref[step * n_exp + e]
        dst = ds_ref[step * n_exp + e]
        for k in reversed(range(RUN_ALIGN.bit_length() - 1, CHUNK_BITS)):
            done = (cnt >> (k + 1)) << (k + 1)
            out.append(((cnt & (1 << k)) != 0,
                        make(pl.multiple_of(off + done, RUN_ALIGN), pl.multiple_of(dst + done, RUN_ALIGN), 1 << k)))
    return out


def _dispatch_kernel(tc_ref, ts_ref, ds_ref, pe_ref, x_ref, sh_ref, sc_ref, pos_ref, xs_ref,
                     stage, zbuf, sem, zsem):
    tm = x_ref.shape[0]
    rb = zbuf.shape[0]
    i = pl.program_id(0)
    slot = i % 2

    n_exp = pe_ref.shape[0] - 1
    n_blocks = xs_ref.shape[0] // rb

    def pad_copy(e):
        return pltpu.make_async_copy(zbuf, xs_ref.at[pl.ds(pl.multiple_of(pe_ref[e + 1] - rb, rb), rb)], zsem)

    def tail_copy(j):
        return pltpu.make_async_copy(zbuf, xs_ref.at[pl.ds(pl.multiple_of(pe_ref[n_exp] + j * rb, rb), rb)], zsem)

    def tail_live(j):
        return pe_ref[n_exp] // rb + j < n_blocks

    @pl.when(i == 0)
    def _():
        zbuf[...] = jnp.zeros_like(zbuf)
        for e in range(n_exp):
            @pl.when(pe_ref[e + 1] > pe_ref[e])
            def _():
                pad_copy(e).start()
        for j in range(n_exp):
            @pl.when(tail_live(j))
            def _():
                tail_copy(j).start()
        for e in range(n_exp):
            @pl.when(pe_ref[e + 1] > pe_ref[e])
            def _():
                pad_copy(e).wait()
        for j in range(n_exp):
            @pl.when(tail_live(j))
            def _():
                tail_copy(j).wait()

    h = (x_ref[...] * (1.0 + sc_ref[0]) + sh_ref[0]).astype(BF16)
    pos = pos_ref[0]
    slot_id = lax.broadcasted_iota(jnp.int32, (stage.shape[1], tm), 0)
    perm = jnp.where((slot_id == pos[0:1, :]) | (slot_id == pos[1:2, :]), 1.0, 0.0).astype(BF16)
    stage[slot] = jnp.dot(perm, h, preferred_element_type=F32)

    def copies(step, s):
        def make(src_row, dst_row, size):
            return pltpu.make_async_copy(stage.at[s, pl.ds(src_row, size)], xs_ref.at[pl.ds(dst_row, size)], sem.at[s])
        return _chunk_copies(tc_ref, ts_ref, ds_ref, step, n_exp, make)

    for live, cp in copies(i, slot):
        pl.when(live)(cp.start)
    for live, cp in copies(jnp.maximum(i - 1, 0), 1 - slot):
        pl.when(live & (i > 0))(cp.wait)
    for live, cp in copies(i, slot):
        pl.when(live & (i == pl.num_programs(0) - 1))(cp.wait)


def _dispatch(x2, shift, scale, posrow, tcnt, tstart, dstart, pends0, cap, S):
    T, D = x2.shape
    tm = min(MOE_TILE, S)
    tpb = S // tm
    mod = pl.BlockSpec((1, 1, D), lambda i, *_: (i // tpb, 0, 0))
    grid_spec = pltpu.PrefetchScalarGridSpec(
        num_scalar_prefetch=4,
        grid=(T // tm,),
        in_specs=[pl.BlockSpec((tm, D), lambda i, *_: (i, 0)), mod, mod,
                  pl.BlockSpec((1, 2, tm), lambda i, *_: (i, 0, 0))],
        out_specs=pl.BlockSpec(memory_space=pl.ANY),
        scratch_shapes=[pltpu.VMEM((2, _stage_rows(tm, pends0.shape[0] - 1), D), F32),
                        pltpu.VMEM((MOE_ROW_BLOCK, D), F32),
                        pltpu.SemaphoreType.DMA((2,)), pltpu.SemaphoreType.DMA(())],
    )
    return pl.pallas_call(
        _dispatch_kernel,
        grid_spec=grid_spec,
        out_shape=jax.ShapeDtypeStruct((cap, D), F32),
        compiler_params=_params(("arbitrary",)),
        name="moe_dispatch",
    )(tcnt, tstart, dstart, pends0, x2, shift, scale, posrow)


def _expert_kernel(be_ref, na_ref, xs_ref, wg_ref, wu_ref, wd_ref, ys_ref):
    del be_ref

    @pl.when(pl.program_id(0) < na_ref[0])
    def _():
        xb = xs_ref[...].astype(BF16)
        g = jnp.dot(xb, wg_ref[0], preferred_element_type=F32)
        u = jnp.dot(xb, wu_ref[0], preferred_element_type=F32)
        a = (g * _sigmoid(g)) * u
        ys_ref[...] = jnp.dot(a.astype(BF16), wd_ref[0], preferred_element_type=F32)

    @pl.when(pl.program_id(0) >= na_ref[0])
    def _():
        ys_ref[...] = jnp.zeros_like(ys_ref)


def _experts(xs, block_expert, n_active, w_gate, w_up, w_down):
    cap, D = xs.shape
    E, _, F = w_gate.shape
    rb = MOE_ROW_BLOCK
    nb = cap // rb

    def blk(i, be, na):
        return jnp.maximum(jnp.minimum(i, na[0] - 1), 0)

    grid_spec = pltpu.PrefetchScalarGridSpec(
        num_scalar_prefetch=2,
        grid=(nb,),
        in_specs=[pl.BlockSpec((rb, D), lambda i, be, na: (blk(i, be, na), 0)),
                  pl.BlockSpec((1, D, F), lambda i, be, na: (be[blk(i, be, na)], 0, 0)),
                  pl.BlockSpec((1, D, F), lambda i, be, na: (be[blk(i, be, na)], 0, 0)),
                  pl.BlockSpec((1, F, D), lambda i, be, na: (be[blk(i, be, na)], 0, 0))],
        out_specs=pl.BlockSpec((rb, D), lambda i, be, na: (i, 0)),
    )
    return pl.pallas_call(
        _expert_kernel,
        grid_spec=grid_spec,
        out_shape=jax.ShapeDtypeStruct((cap, D), F32),
        compiler_params=_params(("arbitrary",)),
        name="moe_experts",
    )(block_expert, n_active, xs, w_gate, w_up, w_down)


def _combine_kernel(tc_ref, ts_ref, ds_ref, ys_ref, rt_ref, x_ref, gate_ref, lg_ref, lb_ref, o_ref, stage, sem,
                    *, n_exp):
    tm = x_ref.shape[0]
    i = pl.program_id(0)
    slot = i % 2

    def copies(step, s):
        def make(stage_row, ys_row, size):
            return pltpu.make_async_copy(ys_ref.at[pl.ds(ys_row, size)], stage.at[s, pl.ds(stage_row, size)], sem.at[s])
        return _chunk_copies(tc_ref, ts_ref, ds_ref, step, n_exp, make)

    @pl.when(i == 0)
    def _():
        stage[...] = jnp.zeros_like(stage)

    for live, cp in copies(i, slot):
        pl.when(live & (i == 0))(cp.start)
    nxt = jnp.minimum(i + 1, pl.num_programs(0) - 1)
    for live, cp in copies(nxt, 1 - slot):
        pl.when(live & (i + 1 < pl.num_programs(0)))(cp.start)
    for live, cp in copies(i, slot):
        pl.when(live)(cp.wait)

    rt = rt_ref[...]
    ysb = stage[slot].astype(BF16)
    slot_id = lax.broadcasted_iota(jnp.int32, (tm, stage.shape[1]), 1).astype(F32)
    y = None
    for j in range(MOE_TOPK):
        perm = jnp.where(slot_id == rt[:, 6 + j:7 + j], 1.0, 0.0).astype(BF16)
        part = rt[:, 2 + j:3 + j] * jnp.dot(perm, ysb, preferred_element_type=F32)
        y = part if y is None else y + part
    r = DEEPNORM_ALPHA * x_ref[...] + gate_ref[0] * y
    o_ref[...] = _layer_norm(r, lg_ref[...], lb_ref[...])


def _combine(ys, route_t, x2, gate, ln_g, ln_b, tcnt, tstart, dstart, S):
    T, D = x2.shape
    tm = min(MOE_TILE, S)
    tpb = S // tm
    grid_spec = pltpu.PrefetchScalarGridSpec(
        num_scalar_prefetch=3,
        grid=(T // tm,),
        in_specs=[pl.BlockSpec(memory_space=pl.ANY),
                  pl.BlockSpec((tm, 8), lambda i, *_: (i, 0)),
                  pl.BlockSpec((tm, D), lambda i, *_: (i, 0)),
                  pl.BlockSpec((1, 1, D), lambda i, *_: (i // tpb, 0, 0)),
                  pl.BlockSpec((1, D), lambda i, *_: (0, 0)),
                  pl.BlockSpec((1, D), lambda i, *_: (0, 0))],
        out_specs=pl.BlockSpec((tm, D), lambda i, *_: (i, 0)),
        scratch_shapes=[pltpu.VMEM((2, _stage_rows(tm, tcnt.shape[0] // (T // tm)), D), F32),
                        pltpu.SemaphoreType.DMA((2,))],
    )
    return pl.pallas_call(
        functools.partial(_combine_kernel, n_exp=tcnt.shape[0] // (T // tm)),
        grid_spec=grid_spec,
        out_shape=jax.ShapeDtypeStruct((T, D), F32),
        compiler_params=_params(("arbitrary",)),
        name="moe_combine_ln",
    )(tcnt, tstart, dstart, ys, route_t, x2, gate, ln_g.reshape(1, D), ln_b.reshape(1, D))


def _moe_layer(x2, shift, scale, gate, ln_g, ln_b, w_router, b_router, w_gate, w_up, w_down, S):
    T, D = x2.shape
    E = w_router.shape[1]
    rb = MOE_ROW_BLOCK
    tm = min(MOE_TILE, S)
    nt = T // tm
    assert tm < (1 << CHUNK_BITS)
    route, cnt = _route(x2, shift, scale, w_router, b_router, S)
    tcnt = cnt.reshape(E, nt, LANES)[:, :, 0].T.astype(jnp.int32)
    tcnt = (tcnt + RUN_ALIGN - 1) // RUN_ALIGN * RUN_ALIGN
    base = jnp.cumsum(tcnt, axis=0) - tcnt
    counts = jnp.sum(tcnt, axis=0)
    padded = (counts + rb - 1) // rb * rb
    pends = jnp.cumsum(padded)
    pstarts = pends - padded
    tstart = jnp.cumsum(tcnt, axis=1) - tcnt
    dstart = pstarts[None, :] + base
    tile_id = jnp.arange(T, dtype=jnp.int32) // tm
    e1 = route[0].astype(jnp.int32)
    e2 = route[1].astype(jnp.int32)
    pos1 = tstart[tile_id, e1] + route[4].astype(jnp.int32)
    pos2 = tstart[tile_id, e2] + route[5].astype(jnp.int32)
    posrow = jnp.stack([pos1.reshape(nt, tm), pos2.reshape(nt, tm)], axis=1)
    route_t = jnp.concatenate([route[:6].T, pos1.astype(F32)[:, None], pos2.astype(F32)[:, None]], axis=1)
    nb = -(-(T * MOE_TOPK + nt * E * (RUN_ALIGN - 1)) // rb) + E
    cap = nb * rb
    block_start = jnp.arange(nb, dtype=jnp.int32) * rb
    block_expert = jnp.minimum(jnp.sum(block_start[:, None] >= pends[None, :], axis=1), E - 1).astype(jnp.int32)
    n_active = (pends[-1:] // rb).astype(jnp.int32)
    pends0 = jnp.concatenate([jnp.zeros((1,), jnp.int32), pends.astype(jnp.int32)])
    tables = [t.reshape(-1).astype(jnp.int32) for t in (tcnt, tstart, dstart)]
    xs = _dispatch(x2, shift, scale, posrow, *tables, pends0, cap, S)
    ys = _experts(xs, block_expert, n_active, w_gate, w_up, w_down)
    return _combine(ys, route_t, x2, gate, ln_g, ln_b, *tables, S)


def _moba_proj_kernel(x_ref, shq_ref, scq_ref, shkv_ref, sckv_ref, wq_ref, wkv_ref,
                      q_ref, k_ref, vt_ref, km_ref, *, qscale, tpb, slopes):
    tm, D = x_ref.shape
    hd = HEAD_DIM
    nh = D // hd
    x = x_ref[...]
    hq = x * (1.0 + scq_ref[0]) + shq_ref[0]
    hkv = x * (1.0 + sckv_ref[0]) + shkv_ref[0]
    qf = jnp.dot(hq.astype(BF16), wq_ref[...], preferred_element_type=F32) * qscale
    kv = jnp.dot(hkv.astype(BF16), wkv_ref[...], preferred_element_type=F32)
    kf = kv[:, :D]
    vt = kv[:, D:].T
    ones_pad = jnp.where(lax.broadcasted_iota(jnp.int32, (V_ROWS - hd, tm), 0) == 0, 1.0, 0.0)
    for g in range(nh):
        vt_ref[0, g * V_ROWS:(g + 1) * V_ROWS, :] = jnp.concatenate(
            [vt[g * hd:(g + 1) * hd, :], ones_pad], axis=0).astype(BF16)
    lane = lax.broadcasted_iota(jnp.int32, (1, LANES), 1)
    head_lanes = lane >= hd
    pos = (pl.program_id(0) % tpb) * tm + lax.broadcasted_iota(jnp.int32, (tm, 1), 0)
    pos_f = pos.astype(F32)
    onehot = jnp.where(lane == pos // MOBA_BLOCK, 1.0, 0.0)
    q_fill = jnp.where((lane >= BIAS_LANE) & (lane < BIAS_LANE + 3), 1.0, 0.0)
    nbt = tm // MOBA_BLOCK
    means = [jnp.mean(kf[j * MOBA_BLOCK:(j + 1) * MOBA_BLOCK, :], axis=0, keepdims=True) for j in range(nbt)]
    km = jnp.concatenate(means + [jnp.zeros((8 - nbt, D), F32)], axis=0)
    for j in range(nh // 2):
        cs = slice(j * LANES, (j + 1) * LANES)
        qb, kb, mb = qf[:, cs], kf[:, cs], km[:, cs]
        q_pair = (pltpu.roll(qb, hd, axis=1), qb)
        k_pair = (pltpu.roll(kb, hd, axis=1), kb)
        m_pair = (pltpu.roll(mb, hd, axis=1), mb)
        for e in range(2):
            g = 2 * j + e
            gs = slice(g * LANES, (g + 1) * LANES)
            q_ref[:, gs] = jnp.where(head_lanes, q_pair[e], q_fill).astype(BF16)
            c = (slopes[g] * LOG2E) * pos_f
            hi = c.astype(BF16).astype(F32)
            mid = (c - hi).astype(BF16).astype(F32)
            lo = c - hi - mid
            aug = jnp.where(lane == BIAS_LANE, hi,
                            jnp.where(lane == BIAS_LANE + 1, mid,
                                      jnp.where(lane == BIAS_LANE + 2, lo, onehot)))
            k_ref[:, gs] = jnp.where(head_lanes, k_pair[e], aug).astype(BF16)
            mg = jnp.where(head_lanes, m_pair[e], 0.0)
            for jb in range(nbt):
                km_ref[jb, :, gs] = mg[jb:jb + 1, :]


def _moba_proj(x2, shq, scq, shkv, sckv, wq, wkv, B, S):
    T, D = x2.shape
    nh = D // HEAD_DIM
    tm = min(512, S)
    tpb = S // tm
    nbt = tm // MOBA_BLOCK
    assert S // MOBA_BLOCK <= BIAS_LANE and nbt <= 8
    slopes = tuple(2.0 ** (-8.0 * (h + 1.0) / nh) for h in range(nh))
    mod = pl.BlockSpec((1, 1, D), lambda i: (i // tpb, 0, 0))
    kern = functools.partial(_moba_proj_kernel, qscale=float(HEAD_DIM) ** -0.5 * LOG2E, tpb=tpb, slopes=slopes)
    return pl.pallas_call(
        kern,
        grid=(T // tm,),
        in_specs=[pl.BlockSpec((tm, D), lambda i: (i, 0)), mod, mod, mod, mod,
                  pl.BlockSpec((D, D), lambda i: (0, 0)),
                  pl.BlockSpec((D, 2 * D), lambda i: (0, 0))],
        out_specs=[pl.BlockSpec((tm, nh * LANES), lambda i: (i, 0)),
                   pl.BlockSpec((tm, nh * LANES), lambda i: (i, 0)),
                   pl.BlockSpec((1, nh * V_ROWS, tm), lambda i: (i // tpb, 0, i % tpb)),
                   pl.BlockSpec((nbt, 1, nh * LANES), lambda i: (i, 0, 0))],
        out_shape=[jax.ShapeDtypeStruct((T, nh * LANES), BF16), jax.ShapeDtypeStruct((T, nh * LANES), BF16),
                   jax.ShapeDtypeStruct((B, nh * V_ROWS, S), BF16),
                   jax.ShapeDtypeStruct((T // MOBA_BLOCK, 1, nh * LANES), F32)],
        compiler_params=_params(("parallel",)),
        name="moba_proj",
    )(x2, shq, scq, shkv, sckv, wq, wkv)


def _moba_select_kernel(q_ref, km_ref, qat_ref):
    tq = q_ref.shape[0]
    G = q_ref.shape[1] // LANES
    nblk = km_ref.shape[1]
    blk_id = lax.broadcasted_iota(jnp.int32, (nblk, tq), 0)
    own = (pl.program_id(2) * tq + lax.broadcasted_iota(jnp.int32, (1, tq), 1)) // MOBA_BLOCK
    for g in range(G):
        gs = slice(g * LANES, (g + 1) * LANES)
        qgt = q_ref[:, gs].astype(F32).T
        gate = jnp.dot(km_ref[0, :, gs], qgt, precision=lax.Precision.HIGHEST,
                       preferred_element_type=F32)
        gsc = jnp.where(blk_id < own, gate, -jnp.inf)
        picked = blk_id < 0
        for _ in range(MOBA_TOPK):
            mx = jnp.max(gsc, axis=0, keepdims=True)
            first = jnp.min(jnp.where(gsc == mx, blk_id, nblk), axis=0, keepdims=True)
            pick = blk_id == first
            picked = picked | pick
            gsc = jnp.where(pick, -jnp.inf, gsc)
        visible = (picked & (blk_id < own)) | (blk_id == own)
        pen = jnp.where(visible, 0.0, NEG)
        qat_ref[0, g] = (qgt + jnp.concatenate([pen, jnp.zeros((LANES - nblk, tq), F32)], axis=0)).astype(BF16)


def _moba_select(q, km, B, S):
    nh = q.shape[1] // LANES
    G = MOBA_HEADS_PER_STEP
    tq = min(512, S)
    nblk = S // MOBA_BLOCK
    return pl.pallas_call(
        _moba_select_kernel,
        grid=(B, nh // G, S // tq),
        in_specs=[pl.BlockSpec((tq, G * LANES), lambda b, hp, i: (b * (S // tq) + i, hp)),
                  pl.BlockSpec((1, nblk, G * LANES), lambda b, hp, i: (b, 0, hp))],
        out_specs=pl.BlockSpec((1, G, LANES, tq), lambda b, hp, i: (b, hp, 0, i)),
        out_shape=jax.ShapeDtypeStruct((B, nh, LANES, S), BF16),
        compiler_params=_params(("parallel", "parallel", "parallel")),
        name="moba_select",
    )(q, km)


def _moba_kernel(qat_ref, k_ref, vt_ref, o_ref, st_scr, p_scr, m_scr, acc_scr):
    G = qat_ref.shape[1]
    BLK = qat_ref.shape[3]
    hd = HEAD_DIM
    nhalf = BLK // LANES
    own = pl.program_id(2)
    k_off = lax.broadcasted_iota(jnp.int32, (BLK, 1), 0)
    q_off = lax.broadcasted_iota(jnp.int32, (1, LANES), 1)

    m_scr[...] = jnp.full(m_scr.shape, NEG, F32)
    acc_scr[...] = jnp.zeros_like(acc_scr)

    def scores(n, slot):
        start = pl.multiple_of(n * BLK, BLK)
        for g in range(G):
            kn = k_ref[pl.ds(start, BLK), g * LANES:(g + 1) * LANES]
            st_scr[slot, g] = jnp.dot(kn, qat_ref[0, g], preferred_element_type=F32)

    def update(n, slot, causal):
        start = pl.multiple_of(n * BLK, BLK)
        for g in range(G):
            alphas = []
            for hf in range(nhalf):
                c = g * nhalf + hf
                ls = slice(hf * LANES, (hf + 1) * LANES)
                st = st_scr[slot, g, :, ls]
                if causal:
                    st = jnp.where(k_off <= q_off + hf * LANES, st, NEG)
                m = m_scr[c]
                m_new = jnp.maximum(m, jnp.max(st, axis=0, keepdims=True))
                alpha = jnp.exp2(m - m_new)
                p = jnp.exp2(st - m_new)
                p_scr[g, :, ls] = p.astype(BF16)
                alphas.append(alpha)
                m_scr[c] = m_new
            vtn = vt_ref[0, g * V_ROWS:(g + 1) * V_ROWS, pl.ds(start, BLK)]
            pv = jnp.dot(vtn, p_scr[g], preferred_element_type=F32)
            acc_scr[g] = jnp.concatenate(alphas, axis=1) * acc_scr[g] + pv

    scores(0, 0)

    def step(i, carry):
        n = 2 * i
        scores(n + 1, 1)
        update(n, 0, False)
        scores(n + 2, 0)
        update(n + 1, 1, False)
        return carry

    lax.fori_loop(0, own // 2, step, 0)

    @pl.when(own % 2 == 0)
    def _():
        update(own, 0, True)

    @pl.when(own % 2 == 1)
    def _():
        scores(own, 1)
        update(own - 1, 0, False)
        update(own, 1, True)
    for g in range(G):
        acc = acc_scr[g]
        o_ref[0, g * hd:(g + 1) * hd, :] = (acc[:hd, :] * (1.0 / acc[hd:hd + 1, :])).astype(BF16)


def _moba_attn(qat, k, vt, B, S):
    nh = qat.shape[1]
    G = MOBA_HEADS_PER_STEP
    BLK = MOBA_BLOCK
    nblk = S // BLK
    hd = HEAD_DIM
    return pl.pallas_call(
        _moba_kernel,
        grid=(B, nh // G, nblk),
        in_specs=[pl.BlockSpec((1, G, LANES, BLK), lambda b, hp, qi: (b, hp, 0, qi)),
                  pl.BlockSpec((S, G * LANES), lambda b, hp, qi: (b, hp)),
                  pl.BlockSpec((1, G * V_ROWS, S), lambda b, hp, qi: (b, hp, 0))],
        out_specs=pl.BlockSpec((1, G * hd, BLK), lambda b, hp, qi: (b, hp, qi)),
        out_shape=jax.ShapeDtypeStruct((B, nh * hd, S), BF16),
        scratch_shapes=[pltpu.VMEM((2, G, BLK, BLK), F32),
                        pltpu.VMEM((G, BLK, BLK), BF16),
                        pltpu.VMEM((G * (BLK // LANES), 1, LANES), F32),
                        pltpu.VMEM((G, V_ROWS, BLK), F32)],
        compiler_params=_params(("parallel", "parallel", "arbitrary")),
        name="moba_attn",
    )(qat, k, vt)


def _attn_out_kernel(a_ref, x_ref, w_ref, gate_ref, lg_ref, lb_ref, o_ref):
    y = lax.dot_general(a_ref[0], w_ref[...], (((0,), (0,)), ((), ())), preferred_element_type=F32)
    r = DEEPNORM_ALPHA * x_ref[...] + gate_ref[0] * y
    o_ref[...] = _layer_norm(r, lg_ref[...], lb_ref[...])


def _attn_out(at, x2, w, gate, ln_g, ln_b, S):
    T, D = x2.shape
    K = at.shape[1]
    tm = min(512, S)
    tpb = S // tm
    row = lambda i: (i, 0)
    full2 = lambda i: (0, 0)
    return pl.pallas_call(
        _attn_out_kernel,
        grid=(T // tm,),
        in_specs=[pl.BlockSpec((1, K, tm), lambda i: (i // tpb, 0, i % tpb)), pl.BlockSpec((tm, D), row),
                  pl.BlockSpec((K, D), full2),
                  pl.BlockSpec((1, 1, D), lambda i: (i // tpb, 0, 0)),
                  pl.BlockSpec((1, D), full2), pl.BlockSpec((1, D), full2)],
        out_specs=pl.BlockSpec((tm, D), row),
        out_shape=jax.ShapeDtypeStruct((T, D), F32),
        compiler_params=_params(("parallel",)),
        name="attn_out_ln",
    )(at, x2, w, gate, ln_g.reshape(1, D), ln_b.reshape(1, D))


def kernel(x, c, w_ada, b_ada, ln_g, ln_b, a_w_in, a_conv_w, a_conv_b, a_wq, a_wk, a_wv, a_w_if, a_b_if,
           a_gn_g, a_skip, a_w_out, b_w_kv, b_wq, b_wo, moe_w_router, moe_b_router, moe_w_gate, moe_w_up,
           moe_w_down):
    B, S, D = x.shape
    T = B * S
    assert DEPTH == 2 and S % MOBA_BLOCK == 0 and S % MLSTM_CHUNK == 0
    n_layer_mod = DEPTH * N_MOD_PER_LAYER * D
    cond = _ada_cond(c, w_ada, b_ada)
    mods = cond[:, :n_layer_mod].reshape(B, DEPTH, 2, 3, 1, D)
    kv_mod = cond[:, n_layer_mod:].reshape(B, 2, 1, D)

    def mod3(layer, sub):
        m = mods[:, layer, sub]
        return m[:, 0], m[:, 1], 1.0 + m[:, 2]

    xf = x.reshape(T, D)

    shift, scale, gate = mod3(0, 0)
    nh = MLSTM_HEADS
    xm, z = _inproj(xf, shift, scale, a_w_in[0].astype(BF16), S)
    q, k, v, xc, gts = _qkv(xm, a_conv_w[0], a_conv_b[0], a_wq[0].astype(BF16), a_wk[0].astype(BF16),
                            a_wv[0].astype(BF16), a_w_if[0], a_b_if[0], S)
    g4 = gts.reshape(B, S, 2, nh)
    gcol = jnp.transpose(g4, (0, 3, 1, 2))
    grow = jnp.transpose(g4, (0, 3, 2, 1))
    hn = _mlstm(q, k, v, gcol, grow, B, S)
    xf = _mlstm_out(hn, xc, z, xf, a_gn_g[0], a_skip[0], a_w_out[0].astype(BF16), gate, ln_g[0, 0], ln_b[0, 0], S)
    shift, scale, gate = mod3(0, 1)
    xf = _moe_layer(xf, shift, scale, gate, ln_g[0, 1], ln_b[0, 1], moe_w_router, moe_b_router,
                    moe_w_gate[0].astype(BF16), moe_w_up[0].astype(BF16), moe_w_down[0].astype(BF16), S)

    shift, scale, gate = mod3(1, 0)
    q, k, vt, km = _moba_proj(xf, shift, scale, kv_mod[:, 0], kv_mod[:, 1], b_wq[0].astype(BF16),
                              b_w_kv.astype(BF16), B, S)
    km = km.reshape(B, S // MOBA_BLOCK, km.shape[-1])
    attn_t = _moba_attn(_moba_select(q, km, B, S), k, vt, B, S)
    xf = _attn_out(attn_t, xf, b_wo[0].astype(BF16), gate, ln_g[1, 0], ln_b[1, 0], S)
    shift, scale, gate = mod3(1, 1)
    xf = _moe_layer(xf, shift, scale, gate, ln_g[1, 1], ln_b[1, 1], moe_w_router, moe_b_router,
                    moe_w_gate[1].astype(BF16), moe_w_up[1].astype(BF16), moe_w_down[1].astype(BF16), S)
    return xf.reshape(B, S, D)
```

```python
import functools

import jax
import jax.numpy as jnp
from jax import lax
from jax.experimental import pallas as pl
from jax.experimental.pallas import tpu as pltpu

DEPTH = 2
MLSTM_HEADS = 4
CONV_WIDTH = 4
MLSTM_CHUNK = 256
ATTN_HEADS = 16
MOBA_BLOCK = 256
MOBA_TOPK = 3
N_EXPERTS = 16
N_GROUPS = 4
MOE_TOPK = 2
MOE_ROW_BLOCK = 256
MOE_TILE = 512
CHUNK_BITS = 10
RUN_ALIGN = 8
DEEPNORM_ALPHA = (2.0 * DEPTH) ** 0.25
LN_EPS = 1e-5
N_MOD_PER_LAYER = 6

HEAD_DIM = 64
V_ROWS = 80
MOBA_HEADS_PER_STEP = 4
BIAS_LANE = 32
LOG2E = 1.4426950408889634

LANES = 128
CONV_HALO = 16
NEG = -1e30
VMEM_LIMIT = 56 * 1024 * 1024

F32 = jnp.float32
BF16 = jnp.bfloat16


def _sigmoid(x):
    return 1.0 / (1.0 + jnp.exp(-x))


def _params(sem, vmem=VMEM_LIMIT):
    return pltpu.CompilerParams(dimension_semantics=sem, vmem_limit_bytes=vmem)


def _layer_norm(r, g, b):
    mu = jnp.mean(r, axis=-1, keepdims=True)
    d = r - mu
    var = jnp.mean(d * d, axis=-1, keepdims=True)
    return d * lax.rsqrt(var + LN_EPS) * g + b


def _ada_kernel(c_ref, w_ref, b_ref, o_ref):
    c = c_ref[...]
    s = c * _sigmoid(c)
    o_ref[...] = jnp.dot(s.astype(BF16), w_ref[...].astype(BF16),
                         preferred_element_type=F32) + b_ref[...]


def _ada_cond(c, w_ada, b_ada):
    B, D = c.shape
    N = w_ada.shape[1]
    tn = 2048 if N % 2048 == 0 else N
    cp = jnp.zeros((8, D), F32).at[:B].set(c)
    out = pl.pallas_call(
        _ada_kernel,
        grid=(N // tn,),
        in_specs=[pl.BlockSpec((8, D), lambda j: (0, 0)),
                  pl.BlockSpec((D, tn), lambda j: (0, j)),
                  pl.BlockSpec((1, tn), lambda j: (0, j))],
        out_specs=pl.BlockSpec((8, tn), lambda j: (0, j)),
        out_shape=jax.ShapeDtypeStruct((8, N), F32),
        compiler_params=_params(("arbitrary",)),
        name="ada_cond",
    )(cp, w_ada, b_ada.reshape(1, N))
    return out[:B]


def _inproj_kernel(x_ref, sh_ref, sc_ref, w_ref, xm_ref, z_ref):
    di = xm_ref.shape[-1]
    h = x_ref[...] * (1.0 + sc_ref[0]) + sh_ref[0]
    r = jnp.dot(h.astype(BF16), w_ref[...], preferred_element_type=F32)
    xm_ref[...] = r[:, :di].astype(BF16)
    z_ref[...] = r[:, di:].astype(BF16)


def _inproj(x2, shift, scale, w_in, S):
    T, D = x2.shape
    di = w_in.shape[1] // 2
    tm = min(512, S)
    tpb = S // tm
    return pl.pallas_call(
        _inproj_kernel,
        grid=(T // tm,),
        in_specs=[pl.BlockSpec((tm, D), lambda i: (i, 0)),
                  pl.BlockSpec((1, 1, D), lambda i: (i // tpb, 0, 0)),
                  pl.BlockSpec((1, 1, D), lambda i: (i // tpb, 0, 0)),
                  pl.BlockSpec((D, 2 * di), lambda i: (0, 0))],
        out_specs=[pl.BlockSpec((tm, di), lambda i: (i, 0)),
                   pl.BlockSpec((tm, di), lambda i: (i, 0))],
        out_shape=[jax.ShapeDtypeStruct((T, di), BF16)] * 2,
        compiler_params=_params(("parallel",)),
        name="mlstm_inproj",
    )(x2, shift, scale, w_in)


def _qkv_kernel(xm_ref, halo_ref, cw_ref, cb_ref, wq_ref, wk_ref, wv_ref, wif_ref, bif_ref,
                q_ref, k_ref, v_ref, xc_ref, g_ref, *, tpb, nh, kscale):
    tm, di = xm_ref.shape
    dh = di // nh
    i = pl.program_id(0)
    xm = xm_ref[...]
    xf = xm.astype(F32)
    halo = halo_ref[...].astype(F32)
    halo = jnp.where(i % tpb == 0, 0.0, halo)
    ext = jnp.concatenate([halo, xf], axis=0)
    acc = cb_ref[...] + cw_ref[CONV_WIDTH - 1:CONV_WIDTH, :] * xf
    for s in range(1, CONV_WIDTH):
        acc = acc + cw_ref[CONV_WIDTH - 1 - s:CONV_WIDTH - s, :] * ext[CONV_HALO - s:CONV_HALO - s + tm, :]
    xc = acc * _sigmoid(acc)
    xcb = xc.astype(BF16)
    xc_ref[...] = xcb
    for h in range(nh):
        sl = slice(h * dh, (h + 1) * dh)
        q_ref[:, sl] = jnp.dot(xcb[:, sl], wq_ref[h], preferred_element_type=F32).astype(BF16)
        k_ref[:, sl] = (jnp.dot(xcb[:, sl], wk_ref[h], preferred_element_type=F32) * kscale).astype(BF16)
        v_ref[:, sl] = jnp.dot(xm[:, sl], wv_ref[h], preferred_element_type=F32).astype(BF16)
    g = jnp.dot(xcb, wif_ref[...], preferred_element_type=F32) + bif_ref[...]
    col = lax.broadcasted_iota(jnp.int32, g.shape, 1)
    logsig = jnp.minimum(g, 0.0) - jnp.log(1.0 + jnp.exp(-jnp.abs(g)))
    g = jnp.where(col >= nh, logsig, g)
    g_ref[...] = g[:, :2 * nh]


def _qkv(xm, conv_w, conv_b, wq, wk, wv, w_if, b_if, S):
    T, di = xm.shape
    nh = wq.shape[0]
    dh = di // nh
    tm = min(512, S)
    tpb = S // tm
    hb = tm // CONV_HALO
    wif = jnp.zeros((di, LANES), BF16).at[:, :2 * nh].set(w_if.astype(BF16))
    bif = jnp.zeros((1, LANES), F32).at[0, :2 * nh].set(b_if)
    kern = functools.partial(_qkv_kernel, tpb=tpb, nh=nh, kscale=float(dh) ** -0.5)
    full2 = lambda i: (0, 0)
    full3 = lambda i: (0, 0, 0)
    row = lambda i: (i, 0)
    return pl.pallas_call(
        kern,
        grid=(T // tm,),
        in_specs=[pl.BlockSpec((tm, di), row),
                  pl.BlockSpec((CONV_HALO, di), lambda i: (jnp.maximum(i * hb - 1, 0), 0)),
                  pl.BlockSpec((CONV_WIDTH, di), full2),
                  pl.BlockSpec((1, di), full2),
                  pl.BlockSpec((nh, dh, dh), full3),
                  pl.BlockSpec((nh, dh, dh), full3),
                  pl.BlockSpec((nh, dh, dh), full3),
                  pl.BlockSpec((di, LANES), full2),
                  pl.BlockSpec((1, LANES), full2)],
        out_specs=[pl.BlockSpec((tm, di), row)] * 4 + [pl.BlockSpec((tm, 2 * nh), row)],
        out_shape=[jax.ShapeDtypeStruct((T, di), BF16)] * 4 + [jax.ShapeDtypeStruct((T, 2 * nh), F32)],
        compiler_params=_params(("parallel",)),
        name="mlstm_qkv",
    )(xm, xm, conv_w, conv_b.reshape(1, di), wq, wk, wv, wif, bif)


def _mlstm_kernel(q_ref, k_ref, v_ref, gc_ref, gr_ref, o_ref, c_scr, n_scr, m_scr):
    L = q_ref.shape[0]
    nh = c_scr.shape[0]
    dh = c_scr.shape[1]

    @pl.when(pl.program_id(1) == 0)
    def _():
        c_scr[...] = jnp.zeros_like(c_scr)
        n_scr[...] = jnp.zeros_like(n_scr)
        m_scr[...] = jnp.zeros_like(m_scr)

    t_idx = lax.broadcasted_iota(jnp.int32, (L, L), 0)
    s_idx = lax.broadcasted_iota(jnp.int32, (L, L), 1)
    causal = s_idx <= t_idx
    for h in range(nh):
        hs = slice(h * dh, (h + 1) * dh)
        _mlstm_head(q_ref[:, hs], k_ref[:, hs], v_ref[:, hs], gc_ref[0, h], gr_ref[0, h],
                    o_ref.at[:, hs], c_scr.at[h], n_scr.at[h], m_scr.at[h], causal, t_idx, s_idx)


def _mlstm_head(q, k, v, gc, gr, o_ref, c_scr, n_scr, m_scr, causal, t_idx, s_idx):
    i_col, f_col = gc[:, 0:1], gc[:, 1:2]
    i_row, f_row = gr[0:1, :], gr[1:2, :]
    b_col = jnp.sum(jnp.where(causal, f_row, 0.0), axis=1, keepdims=True)
    b_row = jnp.sum(jnp.where(t_idx <= s_idx, f_col, 0.0), axis=0, keepdims=True)
    m_prev = m_scr[...]
    d = jnp.where(causal, b_col - b_row + i_row, -jnp.inf)
    a_col = b_col + m_prev
    m_t = jnp.maximum(a_col, jnp.max(d, axis=1, keepdims=True))
    w_intra = jnp.exp(d - m_t)
    w_inter = jnp.exp(a_col - m_t)
    qk = lax.dot_general(q, k, (((1,), (1,)), ((), ())), preferred_element_type=F32)
    s_mat = qk * w_intra
    c_b = c_scr[...].astype(BF16)
    inter = lax.dot_general(q, c_b, (((1,), (1,)), ((), ())), preferred_element_type=F32)
    num = jnp.dot(s_mat.astype(BF16), v, preferred_element_type=F32) + w_inter * inter
    qn = jnp.sum(q.astype(F32) * n_scr[...], axis=1, keepdims=True)
    den = jnp.sum(s_mat, axis=1, keepdims=True) + w_inter * qn
    hcap = num / jnp.maximum(jnp.abs(den), jnp.exp(-m_t))
    mu = jnp.mean(hcap, axis=1, keepdims=True)
    dv = hcap - mu
    var = jnp.mean(dv * dv, axis=1, keepdims=True)
    o_ref[...] = (dv * lax.rsqrt(var + LN_EPS)).astype(BF16)

    b_last = jnp.sum(f_row, axis=1, keepdims=True)
    ws_col = b_last - b_col + i_col
    m_new = jnp.maximum(b_last + m_prev, jnp.max(ws_col, axis=0, keepdims=True))
    decay = jnp.exp(b_last + m_prev - m_new)
    ws = jnp.exp(ws_col - m_new)
    vw = (v.astype(F32) * ws).astype(BF16)
    upd = lax.dot_general(vw, k, (((0,), (0,)), ((), ())), preferred_element_type=F32)
    c_scr[...] = decay * c_scr[...] + upd
    n_scr[...] = decay * n_scr[...] + jnp.sum(k.astype(F32) * ws, axis=0, keepdims=True)
    m_scr[...] = m_new


def _mlstm(q, k, v, gcol, grow, B, S):
    T, di = q.shape
    nh = gcol.shape[1]
    dh = di // nh
    L = MLSTM_CHUNK
    nc = S // L
    blk = pl.BlockSpec((L, di), lambda b, c: (b * nc + c, 0))
    return pl.pallas_call(
        _mlstm_kernel,
        grid=(B, nc),
        in_specs=[blk, blk, blk,
                  pl.BlockSpec((1, nh, L, 2), lambda b, c: (b, 0, c, 0)),
                  pl.BlockSpec((1, nh, 2, L), lambda b, c: (b, 0, 0, c))],
        out_specs=blk,
        out_shape=jax.ShapeDtypeStruct((T, di), BF16),
        scratch_shapes=[pltpu.VMEM((nh, dh, dh), F32), pltpu.VMEM((nh, 1, dh), F32), pltpu.VMEM((nh, 1, 1), F32)],
        compiler_params=_params(("parallel", "arbitrary")),
        name="mlstm_scan",
    )(q, k, v, gcol, grow)


def _mlstm_out_kernel(hn_ref, xc_ref, z_ref, x_ref, gn_ref, skip_ref, w_ref, gate_ref, lg_ref, lb_ref, o_ref):
    z = z_ref[...].astype(F32)
    u = (hn_ref[...].astype(F32) * gn_ref[...] + skip_ref[...] * xc_ref[...].astype(F32)) * (z * _sigmoid(z))
    y = jnp.dot(u.astype(BF16), w_ref[...], preferred_element_type=F32)
    r = DEEPNORM_ALPHA * x_ref[...] + gate_ref[0] * y
    o_ref[...] = _layer_norm(r, lg_ref[...], lb_ref[...])


def _mlstm_out(hn, xc, z, x2, gn_g, skip, w_out, gate, ln_g, ln_b, S):
    T, D = x2.shape
    di = hn.shape[1]
    tm = min(512, S)
    tpb = S // tm
    row = lambda i: (i, 0)
    full2 = lambda i: (0, 0)
    return pl.pallas_call(
        _mlstm_out_kernel,
        grid=(T // tm,),
        in_specs=[pl.BlockSpec((tm, di), row), pl.BlockSpec((tm, di), row), pl.BlockSpec((tm, di), row),
                  pl.BlockSpec((tm, D), row),
                  pl.BlockSpec((1, di), full2), pl.BlockSpec((1, di), full2),
                  pl.BlockSpec((di, D), full2),
                  pl.BlockSpec((1, 1, D), lambda i: (i // tpb, 0, 0)),
                  pl.BlockSpec((1, D), full2), pl.BlockSpec((1, D), full2)],
        out_specs=pl.BlockSpec((tm, D), row),
        out_shape=jax.ShapeDtypeStruct((T, D), F32),
        compiler_params=_params(("parallel",)),
        name="mlstm_out_ln",
    )(hn, xc, z, x2, gn_g.reshape(1, di), skip.reshape(1, di), w_out, gate, ln_g.reshape(1, D), ln_b.reshape(1, D))


def _route_kernel(x_ref, sh_ref, sc_ref, wrt_ref, br_ref, tri_ref, out_ref, cnt_ref):
    E = wrt_ref.shape[0]
    epg = E // N_GROUPS
    h = x_ref[...] * (1.0 + sc_ref[0]) + sh_ref[0]
    lt = lax.dot_general(wrt_ref[...], h, (((1,), (1,)), ((), ())),
                         precision=lax.Precision.HIGHEST, preferred_element_type=F32)
    ex = jnp.exp(lt - jnp.max(lt, axis=0, keepdims=True))
    probs = ex / jnp.sum(ex, axis=0, keepdims=True)
    sel = probs + br_ref[...]
    srow = [sel[e:e + 1, :] for e in range(E)]
    prow = [probs[e:e + 1, :] for e in range(E)]
    gscore = []
    for g in range(N_GROUPS):
        r = srow[g * epg:(g + 1) * epg]
        best = None
        for a in range(epg):
            for b in range(a + 1, epg):
                pair = r[a] + r[b]
                best = pair if best is None else jnp.maximum(best, pair)
        gscore.append(best)
    gmax = functools.reduce(jnp.maximum, gscore)
    chosen, taken = [], None
    for g in range(N_GROUPS):
        c = gscore[g] == gmax
        if taken is not None:
            c = c & jnp.logical_not(taken)
        taken = c if taken is None else (taken | c)
        chosen.append(c)
    zero = jnp.zeros_like(srow[0])
    e1 = zero
    e2 = zero
    p1 = zero
    p2 = zero
    firsts, seconds = [], []
    for e in range(E):
        g = e // epg
        rank = zero
        for o in range(g * epg, (g + 1) * epg):
            if o == e:
                continue
            beats = (srow[o] > srow[e]) | ((srow[o] == srow[e]) & (o < e))
            rank = rank + jnp.where(beats, 1.0, 0.0)
        is1 = chosen[g] & (rank == 0.0)
        is2 = chosen[g] & (rank == 1.0)
        firsts.append(is1)
        seconds.append(is2)
        e1 = e1 + jnp.where(is1, float(e), 0.0)
        e2 = e2 + jnp.where(is2, float(e), 0.0)
        p1 = p1 + jnp.where(is1, prow[e], 0.0)
        p2 = p2 + jnp.where(is2, prow[e], 0.0)
    mask = jnp.concatenate([jnp.where(firsts[e] | seconds[e], 1.0, 0.0) for e in range(E)], axis=0)
    prefix = jnp.dot(mask.astype(BF16), tri_ref[...], preferred_element_type=F32)
    counts = jnp.sum(mask, axis=1, keepdims=True)
    run_start = jnp.zeros((1, 1), F32)
    s1 = zero
    s2 = zero
    for e in range(E):
        pe = prefix[e:e + 1, :] + run_start
        s1 = s1 + jnp.where(firsts[e], pe, 0.0)
        s2 = s2 + jnp.where(seconds[e], pe, 0.0)
        run_start = run_start + jnp.floor((counts[e:e + 1, :] + (RUN_ALIGN - 1)) * (1.0 / RUN_ALIGN)) * RUN_ALIGN
    psum = p1 + p2
    out_ref[...] = jnp.concatenate([e1, e2, p1 / psum, p2 / psum, s1, s2, zero, zero], axis=0)
    cnt_ref[...] = jnp.broadcast_to(counts, cnt_ref.shape)


def _route(x2, shift, scale, w_router, b_router, S):
    T, D = x2.shape
    E = w_router.shape[1]
    tm = min(MOE_TILE, S)
    tpb = S // tm
    tri = (jnp.arange(tm)[:, None] < jnp.arange(tm)[None, :]).astype(BF16)
    return pl.pallas_call(
        _route_kernel,
        grid=(T // tm,),
        in_specs=[pl.BlockSpec((tm, D), lambda i: (i, 0)),
                  pl.BlockSpec((1, 1, D), lambda i: (i // tpb, 0, 0)),
                  pl.BlockSpec((1, 1, D), lambda i: (i // tpb, 0, 0)),
                  pl.BlockSpec((E, D), lambda i: (0, 0)),
                  pl.BlockSpec((E, 1), lambda i: (0, 0)),
                  pl.BlockSpec((tm, tm), lambda i: (0, 0))],
        out_specs=[pl.BlockSpec((8, tm), lambda i: (0, i)),
                   pl.BlockSpec((E, LANES), lambda i: (0, i))],
        out_shape=[jax.ShapeDtypeStruct((8, T), F32), jax.ShapeDtypeStruct((E, (T // tm) * LANES), F32)],
        compiler_params=_params(("parallel",)),
        name="moe_route",
    )(x2, shift, scale, w_router.T, b_router.reshape(E, 1), tri)


def _stage_rows(tm, n_exp):
    return MOE_TOPK * tm + n_exp * RUN_ALIGN


def _chunk_copies(tc_ref, ts_ref, ds_ref, step, n_exp, make):
    out = []
    for e in range(n_exp):
        cnt = tc_ref[step * n_exp + e]
        off = ts_ref[step * n_exp + e]
        dst = ds_ref[step * n_exp + e]
        for k in reversed(range(RUN_ALIGN.bit_length() - 1, CHUNK_BITS)):
            done = (cnt >> (k + 1)) << (k + 1)
            out.append(((cnt & (1 << k)) != 0,
                        make(pl.multiple_of(off + done, RUN_ALIGN), pl.multiple_of(dst + done, RUN_ALIGN), 1 << k)))
    return out


def _dispatch_kernel(tc_ref, ts_ref, ds_ref, pe_ref, x_ref, sh_ref, sc_ref, pos_ref, xs_ref,
                     stage, zbuf, sem, zsem):
    tm = x_ref.shape[0]
    rb = zbuf.shape[0]
    i = pl.program_id(0)
    slot = i % 2

    n_exp = pe_ref.shape[0] - 1
    n_blocks = xs_ref.shape[0] // rb

    def pad_copy(e):
        return pltpu.make_async_copy(zbuf, xs_ref.at[pl.ds(pl.multiple_of(pe_ref[e + 1] - rb, rb), rb)], zsem)

    def tail_copy(j):
        return pltpu.make_async_copy(zbuf, xs_ref.at[pl.ds(pl.multiple_of(pe_ref[n_exp] + j * rb, rb), rb)], zsem)

    def tail_live(j):
        return pe_ref[n_exp] // rb + j < n_blocks

    @pl.when(i == 0)
    def _():
        zbuf[...] = jnp.zeros_like(zbuf)
        for e in range(n_exp):
            @pl.when(pe_ref[e + 1] > pe_ref[e])
            def _():
                pad_copy(e).start()
        for j in range(n_exp):
            @pl.when(tail_live(j))
            def _():
                tail_copy(j).start()
        for e in range(n_exp):
            @pl.when(pe_ref[e + 1] > pe_ref[e])
            def _():
                pad_copy(e).wait()
        for j in range(n_exp):
            @pl.when(tail_live(j))
            def _():
                tail_copy(j).wait()

    h = (x_ref[...] * (1.0 + sc_ref[0]) + sh_ref[0]).astype(BF16)
    pos = pos_ref[0]
    slot_id = lax.broadcasted_iota(jnp.int32, (stage.shape[1], tm), 0)
    perm = jnp.where((slot_id == pos[0:1, :]) | (slot_id == pos[1:2, :]), 1.0, 0.0).astype(BF16)
    stage[slot] = jnp.dot(perm, h, preferred_element_type=F32)

    def copies(step, s):
        def make(src_row, dst_row, size):
            return pltpu.make_async_copy(stage.at[s, pl.ds(src_row, size)], xs_ref.at[pl.ds(dst_row, size)], sem.at[s])
        return _chunk_copies(tc_ref, ts_ref, ds_ref, step, n_exp, make)

    for live, cp in copies(i, slot):
        pl.when(live)(cp.start)
    for live, cp in copies(jnp.maximum(i - 1, 0), 1 - slot):
        pl.when(live & (i > 0))(cp.wait)
    for live, cp in copies(i, slot):
        pl.when(live & (i == pl.num_programs(0) - 1))(cp.wait)


def _dispatch(x2, shift, scale, posrow, tcnt, tstart, dstart, pends0, cap, S):
    T, D = x2.shape
    tm = min(MOE_TILE, S)
    tpb = S // tm
    mod = pl.BlockSpec((1, 1, D), lambda i, *_: (i // tpb, 0, 0))
    grid_spec = pltpu.PrefetchScalarGridSpec(
        num_scalar_prefetch=4,
        grid=(T // tm,),
        in_specs=[pl.BlockSpec((tm, D), lambda i, *_: (i, 0)), mod, mod,
                  pl.BlockSpec((1, 2, tm), lambda i, *_: (i, 0, 0))],
        out_specs=pl.BlockSpec(memory_space=pl.ANY),
        scratch_shapes=[pltpu.VMEM((2, _stage_rows(tm, pends0.shape[0] - 1), D), F32),
                        pltpu.VMEM((MOE_ROW_BLOCK, D), F32),
                        pltpu.SemaphoreType.DMA((2,)), pltpu.SemaphoreType.DMA(())],
    )
    return pl.pallas_call(
        _dispatch_kernel,
        grid_spec=grid_spec,
        out_shape=jax.ShapeDtypeStruct((cap, D), F32),
        compiler_params=_params(("arbitrary",)),
        name="moe_dispatch",
    )(tcnt, tstart, dstart, pends0, x2, shift, scale, posrow)


def _expert_kernel(be_ref, na_ref, xs_ref, wg_ref, wu_ref, wd_ref, ys_ref, wg_b, wu_b, wd_b):
    i = pl.program_id(0)
    active = i < na_ref[0]

    @pl.when(active & ((i == 0) | (be_ref[i] != be_ref[jnp.maximum(i - 1, 0)])))
    def _():
        wg_b[...] = wg_ref[0, 0].astype(BF16)
        wu_b[...] = wu_ref[0, 0].astype(BF16)
        wd_b[...] = wd_ref[0, 0].astype(BF16)

    @pl.when(active)
    def _():
        xb = xs_ref[...].astype(BF16)
        g = jnp.dot(xb, wg_b[...], preferred_element_type=F32)
        u = jnp.dot(xb, wu_b[...], preferred_element_type=F32)
        a = (g * _sigmoid(g)) * u
        ys_ref[...] = jnp.dot(a.astype(BF16), wd_b[...], preferred_element_type=F32)

    @pl.when(jnp.logical_not(active))
    def _():
        ys_ref[...] = jnp.zeros_like(ys_ref)


def _experts(xs, block_expert, n_active, w_gate, w_up, w_down, layer):
    cap, D = xs.shape
    _, E, _, F = w_gate.shape
    rb = MOE_ROW_BLOCK
    nb = cap // rb

    def blk(i, be, na):
        return jnp.maximum(jnp.minimum(i, na[0] - 1), 0)

    grid_spec = pltpu.PrefetchScalarGridSpec(
        num_scalar_prefetch=2,
        grid=(nb,),
        in_specs=[pl.BlockSpec((rb, D), lambda i, be, na: (blk(i, be, na), 0)),
                  pl.BlockSpec((1, 1, D, F), lambda i, be, na: (layer, be[blk(i, be, na)], 0, 0)),
                  pl.BlockSpec((1, 1, D, F), lambda i, be, na: (layer, be[blk(i, be, na)], 0, 0)),
                  pl.BlockSpec((1, 1, F, D), lambda i, be, na: (layer, be[blk(i, be, na)], 0, 0))],
        out_specs=pl.BlockSpec((rb, D), lambda i, be, na: (i, 0)),
        scratch_shapes=[pltpu.VMEM((D, F), BF16), pltpu.VMEM((D, F), BF16), pltpu.VMEM((F, D), BF16)],
    )
    return pl.pallas_call(
        _expert_kernel,
        grid_spec=grid_spec,
        out_shape=jax.ShapeDtypeStruct((cap, D), F32),
        compiler_params=_params(("arbitrary",)),
        name="moe_experts",
    )(block_expert, n_active, xs, w_gate, w_up, w_down)


def _combine_kernel(tc_ref, ts_ref, ds_ref, ys_ref, rt_ref, x_ref, gate_ref, lg_ref, lb_ref, o_ref, stage, sem,
                    *, n_exp):
    tm = x_ref.shape[0]
    i = pl.program_id(0)
    slot = i % 2

    def copies(step, s):
        def make(stage_row, ys_row, size):
            return pltpu.make_async_copy(ys_ref.at[pl.ds(ys_row, size)], stage.at[s, pl.ds(stage_row, size)], sem.at[s])
        return _chunk_copies(tc_ref, ts_ref, ds_ref, step, n_exp, make)

    @pl.when(i == 0)
    def _():
        stage[...] = jnp.zeros_like(stage)

    for live, cp in copies(i, slot):
        pl.when(live & (i == 0))(cp.start)
    nxt = jnp.minimum(i + 1, pl.num_programs(0) - 1)
    for live, cp in copies(nxt, 1 - slot):
        pl.when(live & (i + 1 < pl.num_programs(0)))(cp.start)
    for live, cp in copies(i, slot):
        pl.when(live)(cp.wait)

    rt = rt_ref[...]
    ysb = stage[slot].astype(BF16)
    slot_id = lax.broadcasted_iota(jnp.int32, (tm, stage.shape[1]), 1).astype(F32)
    y = None
    for j in range(MOE_TOPK):
        perm = jnp.where(slot_id == rt[:, 4 + j:5 + j], 1.0, 0.0).astype(BF16)
        part = rt[:, 2 + j:3 + j] * jnp.dot(perm, ysb, preferred_element_type=F32)
        y = part if y is None else y + part
    r = DEEPNORM_ALPHA * x_ref[...] + gate_ref[0] * y
    o_ref[...] = _layer_norm(r, lg_ref[...], lb_ref[...])


def _combine(ys, route_t, x2, gate, ln_g, ln_b, tcnt, tstart, dstart, S):
    T, D = x2.shape
    tm = min(MOE_TILE, S)
    tpb = S // tm
    grid_spec = pltpu.PrefetchScalarGridSpec(
        num_scalar_prefetch=3,
        grid=(T // tm,),
        in_specs=[pl.BlockSpec(memory_space=pl.ANY),
                  pl.BlockSpec((tm, 8), lambda i, *_: (i, 0)),
                  pl.BlockSpec((tm, D), lambda i, *_: (i, 0)),
                  pl.BlockSpec((1, 1, D), lambda i, *_: (i // tpb, 0, 0)),
                  pl.BlockSpec((1, D), lambda i, *_: (0, 0)),
                  pl.BlockSpec((1, D), lambda i, *_: (0, 0))],
        out_specs=pl.BlockSpec((tm, D), lambda i, *_: (i, 0)),
        scratch_shapes=[pltpu.VMEM((2, _stage_rows(tm, tcnt.shape[0] // (T // tm)), D), F32),
                        pltpu.SemaphoreType.DMA((2,))],
    )
    return pl.pallas_call(
        functools.partial(_combine_kernel, n_exp=tcnt.shape[0] // (T // tm)),
        grid_spec=grid_spec,
        out_shape=jax.ShapeDtypeStruct((T, D), F32),
        compiler_params=_params(("arbitrary",)),
        name="moe_combine_ln",
    )(tcnt, tstart, dstart, ys, route_t, x2, gate, ln_g.reshape(1, D), ln_b.reshape(1, D))


def _moe_layer(x2, shift, scale, gate, ln_g, ln_b, w_router, b_router, w_gate, w_up, w_down, layer, S):
    T, D = x2.shape
    E = w_router.shape[1]
    rb = MOE_ROW_BLOCK
    tm = min(MOE_TILE, S)
    nt = T // tm
    assert tm < (1 << CHUNK_BITS)
    route, cnt = _route(x2, shift, scale, w_router, b_router, S)
    tcnt = cnt.reshape(E, nt, LANES)[:, :, 0].T.astype(jnp.int32)
    tcnt = (tcnt + RUN_ALIGN - 1) // RUN_ALIGN * RUN_ALIGN
    base = jnp.cumsum(tcnt, axis=0) - tcnt
    counts = jnp.sum(tcnt, axis=0)
    padded = (counts + rb - 1) // rb * rb
    pends = jnp.cumsum(padded)
    pstarts = pends - padded
    tstart = jnp.cumsum(tcnt, axis=1) - tcnt
    dstart = pstarts[None, :] + base
    posrow = jnp.transpose(route[4:6].astype(jnp.int32).reshape(2, nt, tm), (1, 0, 2))
    nb = -(-(T * MOE_TOPK + nt * E * (RUN_ALIGN - 1)) // rb) + E
    cap = nb * rb
    block_start = jnp.arange(nb, dtype=jnp.int32) * rb
    block_expert = jnp.minimum(jnp.sum(block_start[:, None] >= pends[None, :], axis=1), E - 1).astype(jnp.int32)
    n_active = (pends[-1:] // rb).astype(jnp.int32)
    pends0 = jnp.concatenate([jnp.zeros((1,), jnp.int32), pends.astype(jnp.int32)])
    tables = [t.reshape(-1).astype(jnp.int32) for t in (tcnt, tstart, dstart)]
    xs = _dispatch(x2, shift, scale, posrow, *tables, pends0, cap, S)
    ys = _experts(xs, block_expert, n_active, w_gate, w_up, w_down, layer)
    return _combine(ys, route.T, x2, gate, ln_g, ln_b, *tables, S)


def _moba_proj_kernel(x_ref, shq_ref, scq_ref, shkv_ref, sckv_ref, wq_ref, wkv_ref,
                      q_ref, k_ref, vt_ref, km_ref, *, qscale, tpb, slopes):
    tm, D = x_ref.shape
    hd = HEAD_DIM
    nh = D // hd
    x = x_ref[...]
    hq = x * (1.0 + scq_ref[0]) + shq_ref[0]
    hkv = x * (1.0 + sckv_ref[0]) + shkv_ref[0]
    qf = jnp.dot(hq.astype(BF16), wq_ref[...], preferred_element_type=F32) * qscale
    kv = jnp.dot(hkv.astype(BF16), wkv_ref[...], preferred_element_type=F32)
    kf = kv[:, :D]
    vt = kv[:, D:].T
    ones_pad = jnp.where(lax.broadcasted_iota(jnp.int32, (V_ROWS - hd, tm), 0) == 0, 1.0, 0.0)
    for g in range(nh):
        vt_ref[0, g * V_ROWS:(g + 1) * V_ROWS, :] = jnp.concatenate(
            [vt[g * hd:(g + 1) * hd, :], ones_pad], axis=0).astype(BF16)
    lane = lax.broadcasted_iota(jnp.int32, (1, LANES), 1)
    head_lanes = lane >= hd
    pos = (pl.program_id(0) % tpb) * tm + lax.broadcasted_iota(jnp.int32, (tm, 1), 0)
    pos_f = pos.astype(F32)
    onehot = jnp.where(lane == pos // MOBA_BLOCK, 1.0, 0.0)
    q_fill = jnp.where((lane >= BIAS_LANE) & (lane < BIAS_LANE + 3), 1.0, 0.0)
    nbt = tm // MOBA_BLOCK
    means = [jnp.mean(kf[j * MOBA_BLOCK:(j + 1) * MOBA_BLOCK, :], axis=0, keepdims=True) for j in range(nbt)]
    km = jnp.concatenate(means + [jnp.zeros((8 - nbt, D), F32)], axis=0)
    for j in range(nh // 2):
        cs = slice(j * LANES, (j + 1) * LANES)
        qb, kb, mb = qf[:, cs], kf[:, cs], km[:, cs]
        q_pair = (pltpu.roll(qb, hd, axis=1), qb)
        k_pair = (pltpu.roll(kb, hd, axis=1), kb)
        m_pair = (pltpu.roll(mb, hd, axis=1), mb)
        for e in range(2):
            g = 2 * j + e
            gs = slice(g * LANES, (g + 1) * LANES)
            q_ref[:, gs] = jnp.where(head_lanes, q_pair[e], q_fill).astype(BF16)
            c = (slopes[g] * LOG2E) * pos_f
            hi = c.astype(BF16).astype(F32)
            mid = (c - hi).astype(BF16).astype(F32)
            lo = c - hi - mid
            aug = jnp.where(lane == BIAS_LANE, hi,
                            jnp.where(lane == BIAS_LANE + 1, mid,
                                      jnp.where(lane == BIAS_LANE + 2, lo, onehot)))
            k_ref[:, gs] = jnp.where(head_lanes, k_pair[e], aug).astype(BF16)
            mg = jnp.where(head_lanes, m_pair[e], 0.0)
            for jb in range(nbt):
                km_ref[jb, :, gs] = mg[jb:jb + 1, :]


def _moba_proj(x2, shq, scq, shkv, sckv, wq, wkv, B, S):
    T, D = x2.shape
    nh = D // HEAD_DIM
    tm = min(512, S)
    tpb = S // tm
    nbt = tm // MOBA_BLOCK
    assert S // MOBA_BLOCK <= BIAS_LANE and nbt <= 8
    slopes = tuple(2.0 ** (-8.0 * (h + 1.0) / nh) for h in range(nh))
    mod = pl.BlockSpec((1, 1, D), lambda i: (i // tpb, 0, 0))
    kern = functools.partial(_moba_proj_kernel, qscale=float(HEAD_DIM) ** -0.5 * LOG2E, tpb=tpb, slopes=slopes)
    return pl.pallas_call(
        kern,
        grid=(T // tm,),
        in_specs=[pl.BlockSpec((tm, D), lambda i: (i, 0)), mod, mod, mod, mod,
                  pl.BlockSpec((D, D), lambda i: (0, 0)),
                  pl.BlockSpec((D, 2 * D), lambda i: (0, 0))],
        out_specs=[pl.BlockSpec((tm, nh * LANES), lambda i: (i, 0)),
                   pl.BlockSpec((tm, nh * LANES), lambda i: (i, 0)),
                   pl.BlockSpec((1, nh * V_ROWS, tm), lambda i: (i // tpb, 0, i % tpb)),
                   pl.BlockSpec((nbt, 1, nh * LANES), lambda i: (i, 0, 0))],
        out_shape=[jax.ShapeDtypeStruct((T, nh * LANES), BF16), jax.ShapeDtypeStruct((T, nh * LANES), BF16),
                   jax.ShapeDtypeStruct((B, nh * V_ROWS, S), BF16),
                   jax.ShapeDtypeStruct((T // MOBA_BLOCK, 1, nh * LANES), F32)],
        compiler_params=_params(("parallel",)),
        name="moba_proj",
    )(x2, shq, scq, shkv, sckv, wq, wkv)


def _moba_select_kernel(q_ref, km_ref, qat_ref):
    tq = q_ref.shape[0]
    G = q_ref.shape[1] // LANES
    nblk = km_ref.shape[1]
    blk_id = lax.broadcasted_iota(jnp.int32, (nblk, tq), 0)
    own = (pl.program_id(2) * tq + lax.broadcasted_iota(jnp.int32, (1, tq), 1)) // MOBA_BLOCK
    for g in range(G):
        gs = slice(g * LANES, (g + 1) * LANES)
        qgt = q_ref[:, gs].astype(F32).T
        gate = jnp.dot(km_ref[0, :, gs], qgt, precision=lax.Precision.HIGHEST,
                       preferred_element_type=F32)
        gsc = jnp.where(blk_id < own, gate, -jnp.inf)
        picked = blk_id < 0
        for _ in range(MOBA_TOPK):
            mx = jnp.max(gsc, axis=0, keepdims=True)
            first = jnp.min(jnp.where(gsc == mx, blk_id, nblk), axis=0, keepdims=True)
            pick = blk_id == first
            picked = picked | pick
            gsc = jnp.where(pick, -jnp.inf, gsc)
        visible = (picked & (blk_id < own)) | (blk_id == own)
        pen = jnp.where(visible, 0.0, NEG)
        qat_ref[0, g] = (qgt + jnp.concatenate([pen, jnp.zeros((LANES - nblk, tq), F32)], axis=0)).astype(BF16)


def _moba_select(q, km, B, S):
    nh = q.shape[1] // LANES
    G = MOBA_HEADS_PER_STEP
    tq = min(512, S)
    nblk = S // MOBA_BLOCK
    return pl.pallas_call(
        _moba_select_kernel,
        grid=(B, nh // G, S // tq),
        in_specs=[pl.BlockSpec((tq, G * LANES), lambda b, hp, i: (b * (S // tq) + i, hp)),
                  pl.BlockSpec((1, nblk, G * LANES), lambda b, hp, i: (b, 0, hp))],
        out_specs=pl.BlockSpec((1, G, LANES, tq), lambda b, hp, i: (b, hp, 0, i)),
        out_shape=jax.ShapeDtypeStruct((B, nh, LANES, S), BF16),
        compiler_params=_params(("parallel", "parallel", "parallel")),
        name="moba_select",
    )(q, km)


def _moba_kernel(qat_ref, k_ref, vt_ref, o_ref, st_scr, m_scr, acc_scr):
    G = qat_ref.shape[1]
    BLK = qat_ref.shape[3]
    hd = HEAD_DIM
    nhalf = BLK // LANES
    own = pl.program_id(2)
    k_off = lax.broadcasted_iota(jnp.int32, (BLK, 1), 0)
    q_off = lax.broadcasted_iota(jnp.int32, (1, LANES), 1)

    m_scr[...] = jnp.full(m_scr.shape, NEG, F32)
    acc_scr[...] = jnp.zeros_like(acc_scr)

    def scores(n, slot):
        start = pl.multiple_of(n * BLK, BLK)
        for g in range(G):
            kn = k_ref[pl.ds(start, BLK), g * LANES:(g + 1) * LANES]
            st_scr[slot, g] = jnp.dot(kn, qat_ref[0, g], preferred_element_type=F32)

    def update(n, slot, causal):
        start = pl.multiple_of(n * BLK, BLK)
        for g in range(G):
            alphas, ps = [], []
            for hf in range(nhalf):
                c = g * nhalf + hf
                ls = slice(hf * LANES, (hf + 1) * LANES)
                st = st_scr[slot, g, :, ls]
                if causal:
                    st = jnp.where(k_off <= q_off + hf * LANES, st, NEG)
                m = m_scr[c]
                m_new = jnp.maximum(m, jnp.max(st, axis=0, keepdims=True))
                alphas.append(jnp.exp2(m - m_new))
                ps.append(jnp.exp2(st - m_new).astype(BF16))
                m_scr[c] = m_new
            vtn = vt_ref[0, g * V_ROWS:(g + 1) * V_ROWS, pl.ds(start, BLK)]
            pv = jnp.dot(vtn, jnp.concatenate(ps, axis=1), preferred_element_type=F32)
            acc_scr[g] = jnp.concatenate(alphas, axis=1) * acc_scr[g] + pv

    scores(0, 0)

    def step(i, carry):
        n = 2 * i
        scores(n + 1, 1)
        update(n, 0, False)
        scores(n + 2, 0)
        update(n + 1, 1, False)
        return carry

    lax.fori_loop(0, own // 2, step, 0)

    @pl.when(own % 2 == 0)
    def _():
        update(own, 0, True)

    @pl.when(own % 2 == 1)
    def _():
        scores(own, 1)
        update(own - 1, 0, False)
        update(own, 1, True)
    for g in range(G):
        acc = acc_scr[g]
        o_ref[0, g * hd:(g + 1) * hd, :] = (acc[:hd, :] * (1.0 / acc[hd:hd + 1, :])).astype(BF16)


def _moba_attn(qat, k, vt, B, S):
    nh = qat.shape[1]
    G = MOBA_HEADS_PER_STEP
    BLK = MOBA_BLOCK
    nblk = S // BLK
    hd = HEAD_DIM
    return pl.pallas_call(
        _moba_kernel,
        grid=(B, nh // G, nblk),
        in_specs=[pl.BlockSpec((1, G, LANES, BLK), lambda b, hp, qi: (b, hp, 0, qi)),
                  pl.BlockSpec((S, G * LANES), lambda b, hp, qi: (b, hp)),
                  pl.BlockSpec((1, G * V_ROWS, S), lambda b, hp, qi: (b, hp, 0))],
        out_specs=pl.BlockSpec((1, G * hd, BLK), lambda b, hp, qi: (b, hp, qi)),
        out_shape=jax.ShapeDtypeStruct((B, nh * hd, S), BF16),
        scratch_shapes=[pltpu.VMEM((2, G, BLK, BLK), F32),
                        pltpu.VMEM((G * (BLK // LANES), 1, LANES), F32),
                        pltpu.VMEM((G, V_ROWS, BLK), F32)],
        compiler_params=_params(("parallel", "parallel", "arbitrary")),
        name="moba_attn",
    )(qat, k, vt)


def _attn_out_kernel(a_ref, x_ref, w_ref, gate_ref, lg_ref, lb_ref, o_ref):
    y = lax.dot_general(a_ref[0], w_ref[...], (((0,), (0,)), ((), ())), preferred_element_type=F32)
    r = DEEPNORM_ALPHA * x_ref[...] + gate_ref[0] * y
    o_ref[...] = _layer_norm(r, lg_ref[...], lb_ref[...])


def _attn_out(at, x2, w, gate, ln_g, ln_b, S):
    T, D = x2.shape
    K = at.shape[1]
    tm = min(512, S)
    tpb = S // tm
    row = lambda i: (i, 0)
    full2 = lambda i: (0, 0)
    return pl.pallas_call(
        _attn_out_kernel,
        grid=(T // tm,),
        in_specs=[pl.BlockSpec((1, K, tm), lambda i: (i // tpb, 0, i % tpb)), pl.BlockSpec((tm, D), row),
                  pl.BlockSpec((K, D), full2),
                  pl.BlockSpec((1, 1, D), lambda i: (i // tpb, 0, 0)),
                  pl.BlockSpec((1, D), full2), pl.BlockSpec((1, D), full2)],
        out_specs=pl.BlockSpec((tm, D), row),
        out_shape=jax.ShapeDtypeStruct((T, D), F32),
        compiler_params=_params(("parallel",)),
        name="attn_out_ln",
    )(at, x2, w, gate, ln_g.reshape(1, D), ln_b.reshape(1, D))


def kernel(x, c, w_ada, b_ada, ln_g, ln_b, a_w_in, a_conv_w, a_conv_b, a_wq, a_wk, a_wv, a_w_if, a_b_if,
           a_gn_g, a_skip, a_w_out, b_w_kv, b_wq, b_wo, moe_w_router, moe_b_router, moe_w_gate, moe_w_up,
           moe_w_down):
    B, S, D = x.shape
    T = B * S
    assert DEPTH == 2 and S % MOBA_BLOCK == 0 and S % MLSTM_CHUNK == 0
    n_layer_mod = DEPTH * N_MOD_PER_LAYER * D
    cond = _ada_cond(c, w_ada, b_ada)
    mods = cond[:, :n_layer_mod].reshape(B, DEPTH, 2, 3, 1, D)
    kv_mod = cond[:, n_layer_mod:].reshape(B, 2, 1, D)

    def mod3(layer, sub):
        m = mods[:, layer, sub]
        return m[:, 0], m[:, 1], 1.0 + m[:, 2]

    xf = x.reshape(T, D)

    shift, scale, gate = mod3(0, 0)
    nh = MLSTM_HEADS
    xm, z = _inproj(xf, shift, scale, a_w_in[0].astype(BF16), S)
    q, k, v, xc, gts = _qkv(xm, a_conv_w[0], a_conv_b[0], a_wq[0].astype(BF16), a_wk[0].astype(BF16),
                            a_wv[0].astype(BF16), a_w_if[0], a_b_if[0], S)
    g4 = gts.reshape(B, S, 2, nh)
    gcol = jnp.transpose(g4, (0, 3, 1, 2))
    grow = jnp.transpose(g4, (0, 3, 2, 1))
    hn = _mlstm(q, k, v, gcol, grow, B, S)
    xf = _mlstm_out(hn, xc, z, xf, a_gn_g[0], a_skip[0], a_w_out[0].astype(BF16), gate, ln_g[0, 0], ln_b[0, 0], S)
    shift, scale, gate = mod3(0, 1)
    xf = _moe_layer(xf, shift, scale, gate, ln_g[0, 1], ln_b[0, 1], moe_w_router, moe_b_router,
                    moe_w_gate, moe_w_up, moe_w_down, 0, S)

    shift, scale, gate = mod3(1, 0)
    q, k, vt, km = _moba_proj(xf, shift, scale, kv_mod[:, 0], kv_mod[:, 1], b_wq[0].astype(BF16),
                              b_w_kv.astype(BF16), B, S)
    km = km.reshape(B, S // MOBA_BLOCK, km.shape[-1])
    attn_t = _moba_attn(_moba_select(q, km, B, S), k, vt, B, S)
    xf = _attn_out(attn_t, xf, b_wo[0].astype(BF16), gate, ln_g[1, 0], ln_b[1, 0], S)
    shift, scale, gate = mod3(1, 1)
    xf = _moe_layer(xf, shift, scale, gate, ln_g[1, 1], ln_b[1, 1], moe_w_router, moe_b_router,
                    moe_w_gate, moe_w_up, moe_w_down, 1, S)
    return xf.reshape(B, S, D)
```

```python
import functools

import jax
import jax.numpy as jnp
from jax import lax
from jax.experimental import pallas as pl
from jax.experimental.pallas import tpu as pltpu

DEPTH = 2
MLSTM_HEADS = 4
CONV_WIDTH = 4
MLSTM_CHUNK = 512
ATTN_HEADS = 16
MOBA_BLOCK = 256
MOBA_TOPK = 3
N_EXPERTS = 16
N_GROUPS = 4
MOE_TOPK = 2
MOE_ROW_BLOCK = 256
MOE_TILE = 512
CHUNK_BITS = 10
RUN_ALIGN = 8
DEEPNORM_ALPHA = (2.0 * DEPTH) ** 0.25
LN_EPS = 1e-5
N_MOD_PER_LAYER = 6

HEAD_DIM = 64
V_ROWS = 80
MOBA_HEADS_PER_STEP = 4
BIAS_LANE = 32
LOG2E = 1.4426950408889634

LANES = 128
CONV_HALO = 16
NEG = -1e30
VMEM_LIMIT = 56 * 1024 * 1024

F32 = jnp.float32
BF16 = jnp.bfloat16


def _sigmoid(x):
    return 1.0 / (1.0 + jnp.exp(-x))


def _params(sem, vmem=VMEM_LIMIT):
    return pltpu.CompilerParams(dimension_semantics=sem, vmem_limit_bytes=vmem)


def _layer_norm(r, g, b):
    mu = jnp.mean(r, axis=-1, keepdims=True)
    d = r - mu
    var = jnp.mean(d * d, axis=-1, keepdims=True)
    return d * lax.rsqrt(var + LN_EPS) * g + b


def _ada_kernel(c_ref, w_ref, b_ref, o_ref):
    c = c_ref[...]
    s = c * _sigmoid(c)
    o_ref[...] = jnp.dot(s.astype(BF16), w_ref[...].astype(BF16),
                         preferred_element_type=F32) + b_ref[...]


def _ada_cond(c, w_ada, b_ada):
    B, D = c.shape
    N = w_ada.shape[1]
    tn = 2048 if N % 2048 == 0 else N
    cp = jnp.zeros((8, D), F32).at[:B].set(c)
    out = pl.pallas_call(
        _ada_kernel,
        grid=(N // tn,),
        in_specs=[pl.BlockSpec((8, D), lambda j: (0, 0)),
                  pl.BlockSpec((D, tn), lambda j: (0, j)),
                  pl.BlockSpec((1, tn), lambda j: (0, j))],
        out_specs=pl.BlockSpec((8, tn), lambda j: (0, j)),
        out_shape=jax.ShapeDtypeStruct((8, N), F32),
        compiler_params=_params(("arbitrary",)),
        name="ada_cond",
    )(cp, w_ada, b_ada.reshape(1, N))
    return out[:B]


def _inproj_kernel(x_ref, sh_ref, sc_ref, w_ref, xm_ref, z_ref):
    di = xm_ref.shape[-1]
    h = x_ref[...] * (1.0 + sc_ref[0]) + sh_ref[0]
    r = jnp.dot(h.astype(BF16), w_ref[...], preferred_element_type=F32)
    xm_ref[...] = r[:, :di].astype(BF16)
    z_ref[...] = r[:, di:].astype(BF16)


def _inproj(x2, shift, scale, w_in, S):
    T, D = x2.shape
    di = w_in.shape[1] // 2
    tm = min(512, S)
    tpb = S // tm
    return pl.pallas_call(
        _inproj_kernel,
        grid=(T // tm,),
        in_specs=[pl.BlockSpec((tm, D), lambda i: (i, 0)),
                  pl.BlockSpec((1, 1, D), lambda i: (i // tpb, 0, 0)),
                  pl.BlockSpec((1, 1, D), lambda i: (i // tpb, 0, 0)),
                  pl.BlockSpec((D, 2 * di), lambda i: (0, 0))],
        out_specs=[pl.BlockSpec((tm, di), lambda i: (i, 0)),
                   pl.BlockSpec((tm, di), lambda i: (i, 0))],
        out_shape=[jax.ShapeDtypeStruct((T, di), BF16)] * 2,
        compiler_params=_params(("parallel",)),
        name="mlstm_inproj",
    )(x2, shift, scale, w_in)


def _qkv_kernel(xm_ref, halo_ref, cw_ref, cb_ref, wq_ref, wk_ref, wv_ref, wif_ref, bif_ref,
                q_ref, k_ref, v_ref, xc_ref, g_ref, *, tpb, nh, kscale):
    tm, di = xm_ref.shape
    dh = di // nh
    i = pl.program_id(0)
    xm = xm_ref[...]
    xf = xm.astype(F32)
    halo = halo_ref[...].astype(F32)
    halo = jnp.where(i % tpb == 0, 0.0, halo)
    ext = jnp.concatenate([halo, xf], axis=0)
    acc = cb_ref[...] + cw_ref[CONV_WIDTH - 1:CONV_WIDTH, :] * xf
    for s in range(1, CONV_WIDTH):
        acc = acc + cw_ref[CONV_WIDTH - 1 - s:CONV_WIDTH - s, :] * ext[CONV_HALO - s:CONV_HALO - s + tm, :]
    xc = acc * _sigmoid(acc)
    xcb = xc.astype(BF16)
    xc_ref[...] = xcb
    for h in range(nh):
        sl = slice(h * dh, (h + 1) * dh)
        q_ref[:, sl] = jnp.dot(xcb[:, sl], wq_ref[h], preferred_element_type=F32).astype(BF16)
        k_ref[:, sl] = (jnp.dot(xcb[:, sl], wk_ref[h], preferred_element_type=F32) * kscale).astype(BF16)
        v_ref[:, sl] = jnp.dot(xm[:, sl], wv_ref[h], preferred_element_type=F32).astype(BF16)
    g = jnp.dot(xcb, wif_ref[...], preferred_element_type=F32) + bif_ref[...]
    col = lax.broadcasted_iota(jnp.int32, g.shape, 1)
    logsig = jnp.minimum(g, 0.0) - jnp.log(1.0 + jnp.exp(-jnp.abs(g)))
    g = jnp.where(col >= nh, logsig, g)
    g_ref[...] = g[:, :2 * nh]


def _qkv(xm, conv_w, conv_b, wq, wk, wv, w_if, b_if, S):
    T, di = xm.shape
    nh = wq.shape[0]
    dh = di // nh
    tm = min(512, S)
    tpb = S // tm
    hb = tm // CONV_HALO
    wif = jnp.zeros((di, LANES), BF16).at[:, :2 * nh].set(w_if.astype(BF16))
    bif = jnp.zeros((1, LANES), F32).at[0, :2 * nh].set(b_if)
    kern = functools.partial(_qkv_kernel, tpb=tpb, nh=nh, kscale=float(dh) ** -0.5)
    full2 = lambda i: (0, 0)
    full3 = lambda i: (0, 0, 0)
    row = lambda i: (i, 0)
    return pl.pallas_call(
        kern,
        grid=(T // tm,),
        in_specs=[pl.BlockSpec((tm, di), row),
                  pl.BlockSpec((CONV_HALO, di), lambda i: (jnp.maximum(i * hb - 1, 0), 0)),
                  pl.BlockSpec((CONV_WIDTH, di), full2),
                  pl.BlockSpec((1, di), full2),
                  pl.BlockSpec((nh, dh, dh), full3),
                  pl.BlockSpec((nh, dh, dh), full3),
                  pl.BlockSpec((nh, dh, dh), full3),
                  pl.BlockSpec((di, LANES), full2),
                  pl.BlockSpec((1, LANES), full2)],
        out_specs=[pl.BlockSpec((tm, di), row)] * 4 + [pl.BlockSpec((tm, 2 * nh), row)],
        out_shape=[jax.ShapeDtypeStruct((T, di), BF16)] * 4 + [jax.ShapeDtypeStruct((T, 2 * nh), F32)],
        compiler_params=_params(("parallel",)),
        name="mlstm_qkv",
    )(xm, xm, conv_w, conv_b.reshape(1, di), wq, wk, wv, wif, bif)


def _mlstm_kernel(q_ref, k_ref, v_ref, gc_ref, gr_ref, o_ref, c_scr, n_scr, m_scr):
    L = q_ref.shape[0]
    nh = c_scr.shape[0]
    dh = c_scr.shape[1]

    @pl.when(pl.program_id(1) == 0)
    def _():
        c_scr[...] = jnp.zeros_like(c_scr)
        n_scr[...] = jnp.zeros_like(n_scr)
        m_scr[...] = jnp.zeros_like(m_scr)

    t_idx = lax.broadcasted_iota(jnp.int32, (L, L), 0)
    s_idx = lax.broadcasted_iota(jnp.int32, (L, L), 1)
    causal = s_idx <= t_idx
    for h in range(nh):
        hs = slice(h * dh, (h + 1) * dh)
        _mlstm_head(q_ref[:, hs], k_ref[:, hs], v_ref[:, hs], gc_ref[0, h], gr_ref[0, h],
                    o_ref.at[:, hs], c_scr.at[h], n_scr.at[h], m_scr.at[h], causal, t_idx, s_idx)


def _mlstm_head(q, k, v, gc, gr, o_ref, c_scr, n_scr, m_scr, causal, t_idx, s_idx):
    i_col, f_col = gc[:, 0:1], gc[:, 1:2]
    i_row, f_row = gr[0:1, :], gr[1:2, :]
    b_col = jnp.sum(jnp.where(causal, f_row, 0.0), axis=1, keepdims=True)
    b_row = jnp.sum(jnp.where(t_idx <= s_idx, f_col, 0.0), axis=0, keepdims=True)
    m_prev = m_scr[...]
    d = jnp.where(causal, b_col - b_row + i_row, -jnp.inf)
    a_col = b_col + m_prev
    m_t = jnp.maximum(a_col, jnp.max(d, axis=1, keepdims=True))
    w_intra = jnp.exp(d - m_t)
    w_inter = jnp.exp(a_col - m_t)
    qk = lax.dot_general(q, k, (((1,), (1,)), ((), ())), preferred_element_type=F32)
    s_mat = qk * w_intra
    c_b = c_scr[...].astype(BF16)
    inter = lax.dot_general(q, c_b, (((1,), (1,)), ((), ())), preferred_element_type=F32)
    num = jnp.dot(s_mat.astype(BF16), v, preferred_element_type=F32) + w_inter * inter
    qn = jnp.sum(q.astype(F32) * n_scr[...], axis=1, keepdims=True)
    den = jnp.sum(s_mat, axis=1, keepdims=True) + w_inter * qn
    hcap = num / jnp.maximum(jnp.abs(den), jnp.exp(-m_t))
    mu = jnp.mean(hcap, axis=1, keepdims=True)
    dv = hcap - mu
    var = jnp.mean(dv * dv, axis=1, keepdims=True)
    o_ref[...] = (dv * lax.rsqrt(var + LN_EPS)).astype(BF16)

    b_last = jnp.sum(f_row, axis=1, keepdims=True)
    ws_col = b_last - b_col + i_col
    m_new = jnp.maximum(b_last + m_prev, jnp.max(ws_col, axis=0, keepdims=True))
    decay = jnp.exp(b_last + m_prev - m_new)
    ws = jnp.exp(ws_col - m_new)
    vw = (v.astype(F32) * ws).astype(BF16)
    upd = lax.dot_general(vw, k, (((0,), (0,)), ((), ())), preferred_element_type=F32)
    c_scr[...] = decay * c_scr[...] + upd
    n_scr[...] = decay * n_scr[...] + jnp.sum(k.astype(F32) * ws, axis=0, keepdims=True)
    m_scr[...] = m_new


def _mlstm(q, k, v, gcol, grow, B, S):
    T, di = q.shape
    nh = gcol.shape[1]
    dh = di // nh
    L = MLSTM_CHUNK
    nc = S // L
    blk = pl.BlockSpec((L, di), lambda b, c: (b * nc + c, 0))
    return pl.pallas_call(
        _mlstm_kernel,
        grid=(B, nc),
        in_specs=[blk, blk, blk,
                  pl.BlockSpec((1, nh, L, 2), lambda b, c: (b, 0, c, 0)),
                  pl.BlockSpec((1, nh, 2, L), lambda b, c: (b, 0, 0, c))],
        out_specs=blk,
        out_shape=jax.ShapeDtypeStruct((T, di), BF16),
        scratch_shapes=[pltpu.VMEM((nh, dh, dh), F32), pltpu.VMEM((nh, 1, dh), F32), pltpu.VMEM((nh, 1, 1), F32)],
        compiler_params=_params(("parallel", "arbitrary")),
        name="mlstm_scan",
    )(q, k, v, gcol, grow)


def _mlstm_out_kernel(hn_ref, xc_ref, z_ref, x_ref, gn_ref, skip_ref, w_ref, gate_ref, lg_ref, lb_ref, o_ref):
    z = z_ref[...].astype(F32)
    u = (hn_ref[...].astype(F32) * gn_ref[...] + skip_ref[...] * xc_ref[...].astype(F32)) * (z * _sigmoid(z))
    y = jnp.dot(u.astype(BF16), w_ref[...], preferred_element_type=F32)
    r = DEEPNORM_ALPHA * x_ref[...] + gate_ref[0] * y
    o_ref[...] = _layer_norm(r, lg_ref[...], lb_ref[...])


def _mlstm_out(hn, xc, z, x2, gn_g, skip, w_out, gate, ln_g, ln_b, S):
    T, D = x2.shape
    di = hn.shape[1]
    tm = min(512, S)
    tpb = S // tm
    row = lambda i: (i, 0)
    full2 = lambda i: (0, 0)
    return pl.pallas_call(
        _mlstm_out_kernel,
        grid=(T // tm,),
        in_specs=[pl.BlockSpec((tm, di), row), pl.BlockSpec((tm, di), row), pl.BlockSpec((tm, di), row),
                  pl.BlockSpec((tm, D), row),
                  pl.BlockSpec((1, di), full2), pl.BlockSpec((1, di), full2),
                  pl.BlockSpec((di, D), full2),
                  pl.BlockSpec((1, 1, D), lambda i: (i // tpb, 0, 0)),
                  pl.BlockSpec((1, D), full2), pl.BlockSpec((1, D), full2)],
        out_specs=pl.BlockSpec((tm, D), row),
        out_shape=jax.ShapeDtypeStruct((T, D), F32),
        compiler_params=_params(("parallel",)),
        name="mlstm_out_ln",
    )(hn, xc, z, x2, gn_g.reshape(1, di), skip.reshape(1, di), w_out, gate, ln_g.reshape(1, D), ln_b.reshape(1, D))


def _route_kernel(x_ref, sh_ref, sc_ref, wrt_ref, br_ref, tri_ref, out_ref, cnt_ref):
    E = wrt_ref.shape[0]
    epg = E // N_GROUPS
    h = x_ref[...] * (1.0 + sc_ref[0]) + sh_ref[0]
    lt = lax.dot_general(wrt_ref[...], h, (((1,), (1,)), ((), ())),
                         precision=lax.Precision.HIGHEST, preferred_element_type=F32)
    ex = jnp.exp(lt - jnp.max(lt, axis=0, keepdims=True))
    probs = ex / jnp.sum(ex, axis=0, keepdims=True)
    sel = probs + br_ref[...]
    srow = [sel[e:e + 1, :] for e in range(E)]
    prow = [probs[e:e + 1, :] for e in range(E)]
    gscore = []
    for g in range(N_GROUPS):
        r = srow[g * epg:(g + 1) * epg]
        best = None
        for a in range(epg):
            for b in range(a + 1, epg):
                pair = r[a] + r[b]
                best = pair if best is None else jnp.maximum(best, pair)
        gscore.append(best)
    gmax = functools.reduce(jnp.maximum, gscore)
    chosen, taken = [], None
    for g in range(N_GROUPS):
        c = gscore[g] == gmax
        if taken is not None:
            c = c & jnp.logical_not(taken)
        taken = c if taken is None else (taken | c)
        chosen.append(c)
    zero = jnp.zeros_like(srow[0])
    e1 = zero
    e2 = zero
    p1 = zero
    p2 = zero
    firsts, seconds = [], []
    for e in range(E):
        g = e // epg
        rank = zero
        for o in range(g * epg, (g + 1) * epg):
            if o == e:
                continue
            beats = (srow[o] > srow[e]) | ((srow[o] == srow[e]) & (o < e))
            rank = rank + jnp.where(beats, 1.0, 0.0)
        is1 = chosen[g] & (rank == 0.0)
        is2 = chosen[g] & (rank == 1.0)
        firsts.append(is1)
        seconds.append(is2)
        e1 = e1 + jnp.where(is1, float(e), 0.0)
        e2 = e2 + jnp.where(is2, float(e), 0.0)
        p1 = p1 + jnp.where(is1, prow[e], 0.0)
        p2 = p2 + jnp.where(is2, prow[e], 0.0)
    mask = jnp.concatenate([jnp.where(firsts[e] | seconds[e], 1.0, 0.0) for e in range(E)], axis=0)
    prefix = jnp.dot(mask.astype(BF16), tri_ref[...], preferred_element_type=F32)
    counts = jnp.sum(mask, axis=1, keepdims=True)
    run_start = jnp.zeros((1, 1), F32)
    s1 = zero
    s2 = zero
    for e in range(E):
        pe = prefix[e:e + 1, :] + run_start
        s1 = s1 + jnp.where(firsts[e], pe, 0.0)
        s2 = s2 + jnp.where(seconds[e], pe, 0.0)
        run_start = run_start + jnp.floor((counts[e:e + 1, :] + (RUN_ALIGN - 1)) * (1.0 / RUN_ALIGN)) * RUN_ALIGN
    psum = p1 + p2
    out_ref[...] = jnp.concatenate([e1, e2, p1 / psum, p2 / psum, s1, s2, zero, zero], axis=0)
    cnt_ref[...] = jnp.broadcast_to(counts, cnt_ref.shape)


def _route(x2, shift, scale, w_router, b_router, S):
    T, D = x2.shape
    E = w_router.shape[1]
    tm = min(MOE_TILE, S)
    tpb = S // tm
    tri = (jnp.arange(tm)[:, None] < jnp.arange(tm)[None, :]).astype(BF16)
    return pl.pallas_call(
        _route_kernel,
        grid=(T // tm,),
        in_specs=[pl.BlockSpec((tm, D), lambda i: (i, 0)),
                  pl.BlockSpec((1, 1, D), lambda i: (i // tpb, 0, 0)),
                  pl.BlockSpec((1, 1, D), lambda i: (i // tpb, 0, 0)),
                  pl.BlockSpec((E, D), lambda i: (0, 0)),
                  pl.BlockSpec((E, 1), lambda i: (0, 0)),
                  pl.BlockSpec((tm, tm), lambda i: (0, 0))],
        out_specs=[pl.BlockSpec((8, tm), lambda i: (0, i)),
                   pl.BlockSpec((E, LANES), lambda i: (0, i))],
        out_shape=[jax.ShapeDtypeStruct((8, T), F32), jax.ShapeDtypeStruct((E, (T // tm) * LANES), F32)],
        compiler_params=_params(("parallel",)),
        name="moe_route",
    )(x2, shift, scale, w_router.T, b_router.reshape(E, 1), tri)


def _stage_rows(tm, n_exp):
    return MOE_TOPK * tm + n_exp * RUN_ALIGN


def _chunk_copies(tc_ref, ts_ref, ds_ref, step, n_exp, make):
    out = []
    for e in range(n_exp):
        cnt = tc_ref[step * n_exp + e]
        off = ts_ref[step * n_exp + e]
        dst = ds_ref[step * n_exp + e]
        for k in reversed(range(RUN_ALIGN.bit_length() - 1, CHUNK_BITS)):
            done = (cnt >> (k + 1)) << (k + 1)
            out.append(((cnt & (1 << k)) != 0,
                        make(pl.multiple_of(off + done, RUN_ALIGN), pl.multiple_of(dst + done, RUN_ALIGN), 1 << k)))
    return out


def _dispatch_kernel(tc_ref, ts_ref, ds_ref, pe_ref, x_ref, sh_ref, sc_ref, pos_ref, xs_ref,
                     stage, zbuf, sem, zsem):
    tm = x_ref.shape[0]
    rb = zbuf.shape[0]
    i = pl.program_id(0)
    slot = i % 2

    n_exp = pe_ref.shape[0] - 1
    n_blocks = xs_ref.shape[0] // rb

    def pad_copy(e):
        return pltpu.make_async_copy(zbuf, xs_ref.at[pl.ds(pl.multiple_of(pe_ref[e + 1] - rb, rb), rb)], zsem)

    def tail_copy(j):
        return pltpu.make_async_copy(zbuf, xs_ref.at[pl.ds(pl.multiple_of(pe_ref[n_exp] + j * rb, rb), rb)], zsem)

    def tail_live(j):
        return pe_ref[n_exp] // rb + j < n_blocks

    @pl.when(i == 0)
    def _():
        zbuf[...] = jnp.zeros_like(zbuf)
        for e in range(n_exp):
            @pl.when(pe_ref[e + 1] > pe_ref[e])
            def _():
                pad_copy(e).start()
        for j in range(n_exp):
            @pl.when(tail_live(j))
            def _():
                tail_copy(j).start()
        for e in range(n_exp):
            @pl.when(pe_ref[e + 1] > pe_ref[e])
            def _():
                pad_copy(e).wait()
        for j in range(n_exp):
            @pl.when(tail_live(j))
            def _():
                tail_copy(j).wait()

    h = (x_ref[...] * (1.0 + sc_ref[0]) + sh_ref[0]).astype(BF16)
    pos = pos_ref[0]
    slot_id = lax.broadcasted_iota(jnp.int32, (stage.shape[1], tm), 0)
    perm = jnp.where((slot_id == pos[0:1, :]) | (slot_id == pos[1:2, :]), 1.0, 0.0).astype(BF16)
    stage[slot] = jnp.dot(perm, h, preferred_element_type=F32)

    def copies(step, s):
        def make(src_row, dst_row, size):
            return pltpu.make_async_copy(stage.at[s, pl.ds(src_row, size)], xs_ref.at[pl.ds(dst_row, size)], sem.at[s])
        return _chunk_copies(tc_ref, ts_ref, ds_ref, step, n_exp, make)

    for live, cp in copies(i, slot):
        pl.when(live)(cp.start)
    for live, cp in copies(jnp.maximum(i - 1, 0), 1 - slot):
        pl.when(live & (i > 0))(cp.wait)
    for live, cp in copies(i, slot):
        pl.when(live & (i == pl.num_programs(0) - 1))(cp.wait)


def _dispatch(x2, shift, scale, posrow, tcnt, tstart, dstart, pends0, cap, S):
    T, D = x2.shape
    tm = min(MOE_TILE, S)
    tpb = S // tm
    mod = pl.BlockSpec((1, 1, D), lambda i, *_: (i // tpb, 0, 0))
    grid_spec = pltpu.PrefetchScalarGridSpec(
        num_scalar_prefetch=4,
        grid=(T // tm,),
        in_specs=[pl.BlockSpec((tm, D), lambda i, *_: (i, 0)), mod, mod,
                  pl.BlockSpec((1, 2, tm), lambda i, *_: (i, 0, 0))],
        out_specs=pl.BlockSpec(memory_space=pl.ANY),
        scratch_shapes=[pltpu.VMEM((2, _stage_rows(tm, pends0.shape[0] - 1), D), F32),
                        pltpu.VMEM((MOE_ROW_BLOCK, D), F32),
                        pltpu.SemaphoreType.DMA((2,)), pltpu.SemaphoreType.DMA(())],
    )
    return pl.pallas_call(
        _dispatch_kernel,
        grid_spec=grid_spec,
        out_shape=jax.ShapeDtypeStruct((cap, D), F32),
        compiler_params=_params(("arbitrary",)),
        name="moe_dispatch",
    )(tcnt, tstart, dstart, pends0, x2, shift, scale, posrow)


def _expert_kernel(be_ref, na_ref, xs_ref, wg_ref, wu_ref, wd_ref, ys_ref, wg_b, wu_b, wd_b):
    i = pl.program_id(0)
    active = i < na_ref[0]

    @pl.when(active & ((i == 0) | (be_ref[i] != be_ref[jnp.maximum(i - 1, 0)])))
    def _():
        wg_b[...] = wg_ref[0, 0].astype(BF16)
        wu_b[...] = wu_ref[0, 0].astype(BF16)
        wd_b[...] = wd_ref[0, 0].astype(BF16)

    @pl.when(active)
    def _():
        xb = xs_ref[...].astype(BF16)
        g = jnp.dot(xb, wg_b[...], preferred_element_type=F32)
        u = jnp.dot(xb, wu_b[...], preferred_element_type=F32)
        a = (g * _sigmoid(g)) * u
        ys_ref[...] = jnp.dot(a.astype(BF16), wd_b[...], preferred_element_type=F32)

    @pl.when(jnp.logical_not(active))
    def _():
        ys_ref[...] = jnp.zeros_like(ys_ref)


def _experts(xs, block_expert, n_active, w_gate, w_up, w_down, layer):
    cap, D = xs.shape
    _, E, _, F = w_gate.shape
    rb = MOE_ROW_BLOCK
    nb = cap // rb

    def blk(i, be, na):
        return jnp.maximum(jnp.minimum(i, na[0] - 1), 0)

    grid_spec = pltpu.PrefetchScalarGridSpec(
        num_scalar_prefetch=2,
        grid=(nb,),
        in_specs=[pl.BlockSpec((rb, D), lambda i, be, na: (blk(i, be, na), 0)),
                  pl.BlockSpec((1, 1, D, F), lambda i, be, na: (layer, be[blk(i, be, na)], 0, 0)),
                  pl.BlockSpec((1, 1, D, F), lambda i, be, na: (layer, be[blk(i, be, na)], 0, 0)),
                  pl.BlockSpec((1, 1, F, D), lambda i, be, na: (layer, be[blk(i, be, na)], 0, 0))],
        out_specs=pl.BlockSpec((rb, D), lambda i, be, na: (i, 0)),
        scratch_shapes=[pltpu.VMEM((D, F), BF16), pltpu.VMEM((D, F), BF16), pltpu.VMEM((F, D), BF16)],
    )
    return pl.pallas_call(
        _expert_kernel,
        grid_spec=grid_spec,
        out_shape=jax.ShapeDtypeStruct((cap, D), F32),
        compiler_params=_params(("arbitrary",)),
        name="moe_experts",
    )(block_expert, n_active, xs, w_gate, w_up, w_down)


def _combine_kernel(tc_ref, ts_ref, ds_ref, ys_ref, rt_ref, x_ref, gate_ref, lg_ref, lb_ref, o_ref, stage, sem,
                    *, n_exp):
    tm = x_ref.shape[0]
    i = pl.program_id(0)
    slot = i % 2

    def copies(step, s):
        def make(stage_row, ys_row, size):
            return pltpu.make_async_copy(ys_ref.at[pl.ds(ys_row, size)], stage.at[s, pl.ds(stage_row, size)], sem.at[s])
        return _chunk_copies(tc_ref, ts_ref, ds_ref, step, n_exp, make)

    @pl.when(i == 0)
    def _():
        stage[...] = jnp.zeros_like(stage)

    for live, cp in copies(i, slot):
        pl.when(live & (i == 0))(cp.start)
    nxt = jnp.minimum(i + 1, pl.num_programs(0) - 1)
    for live, cp in copies(nxt, 1 - slot):
        pl.when(live & (i + 1 < pl.num_programs(0)))(cp.start)
    for live, cp in copies(i, slot):
        pl.when(live)(cp.wait)

    rt = rt_ref[...]
    ysb = stage[slot].astype(BF16)
    slot_id = lax.broadcasted_iota(jnp.int32, (tm, stage.shape[1]), 1).astype(F32)
    y = None
    for j in range(MOE_TOPK):
        perm = jnp.where(slot_id == rt[:, 4 + j:5 + j], 1.0, 0.0).astype(BF16)
        part = rt[:, 2 + j:3 + j] * jnp.dot(perm, ysb, preferred_element_type=F32)
        y = part if y is None else y + part
    r = DEEPNORM_ALPHA * x_ref[...] + gate_ref[0] * y
    o_ref[...] = _layer_norm(r, lg_ref[...], lb_ref[...])


def _combine(ys, route_t, x2, gate, ln_g, ln_b, tcnt, tstart, dstart, S):
    T, D = x2.shape
    tm = min(MOE_TILE, S)
    tpb = S // tm
    grid_spec = pltpu.PrefetchScalarGridSpec(
        num_scalar_prefetch=3,
        grid=(T // tm,),
        in_specs=[pl.BlockSpec(memory_space=pl.ANY),
                  pl.BlockSpec((tm, 8), lambda i, *_: (i, 0)),
                  pl.BlockSpec((tm, D), lambda i, *_: (i, 0)),
                  pl.BlockSpec((1, 1, D), lambda i, *_: (i // tpb, 0, 0)),
                  pl.BlockSpec((1, D), lambda i, *_: (0, 0)),
                  pl.BlockSpec((1, D), lambda i, *_: (0, 0))],
        out_specs=pl.BlockSpec((tm, D), lambda i, *_: (i, 0)),
        scratch_shapes=[pltpu.VMEM((2, _stage_rows(tm, tcnt.shape[0] // (T // tm)), D), F32),
                        pltpu.SemaphoreType.DMA((2,))],
    )
    return pl.pallas_call(
        functools.partial(_combine_kernel, n_exp=tcnt.shape[0] // (T // tm)),
        grid_spec=grid_spec,
        out_shape=jax.ShapeDtypeStruct((T, D), F32),
        compiler_params=_params(("arbitrary",)),
        name="moe_combine_ln",
    )(tcnt, tstart, dstart, ys, route_t, x2, gate, ln_g.reshape(1, D), ln_b.reshape(1, D))


def _moe_layer(x2, shift, scale, gate, ln_g, ln_b, w_router, b_router, w_gate, w_up, w_down, layer, S):
    T, D = x2.shape
    E = w_router.shape[1]
    rb = MOE_ROW_BLOCK
    tm = min(MOE_TILE, S)
    nt = T // tm
    assert tm < (1 << CHUNK_BITS)
    route, cnt = _route(x2, shift, scale, w_router, b_router, S)
    tcnt = cnt.reshape(E, nt, LANES)[:, :, 0].T.astype(jnp.int32)
    tcnt = (tcnt + RUN_ALIGN - 1) // RUN_ALIGN * RUN_ALIGN
    base = jnp.cumsum(tcnt, axis=0) - tcnt
    counts = jnp.sum(tcnt, axis=0)
    padded = (counts + rb - 1) // rb * rb
    pends = jnp.cumsum(padded)
    pstarts = pends - padded
    tstart = jnp.cumsum(tcnt, axis=1) - tcnt
    dstart = pstarts[None, :] + base
    posrow = jnp.transpose(route[4:6].astype(jnp.int32).reshape(2, nt, tm), (1, 0, 2))
    nb = -(-(T * MOE_TOPK + nt * E * (RUN_ALIGN - 1)) // rb) + E
    cap = nb * rb
    block_start = jnp.arange(nb, dtype=jnp.int32) * rb
    block_expert = jnp.minimum(jnp.sum(block_start[:, None] >= pends[None, :], axis=1), E - 1).astype(jnp.int32)
    n_active = (pends[-1:] // rb).astype(jnp.int32)
    pends0 = jnp.concatenate([jnp.zeros((1,), jnp.int32), pends.astype(jnp.int32)])
    tables = [t.reshape(-1).astype(jnp.int32) for t in (tcnt, tstart, dstart)]
    xs = _dispatch(x2, shift, scale, posrow, *tables, pends0, cap, S)
    ys = _experts(xs, block_expert, n_active, w_gate, w_up, w_down, layer)
    return _combine(ys, route.T, x2, gate, ln_g, ln_b, *tables, S)


def _moba_proj_kernel(x_ref, shq_ref, scq_ref, shkv_ref, sckv_ref, wq_ref, wkv_ref,
                      q_ref, k_ref, vt_ref, km_ref, *, qscale, tpb, slopes):
    tm, D = x_ref.shape
    hd = HEAD_DIM
    nh = D // hd
    x = x_ref[...]
    hq = x * (1.0 + scq_ref[0]) + shq_ref[0]
    hkv = x * (1.0 + sckv_ref[0]) + shkv_ref[0]
    qf = jnp.dot(hq.astype(BF16), wq_ref[...], preferred_element_type=F32) * qscale
    kv = jnp.dot(hkv.astype(BF16), wkv_ref[...], preferred_element_type=F32)
    kf = kv[:, :D]
    vt = kv[:, D:].T
    ones_pad = jnp.where(lax.broadcasted_iota(jnp.int32, (V_ROWS - hd, tm), 0) == 0, 1.0, 0.0)
    for g in range(nh):
        vt_ref[0, g * V_ROWS:(g + 1) * V_ROWS, :] = jnp.concatenate(
            [vt[g * hd:(g + 1) * hd, :], ones_pad], axis=0).astype(BF16)
    lane = lax.broadcasted_iota(jnp.int32, (1, LANES), 1)
    head_lanes = lane >= hd
    pos = (pl.program_id(0) % tpb) * tm + lax.broadcasted_iota(jnp.int32, (tm, 1), 0)
    pos_f = pos.astype(F32)
    onehot = jnp.where(lane == pos // MOBA_BLOCK, 1.0, 0.0)
    q_fill = jnp.where((lane >= BIAS_LANE) & (lane < BIAS_LANE + 3), 1.0, 0.0)
    nbt = tm // MOBA_BLOCK
    means = [jnp.mean(kf[j * MOBA_BLOCK:(j + 1) * MOBA_BLOCK, :], axis=0, keepdims=True) for j in range(nbt)]
    km = jnp.concatenate(means + [jnp.zeros((8 - nbt, D), F32)], axis=0)
    for j in range(nh // 2):
        cs = slice(j * LANES, (j + 1) * LANES)
        qb, kb, mb = qf[:, cs], kf[:, cs], km[:, cs]
        q_pair = (pltpu.roll(qb, hd, axis=1), qb)
        k_pair = (pltpu.roll(kb, hd, axis=1), kb)
        m_pair = (pltpu.roll(mb, hd, axis=1), mb)
        for e in range(2):
            g = 2 * j + e
            gs = slice(g * LANES, (g + 1) * LANES)
            q_ref[:, gs] = jnp.where(head_lanes, q_pair[e], q_fill).astype(BF16)
            c = (slopes[g] * LOG2E) * pos_f
            hi = c.astype(BF16).astype(F32)
            mid = (c - hi).astype(BF16).astype(F32)
            lo = c - hi - mid
            aug = jnp.where(lane == BIAS_LANE, hi,
                            jnp.where(lane == BIAS_LANE + 1, mid,
                                      jnp.where(lane == BIAS_LANE + 2, lo, onehot)))
            k_ref[:, gs] = jnp.where(head_lanes, k_pair[e], aug).astype(BF16)
            mg = jnp.where(head_lanes, m_pair[e], 0.0)
            for jb in range(nbt):
                km_ref[jb, :, gs] = mg[jb:jb + 1, :]


def _moba_proj(x2, shq, scq, shkv, sckv, wq, wkv, B, S):
    T, D = x2.shape
    nh = D // HEAD_DIM
    tm = min(512, S)
    tpb = S // tm
    nbt = tm // MOBA_BLOCK
    assert S // MOBA_BLOCK <= BIAS_LANE and nbt <= 8
    slopes = tuple(2.0 ** (-8.0 * (h + 1.0) / nh) for h in range(nh))
    mod = pl.BlockSpec((1, 1, D), lambda i: (i // tpb, 0, 0))
    kern = functools.partial(_moba_proj_kernel, qscale=float(HEAD_DIM) ** -0.5 * LOG2E, tpb=tpb, slopes=slopes)
    return pl.pallas_call(
        kern,
        grid=(T // tm,),
        in_specs=[pl.BlockSpec((tm, D), lambda i: (i, 0)), mod, mod, mod, mod,
                  pl.BlockSpec((D, D), lambda i: (0, 0)),
                  pl.BlockSpec((D, 2 * D), lambda i: (0, 0))],
        out_specs=[pl.BlockSpec((tm, nh * LANES), lambda i: (i, 0)),
                   pl.BlockSpec((tm, nh * LANES), lambda i: (i, 0)),
                   pl.BlockSpec((1, nh * V_ROWS, tm), lambda i: (i // tpb, 0, i % tpb)),
                   pl.BlockSpec((nbt, 1, nh * LANES), lambda i: (i, 0, 0))],
        out_shape=[jax.ShapeDtypeStruct((T, nh * LANES), BF16), jax.ShapeDtypeStruct((T, nh * LANES), BF16),
                   jax.ShapeDtypeStruct((B, nh * V_ROWS, S), BF16),
                   jax.ShapeDtypeStruct((T // MOBA_BLOCK, 1, nh * LANES), F32)],
        compiler_params=_params(("parallel",)),
        name="moba_proj",
    )(x2, shq, scq, shkv, sckv, wq, wkv)


def _moba_select_kernel(q_ref, km_ref, qat_ref):
    tq = q_ref.shape[0]
    G = q_ref.shape[1] // LANES
    nblk = km_ref.shape[1]
    blk_id = lax.broadcasted_iota(jnp.int32, (nblk, tq), 0)
    own = (pl.program_id(2) * tq + lax.broadcasted_iota(jnp.int32, (1, tq), 1)) // MOBA_BLOCK
    for g in range(G):
        gs = slice(g * LANES, (g + 1) * LANES)
        qgt = q_ref[:, gs].astype(F32).T
        gate = jnp.dot(km_ref[0, :, gs], qgt, precision=lax.Precision.HIGHEST,
                       preferred_element_type=F32)
        gsc = jnp.where(blk_id < own, gate, -jnp.inf)
        picked = blk_id < 0
        for _ in range(MOBA_TOPK):
            mx = jnp.max(gsc, axis=0, keepdims=True)
            first = jnp.min(jnp.where(gsc == mx, blk_id, nblk), axis=0, keepdims=True)
            pick = blk_id == first
            picked = picked | pick
            gsc = jnp.where(pick, -jnp.inf, gsc)
        visible = (picked & (blk_id < own)) | (blk_id == own)
        pen = jnp.where(visible, 0.0, NEG)
        qat_ref[0, g] = (qgt + jnp.concatenate([pen, jnp.zeros((LANES - nblk, tq), F32)], axis=0)).astype(BF16)


def _moba_select(q, km, B, S):
    nh = q.shape[1] // LANES
    G = MOBA_HEADS_PER_STEP
    tq = min(512, S)
    nblk = S // MOBA_BLOCK
    return pl.pallas_call(
        _moba_select_kernel,
        grid=(B, nh // G, S // tq),
        in_specs=[pl.BlockSpec((tq, G * LANES), lambda b, hp, i: (b * (S // tq) + i, hp)),
                  pl.BlockSpec((1, nblk, G * LANES), lambda b, hp, i: (b, 0, hp))],
        out_specs=pl.BlockSpec((1, G, LANES, tq), lambda b, hp, i: (b, hp, 0, i)),
        out_shape=jax.ShapeDtypeStruct((B, nh, LANES, S), BF16),
        compiler_params=_params(("parallel", "parallel", "parallel")),
        name="moba_select",
    )(q, km)


def _moba_kernel(qat_ref, k_ref, vt_ref, o_ref, st_scr, m_scr, acc_scr):
    G = qat_ref.shape[1]
    BLK = qat_ref.shape[3]
    hd = HEAD_DIM
    nhalf = BLK // LANES
    own = pl.program_id(2)
    k_off = lax.broadcasted_iota(jnp.int32, (BLK, 1), 0)
    q_off = lax.broadcasted_iota(jnp.int32, (1, LANES), 1)

    m_scr[...] = jnp.full(m_scr.shape, NEG, F32)
    acc_scr[...] = jnp.zeros_like(acc_scr)

    def scores(n, slot):
        start = pl.multiple_of(n * BLK, BLK)
        for g in range(G):
            kn = k_ref[pl.ds(start, BLK), g * LANES:(g + 1) * LANES]
            st_scr[slot, g] = jnp.dot(kn, qat_ref[0, g], preferred_element_type=F32)

    def update(n, slot, causal):
        start = pl.multiple_of(n * BLK, BLK)
        for g in range(G):
            alphas, ps = [], []
            for hf in range(nhalf):
                c = g * nhalf + hf
                ls = slice(hf * LANES, (hf + 1) * LANES)
                st = st_scr[slot, g, :, ls]
                if causal:
                    st = jnp.where(k_off <= q_off + hf * LANES, st, NEG)
                m = m_scr[c]
                m_new = jnp.maximum(m, jnp.max(st, axis=0, keepdims=True))
                alphas.append(jnp.exp2(m - m_new))
                ps.append(jnp.exp2(st - m_new).astype(BF16))
                m_scr[c] = m_new
            vtn = vt_ref[0, g * V_ROWS:(g + 1) * V_ROWS, pl.ds(start, BLK)]
            pv = jnp.dot(vtn, jnp.concatenate(ps, axis=1), preferred_element_type=F32)
            acc_scr[g] = jnp.concatenate(alphas, axis=1) * acc_scr[g] + pv

    scores(0, 0)

    def pair(n):
        scores(n + 1, 1)
        update(n, 0, False)
        scores(n + 2, 0)
        update(n + 1, 1, False)

    def two_pairs(i, carry):
        pair(4 * i)
        pair(4 * i + 2)
        return carry

    def one_pair(i, carry):
        pair(4 * (own // 4) + 2 * i)
        return carry

    lax.fori_loop(0, own // 4, two_pairs, 0)
    lax.fori_loop(0, (own % 4) // 2, one_pair, 0)

    @pl.when(own % 2 == 0)
    def _():
        update(own, 0, True)

    @pl.when(own % 2 == 1)
    def _():
        scores(own, 1)
        update(own - 1, 0, False)
        update(own, 1, True)
    for g in range(G):
        acc = acc_scr[g]
        o_ref[0, g * hd:(g + 1) * hd, :] = (acc[:hd, :] * (1.0 / acc[hd:hd + 1, :])).astype(BF16)


def _moba_attn(qat, k, vt, B, S):
    nh = qat.shape[1]
    G = MOBA_HEADS_PER_STEP
    BLK = MOBA_BLOCK
    nblk = S // BLK
    hd = HEAD_DIM
    return pl.pallas_call(
        _moba_kernel,
        grid=(B, nh // G, nblk),
        in_specs=[pl.BlockSpec((1, G, LANES, BLK), lambda b, hp, qi: (b, hp, 0, qi)),
                  pl.BlockSpec((S, G * LANES), lambda b, hp, qi: (b, hp)),
                  pl.BlockSpec((1, G * V_ROWS, S), lambda b, hp, qi: (b, hp, 0))],
        out_specs=pl.BlockSpec((1, G * hd, BLK), lambda b, hp, qi: (b, hp, qi)),
        out_shape=jax.ShapeDtypeStruct((B, nh * hd, S), BF16),
        scratch_shapes=[pltpu.VMEM((2, G, BLK, BLK), F32),
                        pltpu.VMEM((G * (BLK // LANES), 1, LANES), F32),
                        pltpu.VMEM((G, V_ROWS, BLK), F32)],
        compiler_params=_params(("parallel", "parallel", "arbitrary")),
        name="moba_attn",
    )(qat, k, vt)


def _attn_out_kernel(a_ref, x_ref, w_ref, gate_ref, lg_ref, lb_ref, o_ref):
    y = lax.dot_general(a_ref[0], w_ref[...], (((0,), (0,)), ((), ())), preferred_element_type=F32)
    r = DEEPNORM_ALPHA * x_ref[...] + gate_ref[0] * y
    o_ref[...] = _layer_norm(r, lg_ref[...], lb_ref[...])


def _attn_out(at, x2, w, gate, ln_g, ln_b, S):
    T, D = x2.shape
    K = at.shape[1]
    tm = min(512, S)
    tpb = S // tm
    row = lambda i: (i, 0)
    full2 = lambda i: (0, 0)
    return pl.pallas_call(
        _attn_out_kernel,
        grid=(T // tm,),
        in_specs=[pl.BlockSpec((1, K, tm), lambda i: (i // tpb, 0, i % tpb)), pl.BlockSpec((tm, D), row),
                  pl.BlockSpec((K, D), full2),
                  pl.BlockSpec((1, 1, D), lambda i: (i // tpb, 0, 0)),
                  pl.BlockSpec((1, D), full2), pl.BlockSpec((1, D), full2)],
        out_specs=pl.BlockSpec((tm, D), row),
        out_shape=jax.ShapeDtypeStruct((T, D), F32),
        compiler_params=_params(("parallel",)),
        name="attn_out_ln",
    )(at, x2, w, gate, ln_g.reshape(1, D), ln_b.reshape(1, D))


def kernel(x, c, w_ada, b_ada, ln_g, ln_b, a_w_in, a_conv_w, a_conv_b, a_wq, a_wk, a_wv, a_w_if, a_b_if,
           a_gn_g, a_skip, a_w_out, b_w_kv, b_wq, b_wo, moe_w_router, moe_b_router, moe_w_gate, moe_w_up,
           moe_w_down):
    B, S, D = x.shape
    T = B * S
    assert DEPTH == 2 and S % MOBA_BLOCK == 0 and S % MLSTM_CHUNK == 0
    n_layer_mod = DEPTH * N_MOD_PER_LAYER * D
    cond = _ada_cond(c, w_ada, b_ada)
    mods = cond[:, :n_layer_mod].reshape(B, DEPTH, 2, 3, 1, D)
    kv_mod = cond[:, n_layer_mod:].reshape(B, 2, 1, D)

    def mod3(layer, sub):
        m = mods[:, layer, sub]
        return m[:, 0], m[:, 1], 1.0 + m[:, 2]

    xf = x.reshape(T, D)

    shift, scale, gate = mod3(0, 0)
    nh = MLSTM_HEADS
    xm, z = _inproj(xf, shift, scale, a_w_in[0].astype(BF16), S)
    q, k, v, xc, gts = _qkv(xm, a_conv_w[0], a_conv_b[0], a_wq[0].astype(BF16), a_wk[0].astype(BF16),
                            a_wv[0].astype(BF16), a_w_if[0], a_b_if[0], S)
    g4 = gts.reshape(B, S, 2, nh)
    gcol = jnp.transpose(g4, (0, 3, 1, 2))
    grow = jnp.transpose(g4, (0, 3, 2, 1))
    hn = _mlstm(q, k, v, gcol, grow, B, S)
    xf = _mlstm_out(hn, xc, z, xf, a_gn_g[0], a_skip[0], a_w_out[0].astype(BF16), gate, ln_g[0, 0], ln_b[0, 0], S)
    shift, scale, gate = mod3(0, 1)
    xf = _moe_layer(xf, shift, scale, gate, ln_g[0, 1], ln_b[0, 1], moe_w_router, moe_b_router,
                    moe_w_gate, moe_w_up, moe_w_down, 0, S)

    shift, scale, gate = mod3(1, 0)
    q, k, vt, km = _moba_proj(xf, shift, scale, kv_mod[:, 0], kv_mod[:, 1], b_wq[0].astype(BF16),
                              b_w_kv.astype(BF16), B, S)
    km = km.reshape(B, S // MOBA_BLOCK, km.shape[-1])
    attn_t = _moba_attn(_moba_select(q, km, B, S), k, vt, B, S)
    xf = _attn_out(attn_t, xf, b_wo[0].astype(BF16), gate, ln_g[1, 0], ln_b[1, 0], S)
    shift, scale, gate = mod3(1, 1)
    xf = _moe_layer(xf, shift, scale, gate, ln_g[1, 1], ln_b[1, 1], moe_w_router, moe_b_router,
                    moe_w_gate, moe_w_up, moe_w_down, 1, S)
    return xf.reshape(B, S, D)
```

```python
import functools

import jax
import jax.numpy as jnp
from jax import lax
from jax.experimental import pallas as pl
from jax.experimental.pallas import tpu as pltpu

DEPTH = 2
MLSTM_HEADS = 4
CONV_WIDTH = 4
MLSTM_CHUNK = 512
ATTN_HEADS = 16
MOBA_BLOCK = 256
MOBA_TOPK = 3
N_EXPERTS = 16
N_GROUPS = 4
MOE_TOPK = 2
MOE_ROW_BLOCK = 256
MOE_TILE = 512
CHUNK_BITS = 10
RUN_ALIGN = 8
DEEPNORM_ALPHA = (2.0 * DEPTH) ** 0.25
LN_EPS = 1e-5
N_MOD_PER_LAYER = 6

HEAD_DIM = 64
V_ROWS = 80
MOBA_HEADS_PER_STEP = 4
BIAS_LANE = 32
LOG2E = 1.4426950408889634

LANES = 128
CONV_HALO = 16
NEG = -1e30
VMEM_LIMIT = 56 * 1024 * 1024

F32 = jnp.float32
BF16 = jnp.bfloat16


def _sigmoid(x):
    return 1.0 / (1.0 + jnp.exp(-x))


def _params(sem, vmem=VMEM_LIMIT):
    return pltpu.CompilerParams(dimension_semantics=sem, vmem_limit_bytes=vmem)


def _layer_norm(r, g, b):
    mu = jnp.mean(r, axis=-1, keepdims=True)
    d = r - mu
    var = jnp.mean(d * d, axis=-1, keepdims=True)
    return d * lax.rsqrt(var + LN_EPS) * g + b


def _ada_kernel(c_ref, w_ref, b_ref, o_ref):
    c = c_ref[...]
    s = c * _sigmoid(c)
    o_ref[...] = jnp.dot(s.astype(BF16), w_ref[...].astype(BF16),
                         preferred_element_type=F32) + b_ref[...]


def _ada_cond(c, w_ada, b_ada):
    B, D = c.shape
    N = w_ada.shape[1]
    tn = 2048 if N % 2048 == 0 else N
    cp = jnp.zeros((8, D), F32).at[:B].set(c)
    out = pl.pallas_call(
        _ada_kernel,
        grid=(N // tn,),
        in_specs=[pl.BlockSpec((8, D), lambda j: (0, 0)),
                  pl.BlockSpec((D, tn), lambda j: (0, j)),
                  pl.BlockSpec((1, tn), lambda j: (0, j))],
        out_specs=pl.BlockSpec((8, tn), lambda j: (0, j)),
        out_shape=jax.ShapeDtypeStruct((8, N), F32),
        compiler_params=_params(("arbitrary",)),
        name="ada_cond",
    )(cp, w_ada, b_ada.reshape(1, N))
    return out[:B]


def _inproj_kernel(x_ref, sh_ref, sc_ref, w_ref, xm_ref, z_ref):
    di = xm_ref.shape[-1]
    h = x_ref[...] * (1.0 + sc_ref[0]) + sh_ref[0]
    r = jnp.dot(h.astype(BF16), w_ref[...], preferred_element_type=F32)
    xm_ref[...] = r[:, :di].astype(BF16)
    z_ref[...] = r[:, di:].astype(BF16)


def _inproj(x2, shift, scale, w_in, S):
    T, D = x2.shape
    di = w_in.shape[1] // 2
    tm = min(512, S)
    tpb = S // tm
    return pl.pallas_call(
        _inproj_kernel,
        grid=(T // tm,),
        in_specs=[pl.BlockSpec((tm, D), lambda i: (i, 0)),
                  pl.BlockSpec((1, 1, D), lambda i: (i // tpb, 0, 0)),
                  pl.BlockSpec((1, 1, D), lambda i: (i // tpb, 0, 0)),
                  pl.BlockSpec((D, 2 * di), lambda i: (0, 0))],
        out_specs=[pl.BlockSpec((tm, di), lambda i: (i, 0)),
                   pl.BlockSpec((tm, di), lambda i: (i, 0))],
        out_shape=[jax.ShapeDtypeStruct((T, di), BF16)] * 2,
        compiler_params=_params(("parallel",)),
        name="mlstm_inproj",
    )(x2, shift, scale, w_in)


def _qkv_kernel(xm_ref, halo_ref, cw_ref, cb_ref, wq_ref, wk_ref, wv_ref, wif_ref, bif_ref,
                q_ref, k_ref, v_ref, xc_ref, g_ref, *, tpb, nh, kscale):
    tm, di = xm_ref.shape
    dh = di // nh
    i = pl.program_id(0)
    xm = xm_ref[...]
    xf = xm.astype(F32)
    halo = halo_ref[...].astype(F32)
    halo = jnp.where(i % tpb == 0, 0.0, halo)
    ext = jnp.concatenate([halo, xf], axis=0)
    acc = cb_ref[...] + cw_ref[CONV_WIDTH - 1:CONV_WIDTH, :] * xf
    for s in range(1, CONV_WIDTH):
        acc = acc + cw_ref[CONV_WIDTH - 1 - s:CONV_WIDTH - s, :] * ext[CONV_HALO - s:CONV_HALO - s + tm, :]
    xc = acc * _sigmoid(acc)
    xcb = xc.astype(BF16)
    xc_ref[...] = xcb
    for h in range(nh):
        sl = slice(h * dh, (h + 1) * dh)
        q_ref[:, sl] = jnp.dot(xcb[:, sl], wq_ref[h], preferred_element_type=F32).astype(BF16)
        k_ref[:, sl] = (jnp.dot(xcb[:, sl], wk_ref[h], preferred_element_type=F32) * kscale).astype(BF16)
        v_ref[:, sl] = jnp.dot(xm[:, sl], wv_ref[h], preferred_element_type=F32).astype(BF16)
    g = jnp.dot(xcb, wif_ref[...], preferred_element_type=F32) + bif_ref[...]
    col = lax.broadcasted_iota(jnp.int32, g.shape, 1)
    logsig = jnp.minimum(g, 0.0) - jnp.log(1.0 + jnp.exp(-jnp.abs(g)))
    g = jnp.where(col >= nh, logsig, g)
    g_ref[...] = g[:, :2 * nh]


def _qkv(xm, conv_w, conv_b, wq, wk, wv, w_if, b_if, S):
    T, di = xm.shape
    nh = wq.shape[0]
    dh = di // nh
    tm = min(512, S)
    tpb = S // tm
    hb = tm // CONV_HALO
    wif = jnp.zeros((di, LANES), BF16).at[:, :2 * nh].set(w_if.astype(BF16))
    bif = jnp.zeros((1, LANES), F32).at[0, :2 * nh].set(b_if)
    kern = functools.partial(_qkv_kernel, tpb=tpb, nh=nh, kscale=float(dh) ** -0.5)
    full2 = lambda i: (0, 0)
    full3 = lambda i: (0, 0, 0)
    row = lambda i: (i, 0)
    return pl.pallas_call(
        kern,
        grid=(T // tm,),
        in_specs=[pl.BlockSpec((tm, di), row),
                  pl.BlockSpec((CONV_HALO, di), lambda i: (jnp.maximum(i * hb - 1, 0), 0)),
                  pl.BlockSpec((CONV_WIDTH, di), full2),
                  pl.BlockSpec((1, di), full2),
                  pl.BlockSpec((nh, dh, dh), full3),
                  pl.BlockSpec((nh, dh, dh), full3),
                  pl.BlockSpec((nh, dh, dh), full3),
                  pl.BlockSpec((di, LANES), full2),
                  pl.BlockSpec((1, LANES), full2)],
        out_specs=[pl.BlockSpec((tm, di), row)] * 4 + [pl.BlockSpec((tm, 2 * nh), row)],
        out_shape=[jax.ShapeDtypeStruct((T, di), BF16)] * 4 + [jax.ShapeDtypeStruct((T, 2 * nh), F32)],
        compiler_params=_params(("parallel",)),
        name="mlstm_qkv",
    )(xm, xm, conv_w, conv_b.reshape(1, di), wq, wk, wv, wif, bif)


def _mlstm_kernel(q_ref, k_ref, v_ref, gc_ref, gr_ref, o_ref, c_scr, n_scr, m_scr):
    L = q_ref.shape[0]
    nh = c_scr.shape[0]
    dh = c_scr.shape[1]

    @pl.when(pl.program_id(1) == 0)
    def _():
        c_scr[...] = jnp.zeros_like(c_scr)
        n_scr[...] = jnp.zeros_like(n_scr)
        m_scr[...] = jnp.zeros_like(m_scr)

    t_idx = lax.broadcasted_iota(jnp.int32, (L, L), 0)
    s_idx = lax.broadcasted_iota(jnp.int32, (L, L), 1)
    causal = s_idx <= t_idx
    for h in range(nh):
        hs = slice(h * dh, (h + 1) * dh)
        _mlstm_head(q_ref[:, hs], k_ref[:, hs], v_ref[:, hs], gc_ref[0, h], gr_ref[0, h],
                    o_ref.at[:, hs], c_scr.at[h], n_scr.at[h], m_scr.at[h], causal, t_idx, s_idx)


def _mlstm_head(q, k, v, gc, gr, o_ref, c_scr, n_scr, m_scr, causal, t_idx, s_idx):
    i_col, f_col = gc[:, 0:1], gc[:, 1:2]
    i_row, f_row = gr[0:1, :], gr[1:2, :]
    b_col = jnp.sum(jnp.where(causal, f_row, 0.0), axis=1, keepdims=True)
    b_row = jnp.sum(jnp.where(t_idx <= s_idx, f_col, 0.0), axis=0, keepdims=True)
    m_prev = m_scr[...]
    d = jnp.where(causal, b_col - b_row + i_row, -jnp.inf)
    a_col = b_col + m_prev
    m_t = jnp.maximum(a_col, jnp.max(d, axis=1, keepdims=True))
    w_intra = jnp.exp(d - m_t)
    w_inter = jnp.exp(a_col - m_t)
    qk = lax.dot_general(q, k, (((1,), (1,)), ((), ())), preferred_element_type=F32)
    s_mat = qk * w_intra
    c_b = c_scr[...].astype(BF16)
    inter = lax.dot_general(q, c_b, (((1,), (1,)), ((), ())), preferred_element_type=F32)
    num = jnp.dot(s_mat.astype(BF16), v, preferred_element_type=F32) + w_inter * inter
    qn = jnp.sum(q.astype(F32) * n_scr[...], axis=1, keepdims=True)
    den = jnp.sum(s_mat, axis=1, keepdims=True) + w_inter * qn
    hcap = num / jnp.maximum(jnp.abs(den), jnp.exp(-m_t))
    mu = jnp.mean(hcap, axis=1, keepdims=True)
    dv = hcap - mu
    var = jnp.mean(dv * dv, axis=1, keepdims=True)
    o_ref[...] = (dv * lax.rsqrt(var + LN_EPS)).astype(BF16)

    b_last = jnp.sum(f_row, axis=1, keepdims=True)
    ws_col = b_last - b_col + i_col
    m_new = jnp.maximum(b_last + m_prev, jnp.max(ws_col, axis=0, keepdims=True))
    decay = jnp.exp(b_last + m_prev - m_new)
    ws = jnp.exp(ws_col - m_new)
    vw = (v.astype(F32) * ws).astype(BF16)
    upd = lax.dot_general(vw, k, (((0,), (0,)), ((), ())), preferred_element_type=F32)
    c_scr[...] = decay * c_scr[...] + upd
    n_scr[...] = decay * n_scr[...] + jnp.sum(k.astype(F32) * ws, axis=0, keepdims=True)
    m_scr[...] = m_new


def _mlstm(q, k, v, gcol, grow, B, S):
    T, di = q.shape
    nh = gcol.shape[1]
    dh = di // nh
    L = min(MLSTM_CHUNK, S)
    nc = S // L
    blk = pl.BlockSpec((L, di), lambda b, c: (b * nc + c, 0))
    return pl.pallas_call(
        _mlstm_kernel,
        grid=(B, nc),
        in_specs=[blk, blk, blk,
                  pl.BlockSpec((1, nh, L, 2), lambda b, c: (b, 0, c, 0)),
                  pl.BlockSpec((1, nh, 2, L), lambda b, c: (b, 0, 0, c))],
        out_specs=blk,
        out_shape=jax.ShapeDtypeStruct((T, di), BF16),
        scratch_shapes=[pltpu.VMEM((nh, dh, dh), F32), pltpu.VMEM((nh, 1, dh), F32), pltpu.VMEM((nh, 1, 1), F32)],
        compiler_params=_params(("parallel", "arbitrary")),
        name="mlstm_scan",
    )(q, k, v, gcol, grow)


def _mlstm_out_kernel(hn_ref, xc_ref, z_ref, x_ref, gn_ref, skip_ref, w_ref, gate_ref, lg_ref, lb_ref, o_ref):
    z = z_ref[...].astype(F32)
    u = (hn_ref[...].astype(F32) * gn_ref[...] + skip_ref[...] * xc_ref[...].astype(F32)) * (z * _sigmoid(z))
    y = jnp.dot(u.astype(BF16), w_ref[...], preferred_element_type=F32)
    r = DEEPNORM_ALPHA * x_ref[...] + gate_ref[0] * y
    o_ref[...] = _layer_norm(r, lg_ref[...], lb_ref[...])


def _mlstm_out(hn, xc, z, x2, gn_g, skip, w_out, gate, ln_g, ln_b, S):
    T, D = x2.shape
    di = hn.shape[1]
    tm = min(512, S)
    tpb = S // tm
    row = lambda i: (i, 0)
    full2 = lambda i: (0, 0)
    return pl.pallas_call(
        _mlstm_out_kernel,
        grid=(T // tm,),
        in_specs=[pl.BlockSpec((tm, di), row), pl.BlockSpec((tm, di), row), pl.BlockSpec((tm, di), row),
                  pl.BlockSpec((tm, D), row),
                  pl.BlockSpec((1, di), full2), pl.BlockSpec((1, di), full2),
                  pl.BlockSpec((di, D), full2),
                  pl.BlockSpec((1, 1, D), lambda i: (i // tpb, 0, 0)),
                  pl.BlockSpec((1, D), full2), pl.BlockSpec((1, D), full2)],
        out_specs=pl.BlockSpec((tm, D), row),
        out_shape=jax.ShapeDtypeStruct((T, D), F32),
        compiler_params=_params(("parallel",)),
        name="mlstm_out_ln",
    )(hn, xc, z, x2, gn_g.reshape(1, di), skip.reshape(1, di), w_out, gate, ln_g.reshape(1, D), ln_b.reshape(1, D))


def _route_kernel(x_ref, sh_ref, sc_ref, wrt_ref, br_ref, tri_ref, out_ref, cnt_ref):
    E = wrt_ref.shape[0]
    ng = N_GROUPS
    epg = E // ng
    tm = x_ref.shape[0]
    h = x_ref[...] * (1.0 + sc_ref[0]) + sh_ref[0]
    nt = (((1,), (1,)), ((), ()))
    w = wrt_ref[...]
    w_hi = w.astype(BF16)
    w_lo = (w - w_hi.astype(F32)).astype(BF16)
    h_hi = h.astype(BF16)
    h_lo = (h - h_hi.astype(F32)).astype(BF16)
    lt = (lax.dot_general(w_hi, h_hi, nt, preferred_element_type=F32)
          + lax.dot_general(w_hi, h_lo, nt, preferred_element_type=F32)
          + lax.dot_general(w_lo, h_hi, nt, preferred_element_type=F32))
    ex = jnp.exp(lt - jnp.max(lt, axis=0, keepdims=True))
    probs = ex / jnp.sum(ex, axis=0, keepdims=True)
    sel = probs + br_ref[...]
    member = [sel[k * ng:(k + 1) * ng, :] for k in range(epg)]
    pmember = [probs[k * ng:(k + 1) * ng, :] for k in range(epg)]
    gscore = None
    for a in range(epg):
        for b in range(a + 1, epg):
            pair = member[a] + member[b]
            gscore = pair if gscore is None else jnp.maximum(gscore, pair)
    gidx = lax.broadcasted_iota(jnp.int32, (ng, tm), 0)
    gmax = jnp.max(gscore, axis=0, keepdims=True)
    chosen = gidx == jnp.min(jnp.where(gscore == gmax, gidx, ng), axis=0, keepdims=True)
    zero_g = jnp.zeros((ng, tm), F32)
    e1, e2, p1, p2 = zero_g, zero_g, zero_g, zero_g
    firsts, seconds = [], []
    for k in range(epg):
        rank = zero_g
        for o in range(epg):
            if o != k:
                beats = (member[o] >= member[k]) if o < k else (member[o] > member[k])
                rank = rank + jnp.where(beats, 1.0, 0.0)
        is1 = chosen & (rank == 0.0)
        is2 = chosen & (rank == 1.0)
        firsts.append(jnp.where(is1, 1.0, 0.0))
        seconds.append(jnp.where(is2, 1.0, 0.0))
        eid = (gidx * epg + k).astype(F32)
        e1 = e1 + jnp.where(is1, eid, 0.0)
        e2 = e2 + jnp.where(is2, eid, 0.0)
        p1 = p1 + jnp.where(is1, pmember[k], 0.0)
        p2 = p2 + jnp.where(is2, pmember[k], 0.0)
    e1, e2, p1, p2 = [jnp.sum(v, axis=0, keepdims=True) for v in (e1, e2, p1, p2)]
    first = jnp.concatenate(firsts, axis=0)
    second = jnp.concatenate(seconds, axis=0)
    mask = first + second
    prefix = jnp.dot(mask.astype(BF16), tri_ref[...], preferred_element_type=F32)
    counts = jnp.sum(mask, axis=1, keepdims=True)
    padded = jnp.floor((counts + (RUN_ALIGN - 1)) * (1.0 / RUN_ALIGN)) * RUN_ALIGN
    row_of = [(e % epg) * ng + e // epg for e in range(E)]
    starts = [None] * E
    acc = jnp.zeros((1, 1), F32)
    for e in range(E):
        starts[row_of[e]] = acc
        acc = acc + padded[row_of[e]:row_of[e] + 1, :]
    slot = prefix + jnp.concatenate(starts, axis=0)
    s1 = jnp.sum(first * slot, axis=0, keepdims=True)
    s2 = jnp.sum(second * slot, axis=0, keepdims=True)
    psum = p1 + p2
    zero = jnp.zeros((1, tm), F32)
    out_ref[...] = jnp.concatenate([e1, e2, p1 / psum, p2 / psum, s1, s2, zero, zero], axis=0)
    counts_by_expert = jnp.concatenate([counts[row_of[e]:row_of[e] + 1, :] for e in range(E)], axis=0)
    cnt_ref[...] = jnp.broadcast_to(counts_by_expert, cnt_ref.shape)


def _route(x2, shift, scale, w_router, b_router, S):
    T, D = x2.shape
    E = w_router.shape[1]
    tm = min(MOE_TILE, S)
    tpb = S // tm
    tri = (jnp.arange(tm)[:, None] < jnp.arange(tm)[None, :]).astype(BF16)
    epg = E // N_GROUPS
    rows = jnp.array([g * epg + k for k in range(epg) for g in range(N_GROUPS)], jnp.int32)
    return pl.pallas_call(
        _route_kernel,
        grid=(T // tm,),
        in_specs=[pl.BlockSpec((tm, D), lambda i: (i, 0)),
                  pl.BlockSpec((1, 1, D), lambda i: (i // tpb, 0, 0)),
                  pl.BlockSpec((1, 1, D), lambda i: (i // tpb, 0, 0)),
                  pl.BlockSpec((E, D), lambda i: (0, 0)),
                  pl.BlockSpec((E, 1), lambda i: (0, 0)),
                  pl.BlockSpec((tm, tm), lambda i: (0, 0))],
        out_specs=[pl.BlockSpec((8, tm), lambda i: (0, i)),
                   pl.BlockSpec((E, LANES), lambda i: (0, i))],
        out_shape=[jax.ShapeDtypeStruct((8, T), F32), jax.ShapeDtypeStruct((E, (T // tm) * LANES), F32)],
        compiler_params=_params(("parallel",)),
        name="moe_route",
    )(x2, shift, scale, w_router.T[rows], b_router[rows].reshape(E, 1), tri)


def _stage_rows(tm, n_exp):
    return MOE_TOPK * tm + n_exp * RUN_ALIGN


def _chunk_copies(tc_ref, ts_ref, ds_ref, step, n_exp, make):
    out = []
    for e in range(n_exp):
        cnt = tc_ref[step * n_exp + e]
        off = ts_ref[step * n_exp + e]
        dst = ds_ref[step * n_exp + e]
        for k in reversed(range(RUN_ALIGN.bit_length() - 1, CHUNK_BITS)):
            done = (cnt >> (k + 1)) << (k + 1)
            out.append(((cnt & (1 << k)) != 0,
                        make(pl.multiple_of(off + done, RUN_ALIGN), pl.multiple_of(dst + done, RUN_ALIGN), 1 << k)))
    return out


def _dispatch_kernel(tc_ref, ts_ref, ds_ref, pe_ref, x_ref, sh_ref, sc_ref, pos_ref, xs_ref,
                     stage, zbuf, sem, zsem):
    tm = x_ref.shape[0]
    rb = zbuf.shape[0]
    i = pl.program_id(0)
    slot = i % 2

    n_exp = pe_ref.shape[0] - 1
    n_blocks = xs_ref.shape[0] // rb

    def pad_copy(e):
        return pltpu.make_async_copy(zbuf, xs_ref.at[pl.ds(pl.multiple_of(pe_ref[e + 1] - rb, rb), rb)], zsem)

    def tail_copy(j):
        return pltpu.make_async_copy(zbuf, xs_ref.at[pl.ds(pl.multiple_of(pe_ref[n_exp] + j * rb, rb), rb)], zsem)

    def tail_live(j):
        return pe_ref[n_exp] // rb + j < n_blocks

    @pl.when(i == 0)
    def _():
        zbuf[...] = jnp.zeros_like(zbuf)
        for e in range(n_exp):
            @pl.when(pe_ref[e + 1] > pe_ref[e])
            def _():
                pad_copy(e).start()
        for j in range(n_exp):
            @pl.when(tail_live(j))
            def _():
                tail_copy(j).start()
        for e in range(n_exp):
            @pl.when(pe_ref[e + 1] > pe_ref[e])
            def _():
                pad_copy(e).wait()
        for j in range(n_exp):
            @pl.when(tail_live(j))
            def _():
                tail_copy(j).wait()

    h = (x_ref[...] * (1.0 + sc_ref[0]) + sh_ref[0]).astype(BF16)
    pos = pos_ref[0]
    slot_id = lax.broadcasted_iota(jnp.int32, (stage.shape[1], tm), 0)
    perm = jnp.where((slot_id == pos[0:1, :]) | (slot_id == pos[1:2, :]), 1.0, 0.0).astype(BF16)
    stage[slot] = jnp.dot(perm, h, preferred_element_type=F32)

    def copies(step, s):
        def make(src_row, dst_row, size):
            return pltpu.make_async_copy(stage.at[s, pl.ds(src_row, size)], xs_ref.at[pl.ds(dst_row, size)], sem.at[s])
        return _chunk_copies(tc_ref, ts_ref, ds_ref, step, n_exp, make)

    for live, cp in copies(i, slot):
        pl.when(live)(cp.start)
    for live, cp in copies(jnp.maximum(i - 1, 0), 1 - slot):
        pl.when(live & (i > 0))(cp.wait)
    for live, cp in copies(i, slot):
        pl.when(live & (i == pl.num_programs(0) - 1))(cp.wait)


def _dispatch(x2, shift, scale, posrow, tcnt, tstart, dstart, pends0, cap, S):
    T, D = x2.shape
    tm = min(MOE_TILE, S)
    tpb = S // tm
    mod = pl.BlockSpec((1, 1, D), lambda i, *_: (i // tpb, 0, 0))
    grid_spec = pltpu.PrefetchScalarGridSpec(
        num_scalar_prefetch=4,
        grid=(T // tm,),
        in_specs=[pl.BlockSpec((tm, D), lambda i, *_: (i, 0)), mod, mod,
                  pl.BlockSpec((1, 2, tm), lambda i, *_: (i, 0, 0))],
        out_specs=pl.BlockSpec(memory_space=pl.ANY),
        scratch_shapes=[pltpu.VMEM((2, _stage_rows(tm, pends0.shape[0] - 1), D), F32),
                        pltpu.VMEM((MOE_ROW_BLOCK, D), F32),
                        pltpu.SemaphoreType.DMA((2,)), pltpu.SemaphoreType.DMA(())],
    )
    return pl.pallas_call(
        _dispatch_kernel,
        grid_spec=grid_spec,
        out_shape=jax.ShapeDtypeStruct((cap, D), F32),
        compiler_params=_params(("arbitrary",)),
        name="moe_dispatch",
    )(tcnt, tstart, dstart, pends0, x2, shift, scale, posrow)


def _expert_kernel(be_ref, na_ref, xs_ref, wg_ref, wu_ref, wd_ref, ys_ref, wg_b, wu_b, wd_b):
    i = pl.program_id(0)
    active = i < na_ref[0]

    @pl.when(active & ((i == 0) | (be_ref[i] != be_ref[jnp.maximum(i - 1, 0)])))
    def _():
        wg_b[...] = wg_ref[0, 0].astype(BF16)
        wu_b[...] = wu_ref[0, 0].astype(BF16)
        wd_b[...] = wd_ref[0, 0].astype(BF16)

    @pl.when(active)
    def _():
        xb = xs_ref[...].astype(BF16)
        g = jnp.dot(xb, wg_b[...], preferred_element_type=F32)
        u = jnp.dot(xb, wu_b[...], preferred_element_type=F32)
        a = (g * _sigmoid(g)) * u
        ys_ref[...] = jnp.dot(a.astype(BF16), wd_b[...], preferred_element_type=F32)

    @pl.when(jnp.logical_not(active))
    def _():
        ys_ref[...] = jnp.zeros_like(ys_ref)


def _experts(xs, block_expert, n_active, w_gate, w_up, w_down, layer):
    cap, D = xs.shape
    _, E, _, F = w_gate.shape
    rb = MOE_ROW_BLOCK
    nb = cap // rb

    def blk(i, be, na):
        return jnp.maximum(jnp.minimum(i, na[0] - 1), 0)

    grid_spec = pltpu.PrefetchScalarGridSpec(
        num_scalar_prefetch=2,
        grid=(nb,),
        in_specs=[pl.BlockSpec((rb, D), lambda i, be, na: (blk(i, be, na), 0)),
                  pl.BlockSpec((1, 1, D, F), lambda i, be, na: (layer, be[blk(i, be, na)], 0, 0)),
                  pl.BlockSpec((1, 1, D, F), lambda i, be, na: (layer, be[blk(i, be, na)], 0, 0)),
                  pl.BlockSpec((1, 1, F, D), lambda i, be, na: (layer, be[blk(i, be, na)], 0, 0))],
        out_specs=pl.BlockSpec((rb, D), lambda i, be, na: (i, 0)),
        scratch_shapes=[pltpu.VMEM((D, F), BF16), pltpu.VMEM((D, F), BF16), pltpu.VMEM((F, D), BF16)],
    )
    return pl.pallas_call(
        _expert_kernel,
        grid_spec=grid_spec,
        out_shape=jax.ShapeDtypeStruct((cap, D), F32),
        compiler_params=_params(("arbitrary",)),
        name="moe_experts",
    )(block_expert, n_active, xs, w_gate, w_up, w_down)


def _combine_kernel(tc_ref, ts_ref, ds_ref, ys_ref, rt_ref, x_ref, gate_ref, lg_ref, lb_ref, o_ref, stage, sem,
                    *, n_exp):
    tm = x_ref.shape[0]
    i = pl.program_id(0)
    slot = i % 2

    def copies(step, s):
        def make(stage_row, ys_row, size):
            return pltpu.make_async_copy(ys_ref.at[pl.ds(ys_row, size)], stage.at[s, pl.ds(stage_row, size)], sem.at[s])
        return _chunk_copies(tc_ref, ts_ref, ds_ref, step, n_exp, make)

    @pl.when(i == 0)
    def _():
        stage[...] = jnp.zeros_like(stage)

    for live, cp in copies(i, slot):
        pl.when(live & (i == 0))(cp.start)
    nxt = jnp.minimum(i + 1, pl.num_programs(0) - 1)
    for live, cp in copies(nxt, 1 - slot):
        pl.when(live & (i + 1 < pl.num_programs(0)))(cp.start)
    for live, cp in copies(i, slot):
        pl.when(live)(cp.wait)

    rt = rt_ref[...]
    ysb = stage[slot].astype(BF16)
    slot_id = lax.broadcasted_iota(jnp.int32, (tm, stage.shape[1]), 1).astype(F32)
    y = None
    for j in range(MOE_TOPK):
        perm = jnp.where(slot_id == rt[:, 4 + j:5 + j], 1.0, 0.0).astype(BF16)
        part = rt[:, 2 + j:3 + j] * jnp.dot(perm, ysb, preferred_element_type=F32)
        y = part if y is None else y + part
    r = DEEPNORM_ALPHA * x_ref[...] + gate_ref[0] * y
    o_ref[...] = _layer_norm(r, lg_ref[...], lb_ref[...])


def _combine(ys, route_t, x2, gate, ln_g, ln_b, tcnt, tstart, dstart, S):
    T, D = x2.shape
    tm = min(MOE_TILE, S)
    tpb = S // tm
    grid_spec = pltpu.PrefetchScalarGridSpec(
        num_scalar_prefetch=3,
        grid=(T // tm,),
        in_specs=[pl.BlockSpec(memory_space=pl.ANY),
                  pl.BlockSpec((tm, 8), lambda i, *_: (i, 0)),
                  pl.BlockSpec((tm, D), lambda i, *_: (i, 0)),
                  pl.BlockSpec((1, 1, D), lambda i, *_: (i // tpb, 0, 0)),
                  pl.BlockSpec((1, D), lambda i, *_: (0, 0)),
                  pl.BlockSpec((1, D), lambda i, *_: (0, 0))],
        out_specs=pl.BlockSpec((tm, D), lambda i, *_: (i, 0)),
        scratch_shapes=[pltpu.VMEM((2, _stage_rows(tm, tcnt.shape[0] // (T // tm)), D), F32),
                        pltpu.SemaphoreType.DMA((2,))],
    )
    return pl.pallas_call(
        functools.partial(_combine_kernel, n_exp=tcnt.shape[0] // (T // tm)),
        grid_spec=grid_spec,
        out_shape=jax.ShapeDtypeStruct((T, D), F32),
        compiler_params=_params(("arbitrary",)),
        name="moe_combine_ln",
    )(tcnt, tstart, dstart, ys, route_t, x2, gate, ln_g.reshape(1, D), ln_b.reshape(1, D))


def _moe_layer(x2, shift, scale, gate, ln_g, ln_b, w_router, b_router, w_gate, w_up, w_down, layer, S):
    T, D = x2.shape
    E = w_router.shape[1]
    rb = MOE_ROW_BLOCK
    tm = min(MOE_TILE, S)
    nt = T // tm
    assert tm < (1 << CHUNK_BITS)
    route, cnt = _route(x2, shift, scale, w_router, b_router, S)
    tcnt = cnt.reshape(E, nt, LANES)[:, :, 0].T.astype(jnp.int32)
    tcnt = (tcnt + RUN_ALIGN - 1) // RUN_ALIGN * RUN_ALIGN
    base = jnp.cumsum(tcnt, axis=0) - tcnt
    counts = jnp.sum(tcnt, axis=0)
    padded = (counts + rb - 1) // rb * rb
    pends = jnp.cumsum(padded)
    pstarts = pends - padded
    tstart = jnp.cumsum(tcnt, axis=1) - tcnt
    dstart = pstarts[None, :] + base
    posrow = jnp.transpose(route[4:6].astype(jnp.int32).reshape(2, nt, tm), (1, 0, 2))
    nb = -(-(T * MOE_TOPK + nt * E * (RUN_ALIGN - 1)) // rb) + E
    cap = nb * rb
    block_start = jnp.arange(nb, dtype=jnp.int32) * rb
    block_expert = jnp.minimum(jnp.sum(block_start[:, None] >= pends[None, :], axis=1), E - 1).astype(jnp.int32)
    n_active = (pends[-1:] // rb).astype(jnp.int32)
    pends0 = jnp.concatenate([jnp.zeros((1,), jnp.int32), pends.astype(jnp.int32)])
    tables = [t.reshape(-1).astype(jnp.int32) for t in (tcnt, tstart, dstart)]
    xs = _dispatch(x2, shift, scale, posrow, *tables, pends0, cap, S)
    ys = _experts(xs, block_expert, n_active, w_gate, w_up, w_down, layer)
    return _combine(ys, route.T, x2, gate, ln_g, ln_b, *tables, S)


def _moba_proj_kernel(x_ref, shq_ref, scq_ref, shkv_ref, sckv_ref, wq_ref, wkv_ref,
                      q_ref, k_ref, vt_ref, km_ref, *, qscale, tpb, slopes):
    tm, D = x_ref.shape
    hd = HEAD_DIM
    nh = D // hd
    x = x_ref[...]
    hq = x * (1.0 + scq_ref[0]) + shq_ref[0]
    hkv = x * (1.0 + sckv_ref[0]) + shkv_ref[0]
    qf = jnp.dot(hq.astype(BF16), wq_ref[...], preferred_element_type=F32) * qscale
    kv = jnp.dot(hkv.astype(BF16), wkv_ref[...], preferred_element_type=F32)
    kf = kv[:, :D]
    vt = kv[:, D:].T
    ones_pad = jnp.where(lax.broadcasted_iota(jnp.int32, (V_ROWS - hd, tm), 0) == 0, 1.0, 0.0)
    for g in range(nh):
        vt_ref[0, g * V_ROWS:(g + 1) * V_ROWS, :] = jnp.concatenate(
            [vt[g * hd:(g + 1) * hd, :], ones_pad], axis=0).astype(BF16)
    lane = lax.broadcasted_iota(jnp.int32, (1, LANES), 1)
    head_lanes = lane >= hd
    pos = (pl.program_id(0) % tpb) * tm + lax.broadcasted_iota(jnp.int32, (tm, 1), 0)
    pos_f = pos.astype(F32)
    onehot = jnp.where(lane == pos // MOBA_BLOCK, 1.0, 0.0)
    q_fill = jnp.where((lane >= BIAS_LANE) & (lane < BIAS_LANE + 3), 1.0, 0.0)
    nbt = tm // MOBA_BLOCK
    means = [jnp.mean(kf[j * MOBA_BLOCK:(j + 1) * MOBA_BLOCK, :], axis=0, keepdims=True) for j in range(nbt)]
    km = jnp.concatenate(means + [jnp.zeros((8 - nbt, D), F32)], axis=0)
    for j in range(nh // 2):
        cs = slice(j * LANES, (j + 1) * LANES)
        qb, kb, mb = qf[:, cs], kf[:, cs], km[:, cs]
        q_pair = (pltpu.roll(qb, hd, axis=1), qb)
        k_pair = (pltpu.roll(kb, hd, axis=1), kb)
        m_pair = (pltpu.roll(mb, hd, axis=1), mb)
        for e in range(2):
            g = 2 * j + e
            gs = slice(g * LANES, (g + 1) * LANES)
            q_ref[:, gs] = jnp.where(head_lanes, q_pair[e], q_fill).astype(BF16)
            c = (slopes[g] * LOG2E) * pos_f
            hi = c.astype(BF16).astype(F32)
            mid = (c - hi).astype(BF16).astype(F32)
            lo = c - hi - mid
            aug = jnp.where(lane == BIAS_LANE, hi,
                            jnp.where(lane == BIAS_LANE + 1, mid,
                                      jnp.where(lane == BIAS_LANE + 2, lo, onehot)))
            k_ref[:, gs] = jnp.where(head_lanes, k_pair[e], aug).astype(BF16)
            mg = jnp.where(head_lanes, m_pair[e], 0.0)
            for jb in range(nbt):
                km_ref[jb, :, gs] = mg[jb:jb + 1, :]


def _moba_proj(x2, shq, scq, shkv, sckv, wq, wkv, B, S):
    T, D = x2.shape
    nh = D // HEAD_DIM
    tm = min(512, S)
    tpb = S // tm
    nbt = tm // MOBA_BLOCK
    assert S // MOBA_BLOCK <= BIAS_LANE and nbt <= 8
    slopes = tuple(2.0 ** (-8.0 * (h + 1.0) / nh) for h in range(nh))
    mod = pl.BlockSpec((1, 1, D), lambda i: (i // tpb, 0, 0))
    kern = functools.partial(_moba_proj_kernel, qscale=float(HEAD_DIM) ** -0.5 * LOG2E, tpb=tpb, slopes=slopes)
    return pl.pallas_call(
        kern,
        grid=(T // tm,),
        in_specs=[pl.BlockSpec((tm, D), lambda i: (i, 0)), mod, mod, mod, mod,
                  pl.BlockSpec((D, D), lambda i: (0, 0)),
                  pl.BlockSpec((D, 2 * D), lambda i: (0, 0))],
        out_specs=[pl.BlockSpec((tm, nh * LANES), lambda i: (i, 0)),
                   pl.BlockSpec((tm, nh * LANES), lambda i: (i, 0)),
                   pl.BlockSpec((1, nh * V_ROWS, tm), lambda i: (i // tpb, 0, i % tpb)),
                   pl.BlockSpec((nbt, 1, nh * LANES), lambda i: (i, 0, 0))],
        out_shape=[jax.ShapeDtypeStruct((T, nh * LANES), BF16), jax.ShapeDtypeStruct((T, nh * LANES), BF16),
                   jax.ShapeDtypeStruct((B, nh * V_ROWS, S), BF16),
                   jax.ShapeDtypeStruct((T // MOBA_BLOCK, 1, nh * LANES), F32)],
        compiler_params=_params(("parallel",)),
        name="moba_proj",
    )(x2, shq, scq, shkv, sckv, wq, wkv)


def _moba_select_kernel(q_ref, km_ref, qat_ref):
    tq = q_ref.shape[0]
    G = q_ref.shape[1] // LANES
    nblk = km_ref.shape[1]
    blk_id = lax.broadcasted_iota(jnp.int32, (nblk, tq), 0)
    own = (pl.program_id(2) * tq + lax.broadcasted_iota(jnp.int32, (1, tq), 1)) // MOBA_BLOCK
    for g in range(G):
        gs = slice(g * LANES, (g + 1) * LANES)
        qgt = q_ref[:, gs].astype(F32).T
        qb = qgt.astype(BF16)
        km = km_ref[0, :, gs]
        km_hi = km.astype(BF16)
        km_mid = (km - km_hi.astype(F32)).astype(BF16)
        km_lo = (km - km_hi.astype(F32) - km_mid.astype(F32)).astype(BF16)
        gate = (jnp.dot(km_hi, qb, preferred_element_type=F32) + jnp.dot(km_mid, qb, preferred_element_type=F32)
                + jnp.dot(km_lo, qb, preferred_element_type=F32))
        gsc = jnp.where(blk_id < own, gate, -jnp.inf)
        picked = blk_id < 0
        for _ in range(MOBA_TOPK):
            mx = jnp.max(gsc, axis=0, keepdims=True)
            first = jnp.min(jnp.where(gsc == mx, blk_id, nblk), axis=0, keepdims=True)
            pick = blk_id == first
            picked = picked | pick
            gsc = jnp.where(pick, -jnp.inf, gsc)
        visible = (picked & (blk_id < own)) | (blk_id == own)
        pen = jnp.where(visible, 0.0, NEG)
        qat_ref[0, g] = (qgt + jnp.concatenate([pen, jnp.zeros((LANES - nblk, tq), F32)], axis=0)).astype(BF16)


def _moba_select(q, km, B, S):
    nh = q.shape[1] // LANES
    G = MOBA_HEADS_PER_STEP
    tq = min(512, S)
    nblk = S // MOBA_BLOCK
    return pl.pallas_call(
        _moba_select_kernel,
        grid=(B, nh // G, S // tq),
        in_specs=[pl.BlockSpec((tq, G * LANES), lambda b, hp, i: (b * (S // tq) + i, hp)),
                  pl.BlockSpec((1, nblk, G * LANES), lambda b, hp, i: (b, 0, hp))],
        out_specs=pl.BlockSpec((1, G, LANES, tq), lambda b, hp, i: (b, hp, 0, i)),
        out_shape=jax.ShapeDtypeStruct((B, nh, LANES, S), BF16),
        compiler_params=_params(("parallel", "parallel", "parallel")),
        name="moba_select",
    )(q, km)


def _moba_kernel(qat_ref, k_ref, vt_ref, o_ref, st_scr, m_scr, acc_scr):
    G = qat_ref.shape[1]
    BLK = qat_ref.shape[3]
    hd = HEAD_DIM
    nhalf = BLK // LANES
    own = pl.program_id(2)
    k_off = lax.broadcasted_iota(jnp.int32, (BLK, 1), 0)
    q_off = lax.broadcasted_iota(jnp.int32, (1, LANES), 1)

    m_scr[...] = jnp.full(m_scr.shape, NEG, F32)
    acc_scr[...] = jnp.zeros_like(acc_scr)

    def scores(n, slot):
        start = pl.multiple_of(n * BLK, BLK)
        for g in range(G):
            kn = k_ref[pl.ds(start, BLK), g * LANES:(g + 1) * LANES]
            st_scr[slot, g] = jnp.dot(kn, qat_ref[0, g], preferred_element_type=F32)

    def update(n, slot, causal):
        start = pl.multiple_of(n * BLK, BLK)
        for g in range(G):
            alphas, ps = [], []
            for hf in range(nhalf):
                c = g * nhalf + hf
                ls = slice(hf * LANES, (hf + 1) * LANES)
                st = st_scr[slot, g, :, ls]
                if causal:
                    st = jnp.where(k_off <= q_off + hf * LANES, st, NEG)
                m = m_scr[c]
                m_new = jnp.maximum(m, jnp.max(st, axis=0, keepdims=True))
                alphas.append(jnp.exp2(m - m_new))
                ps.append(jnp.exp2(st - m_new).astype(BF16))
                m_scr[c] = m_new
            vtn = vt_ref[0, g * V_ROWS:(g + 1) * V_ROWS, pl.ds(start, BLK)]
            pv = jnp.dot(vtn, jnp.concatenate(ps, axis=1), preferred_element_type=F32)
            acc_scr[g] = jnp.concatenate(alphas, axis=1) * acc_scr[g] + pv

    scores(0, 0)

    def pair(n):
        scores(n + 1, 1)
        update(n, 0, False)
        scores(n + 2, 0)
        update(n + 1, 1, False)

    def two_pairs(i, carry):
        pair(4 * i)
        pair(4 * i + 2)
        return carry

    def one_pair(i, carry):
        pair(4 * (own // 4) + 2 * i)
        return carry

    lax.fori_loop(0, own // 4, two_pairs, 0)
    lax.fori_loop(0, (own % 4) // 2, one_pair, 0)

    @pl.when(own % 2 == 0)
    def _():
        update(own, 0, True)

    @pl.when(own % 2 == 1)
    def _():
        scores(own, 1)
        update(own - 1, 0, False)
        update(own, 1, True)
    for g in range(G):
        acc = acc_scr[g]
        o_ref[0, g * hd:(g + 1) * hd, :] = (acc[:hd, :] * (1.0 / acc[hd:hd + 1, :])).astype(BF16)


def _moba_attn(qat, k, vt, B, S):
    nh = qat.shape[1]
    G = MOBA_HEADS_PER_STEP
    BLK = MOBA_BLOCK
    nblk = S // BLK
    hd = HEAD_DIM
    return pl.pallas_call(
        _moba_kernel,
        grid=(B, nh // G, nblk),
        in_specs=[pl.BlockSpec((1, G, LANES, BLK), lambda b, hp, qi: (b, hp, 0, qi)),
                  pl.BlockSpec((S, G * LANES), lambda b, hp, qi: (b, hp)),
                  pl.BlockSpec((1, G * V_ROWS, S), lambda b, hp, qi: (b, hp, 0))],
        out_specs=pl.BlockSpec((1, G * hd, BLK), lambda b, hp, qi: (b, hp, qi)),
        out_shape=jax.ShapeDtypeStruct((B, nh * hd, S), BF16),
        scratch_shapes=[pltpu.VMEM((2, G, BLK, BLK), F32),
                        pltpu.VMEM((G * (BLK // LANES), 1, LANES), F32),
                        pltpu.VMEM((G, V_ROWS, BLK), F32)],
        compiler_params=_params(("parallel", "parallel", "arbitrary")),
        name="moba_attn",
    )(qat, k, vt)


def _attn_out_kernel(a_ref, x_ref, w_ref, gate_ref, lg_ref, lb_ref, o_ref):
    y = lax.dot_general(a_ref[0], w_ref[...], (((0,), (0,)), ((), ())), preferred_element_type=F32)
    r = DEEPNORM_ALPHA * x_ref[...] + gate_ref[0] * y
    o_ref[...] = _layer_norm(r, lg_ref[...], lb_ref[...])


def _attn_out(at, x2, w, gate, ln_g, ln_b, S):
    T, D = x2.shape
    K = at.shape[1]
    tm = min(512, S)
    tpb = S // tm
    row = lambda i: (i, 0)
    full2 = lambda i: (0, 0)
    return pl.pallas_call(
        _attn_out_kernel,
        grid=(T // tm,),
        in_specs=[pl.BlockSpec((1, K, tm), lambda i: (i // tpb, 0, i % tpb)), pl.BlockSpec((tm, D), row),
                  pl.BlockSpec((K, D), full2),
                  pl.BlockSpec((1, 1, D), lambda i: (i // tpb, 0, 0)),
                  pl.BlockSpec((1, D), full2), pl.BlockSpec((1, D), full2)],
        out_specs=pl.BlockSpec((tm, D), row),
        out_shape=jax.ShapeDtypeStruct((T, D), F32),
        compiler_params=_params(("parallel",)),
        name="attn_out_ln",
    )(at, x2, w, gate, ln_g.reshape(1, D), ln_b.reshape(1, D))


def kernel(x, c, w_ada, b_ada, ln_g, ln_b, a_w_in, a_conv_w, a_conv_b, a_wq, a_wk, a_wv, a_w_if, a_b_if,
           a_gn_g, a_skip, a_w_out, b_w_kv, b_wq, b_wo, moe_w_router, moe_b_router, moe_w_gate, moe_w_up,
           moe_w_down):
    B, S, D = x.shape
    T = B * S
    assert DEPTH == 2 and S % MOBA_BLOCK == 0 and S % MLSTM_CHUNK == 0
    n_layer_mod = DEPTH * N_MOD_PER_LAYER * D
    cond = _ada_cond(c, w_ada, b_ada)
    mods = cond[:, :n_layer_mod].reshape(B, DEPTH, 2, 3, 1, D)
    kv_mod = cond[:, n_layer_mod:].reshape(B, 2, 1, D)

    def mod3(layer, sub):
        m = mods[:, layer, sub]
        return m[:, 0], m[:, 1], 1.0 + m[:, 2]

    xf = x.reshape(T, D)

    shift, scale, gate = mod3(0, 0)
    nh = MLSTM_HEADS
    xm, z = _inproj(xf, shift, scale, a_w_in[0].astype(BF16), S)
    q, k, v, xc, gts = _qkv(xm, a_conv_w[0], a_conv_b[0], a_wq[0].astype(BF16), a_wk[0].astype(BF16),
                            a_wv[0].astype(BF16), a_w_if[0], a_b_if[0], S)
    g4 = gts.reshape(B, S, 2, nh)
    gcol = jnp.transpose(g4, (0, 3, 1, 2))
    grow = jnp.transpose(g4, (0, 3, 2, 1))
    hn = _mlstm(q, k, v, gcol, grow, B, S)
    xf = _mlstm_out(hn, xc, z, xf, a_gn_g[0], a_skip[0], a_w_out[0].astype(BF16), gate, ln_g[0, 0], ln_b[0, 0], S)
    shift, scale, gate = mod3(0, 1)
    xf = _moe_layer(xf, shift, scale, gate, ln_g[0, 1], ln_b[0, 1], moe_w_router, moe_b_router,
                    moe_w_gate, moe_w_up, moe_w_down, 0, S)

    shift, scale, gate = mod3(1, 0)
    q, k, vt, km = _moba_proj(xf, shift, scale, kv_mod[:, 0], kv_mod[:, 1], b_wq[0].astype(BF16),
                              b_w_kv.astype(BF16), B, S)
    km = km.reshape(B, S // MOBA_BLOCK, km.shape[-1])
    attn_t = _moba_attn(_moba_select(q, km, B, S), k, vt, B, S)
    xf = _attn_out(attn_t, xf, b_wo[0].astype(BF16), gate, ln_g[1, 0], ln_b[1, 0], S)
    shift, scale, gate = mod3(1, 1)
    xf = _moe_layer(xf, shift, scale, gate, ln_g[1, 1], ln_b[1, 1], moe_w_router, moe_b_router,
                    moe_w_gate, moe_w_up, moe_w_down, 1, S)
    return xf.reshape(B, S, D)
```

```python
import functools

import jax
import jax.numpy as jnp
from jax import lax
from jax.experimental import pallas as pl
from jax.experimental.pallas import tpu as pltpu

DEPTH = 2
MLSTM_HEADS = 4
CONV_WIDTH = 4
MLSTM_CHUNK = 512
ATTN_HEADS = 16
MOBA_BLOCK = 256
MOBA_TOPK = 3
N_EXPERTS = 16
N_GROUPS = 4
MOE_TOPK = 2
MOE_ROW_BLOCK = 256
MOE_TILE = 512
CHUNK_BITS = 10
RUN_ALIGN = 8
DEEPNORM_ALPHA = (2.0 * DEPTH) ** 0.25
LN_EPS = 1e-5
N_MOD_PER_LAYER = 6

HEAD_DIM = 64
V_ROWS = 80
MOBA_HEADS_PER_STEP = 4
BIAS_LANE = 32
LOG2E = 1.4426950408889634

LANES = 128
CONV_HALO = 16
NEG = -1e30
VMEM_LIMIT = 56 * 1024 * 1024

F32 = jnp.float32
BF16 = jnp.bfloat16


def _sigmoid(x):
    return 1.0 / (1.0 + jnp.exp(-x))


def _params(sem, vmem=VMEM_LIMIT):
    return pltpu.CompilerParams(dimension_semantics=sem, vmem_limit_bytes=vmem)


def _layer_norm(r, g, b):
    mu = jnp.mean(r, axis=-1, keepdims=True)
    d = r - mu
    var = jnp.mean(d * d, axis=-1, keepdims=True)
    return d * lax.rsqrt(var + LN_EPS) * g + b


def _ada_kernel(c_ref, w_ref, b_ref, o_ref):
    c = c_ref[...]
    s = c * _sigmoid(c)
    o_ref[...] = jnp.dot(s.astype(BF16), w_ref[...].astype(BF16),
                         preferred_element_type=F32) + b_ref[...]


def _ada_cond(c, w_ada, b_ada):
    B, D = c.shape
    N = w_ada.shape[1]
    tn = 2048 if N % 2048 == 0 else N
    cp = jnp.zeros((8, D), F32).at[:B].set(c)
    out = pl.pallas_call(
        _ada_kernel,
        grid=(N // tn,),
        in_specs=[pl.BlockSpec((8, D), lambda j: (0, 0)),
                  pl.BlockSpec((D, tn), lambda j: (0, j)),
                  pl.BlockSpec((1, tn), lambda j: (0, j))],
        out_specs=pl.BlockSpec((8, tn), lambda j: (0, j)),
        out_shape=jax.ShapeDtypeStruct((8, N), F32),
        compiler_params=_params(("arbitrary",)),
        name="ada_cond",
    )(cp, w_ada, b_ada.reshape(1, N))
    return out[:B]


def _inproj_kernel(x_ref, sh_ref, sc_ref, w_ref, xm_ref, z_ref):
    di = xm_ref.shape[-1]
    h = x_ref[...] * (1.0 + sc_ref[0]) + sh_ref[0]
    r = jnp.dot(h.astype(BF16), w_ref[...], preferred_element_type=F32)
    xm_ref[...] = r[:, :di].astype(BF16)
    z_ref[...] = r[:, di:].astype(BF16)


def _inproj(x2, shift, scale, w_in, S):
    T, D = x2.shape
    di = w_in.shape[1] // 2
    tm = min(512, S)
    tpb = S // tm
    return pl.pallas_call(
        _inproj_kernel,
        grid=(T // tm,),
        in_specs=[pl.BlockSpec((tm, D), lambda i: (i, 0)),
                  pl.BlockSpec((1, 1, D), lambda i: (i // tpb, 0, 0)),
                  pl.BlockSpec((1, 1, D), lambda i: (i // tpb, 0, 0)),
                  pl.BlockSpec((D, 2 * di), lambda i: (0, 0))],
        out_specs=[pl.BlockSpec((tm, di), lambda i: (i, 0)),
                   pl.BlockSpec((tm, di), lambda i: (i, 0))],
        out_shape=[jax.ShapeDtypeStruct((T, di), BF16)] * 2,
        compiler_params=_params(("parallel",)),
        name="mlstm_inproj",
    )(x2, shift, scale, w_in)


def _qkv_kernel(xm_ref, halo_ref, cw_ref, cb_ref, wq_ref, wk_ref, wv_ref, wif_ref, bif_ref,
                q_ref, k_ref, v_ref, xc_ref, g_ref, *, tpb, nh, kscale):
    tm, di = xm_ref.shape
    dh = di // nh
    i = pl.program_id(0)
    xm = xm_ref[...]
    xf = xm.astype(F32)
    halo = halo_ref[...].astype(F32)
    halo = jnp.where(i % tpb == 0, 0.0, halo)
    ext = jnp.concatenate([halo, xf], axis=0)
    acc = cb_ref[...] + cw_ref[CONV_WIDTH - 1:CONV_WIDTH, :] * xf
    for s in range(1, CONV_WIDTH):
        acc = acc + cw_ref[CONV_WIDTH - 1 - s:CONV_WIDTH - s, :] * ext[CONV_HALO - s:CONV_HALO - s + tm, :]
    xc = acc * _sigmoid(acc)
    xcb = xc.astype(BF16)
    xc_ref[...] = xcb
    for h in range(nh):
        sl = slice(h * dh, (h + 1) * dh)
        q_ref[:, sl] = jnp.dot(xcb[:, sl], wq_ref[h], preferred_element_type=F32).astype(BF16)
        k_ref[:, sl] = (jnp.dot(xcb[:, sl], wk_ref[h], preferred_element_type=F32) * kscale).astype(BF16)
        v_ref[:, sl] = jnp.dot(xm[:, sl], wv_ref[h], preferred_element_type=F32).astype(BF16)
    g = jnp.dot(xcb, wif_ref[...], preferred_element_type=F32) + bif_ref[...]
    col = lax.broadcasted_iota(jnp.int32, g.shape, 1)
    logsig = jnp.minimum(g, 0.0) - jnp.log(1.0 + jnp.exp(-jnp.abs(g)))
    g = jnp.where(col >= nh, logsig, g)
    g_ref[...] = g[:, :2 * nh]


def _qkv(xm, conv_w, conv_b, wq, wk, wv, w_if, b_if, S):
    T, di = xm.shape
    nh = wq.shape[0]
    dh = di // nh
    tm = min(512, S)
    tpb = S // tm
    hb = tm // CONV_HALO
    wif = jnp.zeros((di, LANES), BF16).at[:, :2 * nh].set(w_if.astype(BF16))
    bif = jnp.zeros((1, LANES), F32).at[0, :2 * nh].set(b_if)
    kern = functools.partial(_qkv_kernel, tpb=tpb, nh=nh, kscale=float(dh) ** -0.5)
    full2 = lambda i: (0, 0)
    full3 = lambda i: (0, 0, 0)
    row = lambda i: (i, 0)
    return pl.pallas_call(
        kern,
        grid=(T // tm,),
        in_specs=[pl.BlockSpec((tm, di), row),
                  pl.BlockSpec((CONV_HALO, di), lambda i: (jnp.maximum(i * hb - 1, 0), 0)),
                  pl.BlockSpec((CONV_WIDTH, di), full2),
                  pl.BlockSpec((1, di), full2),
                  pl.BlockSpec((nh, dh, dh), full3),
                  pl.BlockSpec((nh, dh, dh), full3),
                  pl.BlockSpec((nh, dh, dh), full3),
                  pl.BlockSpec((di, LANES), full2),
                  pl.BlockSpec((1, LANES), full2)],
        out_specs=[pl.BlockSpec((tm, di), row)] * 4 + [pl.BlockSpec((tm, 2 * nh), row)],
        out_shape=[jax.ShapeDtypeStruct((T, di), BF16)] * 4 + [jax.ShapeDtypeStruct((T, 2 * nh), F32)],
        compiler_params=_params(("parallel",)),
        name="mlstm_qkv",
    )(xm, xm, conv_w, conv_b.reshape(1, di), wq, wk, wv, wif, bif)


def _mlstm_kernel(q_ref, k_ref, v_ref, gc_ref, gr_ref, o_ref, c_scr, n_scr, m_scr):
    L = q_ref.shape[0]
    nh = c_scr.shape[0]
    dh = c_scr.shape[1]

    @pl.when(pl.program_id(1) == 0)
    def _():
        c_scr[...] = jnp.zeros_like(c_scr)
        n_scr[...] = jnp.zeros_like(n_scr)
        m_scr[...] = jnp.zeros_like(m_scr)

    t_idx = lax.broadcasted_iota(jnp.int32, (L, L), 0)
    s_idx = lax.broadcasted_iota(jnp.int32, (L, L), 1)
    causal = s_idx <= t_idx
    for h in range(nh):
        hs = slice(h * dh, (h + 1) * dh)
        _mlstm_head(q_ref[:, hs], k_ref[:, hs], v_ref[:, hs], gc_ref[0, h], gr_ref[0, h],
                    o_ref.at[:, hs], c_scr.at[h], n_scr.at[h], m_scr.at[h], causal, t_idx, s_idx)


def _mlstm_head(q, k, v, gc, gr, o_ref, c_scr, n_scr, m_scr, causal, t_idx, s_idx):
    i_col, f_col = gc[:, 0:1], gc[:, 1:2]
    i_row, f_row = gr[0:1, :], gr[1:2, :]
    b_col = jnp.sum(jnp.where(causal, f_row, 0.0), axis=1, keepdims=True)
    b_row = jnp.sum(jnp.where(t_idx <= s_idx, f_col, 0.0), axis=0, keepdims=True)
    m_prev = m_scr[...]
    d = jnp.where(causal, b_col - b_row + i_row, -jnp.inf)
    a_col = b_col + m_prev
    m_t = jnp.maximum(a_col, jnp.max(d, axis=1, keepdims=True))
    w_intra = jnp.exp(d - m_t)
    w_inter = jnp.exp(a_col - m_t)
    qk = lax.dot_general(q, k, (((1,), (1,)), ((), ())), preferred_element_type=F32)
    s_mat = qk * w_intra
    c_b = c_scr[...].astype(BF16)
    inter = lax.dot_general(q, c_b, (((1,), (1,)), ((), ())), preferred_element_type=F32)
    num = jnp.dot(s_mat.astype(BF16), v, preferred_element_type=F32) + w_inter * inter
    qn = jnp.sum(q.astype(F32) * n_scr[...], axis=1, keepdims=True)
    den = jnp.sum(s_mat, axis=1, keepdims=True) + w_inter * qn
    hcap = num / jnp.maximum(jnp.abs(den), jnp.exp(-m_t))
    mu = jnp.mean(hcap, axis=1, keepdims=True)
    dv = hcap - mu
    var = jnp.mean(dv * dv, axis=1, keepdims=True)
    o_ref[...] = (dv * lax.rsqrt(var + LN_EPS)).astype(BF16)

    b_last = jnp.sum(f_row, axis=1, keepdims=True)
    ws_col = b_last - b_col + i_col
    m_new = jnp.maximum(b_last + m_prev, jnp.max(ws_col, axis=0, keepdims=True))
    decay = jnp.exp(b_last + m_prev - m_new)
    ws = jnp.exp(ws_col - m_new)
    vw = (v.astype(F32) * ws).astype(BF16)
    upd = lax.dot_general(vw, k, (((0,), (0,)), ((), ())), preferred_element_type=F32)
    c_scr[...] = decay * c_scr[...] + upd
    n_scr[...] = decay * n_scr[...] + jnp.sum(k.astype(F32) * ws, axis=0, keepdims=True)
    m_scr[...] = m_new


def _mlstm(q, k, v, gcol, grow, B, S):
    T, di = q.shape
    nh = gcol.shape[1]
    dh = di // nh
    L = min(MLSTM_CHUNK, S)
    nc = S // L
    blk = pl.BlockSpec((L, di), lambda b, c: (b * nc + c, 0))
    return pl.pallas_call(
        _mlstm_kernel,
        grid=(B, nc),
        in_specs=[blk, blk, blk,
                  pl.BlockSpec((1, nh, L, 2), lambda b, c: (b, 0, c, 0)),
                  pl.BlockSpec((1, nh, 2, L), lambda b, c: (b, 0, 0, c))],
        out_specs=blk,
        out_shape=jax.ShapeDtypeStruct((T, di), BF16),
        scratch_shapes=[pltpu.VMEM((nh, dh, dh), F32), pltpu.VMEM((nh, 1, dh), F32), pltpu.VMEM((nh, 1, 1), F32)],
        compiler_params=_params(("parallel", "arbitrary")),
        name="mlstm_scan",
    )(q, k, v, gcol, grow)


def _mlstm_out_kernel(hn_ref, xc_ref, z_ref, x_ref, gn_ref, skip_ref, w_ref, gate_ref, lg_ref, lb_ref, o_ref):
    z = z_ref[...].astype(F32)
    u = (hn_ref[...].astype(F32) * gn_ref[...] + skip_ref[...] * xc_ref[...].astype(F32)) * (z * _sigmoid(z))
    y = jnp.dot(u.astype(BF16), w_ref[...], preferred_element_type=F32)
    r = DEEPNORM_ALPHA * x_ref[...] + gate_ref[0] * y
    o_ref[...] = _layer_norm(r, lg_ref[...], lb_ref[...])


def _mlstm_out(hn, xc, z, x2, gn_g, skip, w_out, gate, ln_g, ln_b, S):
    T, D = x2.shape
    di = hn.shape[1]
    tm = min(512, S)
    tpb = S // tm
    row = lambda i: (i, 0)
    full2 = lambda i: (0, 0)
    return pl.pallas_call(
        _mlstm_out_kernel,
        grid=(T // tm,),
        in_specs=[pl.BlockSpec((tm, di), row), pl.BlockSpec((tm, di), row), pl.BlockSpec((tm, di), row),
                  pl.BlockSpec((tm, D), row),
                  pl.BlockSpec((1, di), full2), pl.BlockSpec((1, di), full2),
                  pl.BlockSpec((di, D), full2),
                  pl.BlockSpec((1, 1, D), lambda i: (i // tpb, 0, 0)),
                  pl.BlockSpec((1, D), full2), pl.BlockSpec((1, D), full2)],
        out_specs=pl.BlockSpec((tm, D), row),
        out_shape=jax.ShapeDtypeStruct((T, D), F32),
        compiler_params=_params(("parallel",)),
        name="mlstm_out_ln",
    )(hn, xc, z, x2, gn_g.reshape(1, di), skip.reshape(1, di), w_out, gate, ln_g.reshape(1, D), ln_b.reshape(1, D))


def _route_kernel(x_ref, sh_ref, sc_ref, wrt_ref, br_ref, tri_ref, out_ref, cnt_ref):
    E = wrt_ref.shape[0]
    ng = N_GROUPS
    epg = E // ng
    tm = x_ref.shape[0]
    h = x_ref[...] * (1.0 + sc_ref[0]) + sh_ref[0]
    nt = (((1,), (1,)), ((), ()))
    w = wrt_ref[...]
    w_hi = w.astype(BF16)
    w_lo = (w - w_hi.astype(F32)).astype(BF16)
    h_hi = h.astype(BF16)
    h_lo = (h - h_hi.astype(F32)).astype(BF16)
    lt = (lax.dot_general(w_hi, h_hi, nt, preferred_element_type=F32)
          + lax.dot_general(w_hi, h_lo, nt, preferred_element_type=F32)
          + lax.dot_general(w_lo, h_hi, nt, preferred_element_type=F32))
    ex = jnp.exp(lt - jnp.max(lt, axis=0, keepdims=True))
    probs = ex / jnp.sum(ex, axis=0, keepdims=True)
    sel = probs + br_ref[...]
    member = [sel[k * ng:(k + 1) * ng, :] for k in range(epg)]
    pmember = [probs[k * ng:(k + 1) * ng, :] for k in range(epg)]
    gscore = None
    for a in range(epg):
        for b in range(a + 1, epg):
            pair = member[a] + member[b]
            gscore = pair if gscore is None else jnp.maximum(gscore, pair)
    gidx = lax.broadcasted_iota(jnp.int32, (ng, tm), 0)
    gmax = jnp.max(gscore, axis=0, keepdims=True)
    chosen = gidx == jnp.min(jnp.where(gscore == gmax, gidx, ng), axis=0, keepdims=True)
    zero_g = jnp.zeros((ng, tm), F32)
    e1, e2, p1, p2 = zero_g, zero_g, zero_g, zero_g
    firsts, seconds = [], []
    for k in range(epg):
        rank = zero_g
        for o in range(epg):
            if o != k:
                beats = (member[o] >= member[k]) if o < k else (member[o] > member[k])
                rank = rank + jnp.where(beats, 1.0, 0.0)
        is1 = chosen & (rank == 0.0)
        is2 = chosen & (rank == 1.0)
        firsts.append(jnp.where(is1, 1.0, 0.0))
        seconds.append(jnp.where(is2, 1.0, 0.0))
        eid = (gidx * epg + k).astype(F32)
        e1 = e1 + jnp.where(is1, eid, 0.0)
        e2 = e2 + jnp.where(is2, eid, 0.0)
        p1 = p1 + jnp.where(is1, pmember[k], 0.0)
        p2 = p2 + jnp.where(is2, pmember[k], 0.0)
    e1, e2, p1, p2 = [jnp.sum(v, axis=0, keepdims=True) for v in (e1, e2, p1, p2)]
    first = jnp.concatenate(firsts, axis=0)
    second = jnp.concatenate(seconds, axis=0)
    mask = first + second
    prefix = jnp.dot(mask.astype(BF16), tri_ref[...], preferred_element_type=F32)
    counts = jnp.sum(mask, axis=1, keepdims=True)
    padded = jnp.floor((counts + (RUN_ALIGN - 1)) * (1.0 / RUN_ALIGN)) * RUN_ALIGN
    row_of = [(e % epg) * ng + e // epg for e in range(E)]
    starts = [None] * E
    acc = jnp.zeros((1, 1), F32)
    for e in range(E):
        starts[row_of[e]] = acc
        acc = acc + padded[row_of[e]:row_of[e] + 1, :]
    slot = prefix + jnp.concatenate(starts, axis=0)
    s1 = jnp.sum(first * slot, axis=0, keepdims=True)
    s2 = jnp.sum(second * slot, axis=0, keepdims=True)
    psum = p1 + p2
    zero = jnp.zeros((1, tm), F32)
    out_ref[...] = jnp.concatenate([e1, e2, p1 / psum, p2 / psum, s1, s2, zero, zero], axis=0)
    counts_by_expert = jnp.concatenate([counts[row_of[e]:row_of[e] + 1, :] for e in range(E)], axis=0)
    cnt_ref[...] = jnp.broadcast_to(counts_by_expert, cnt_ref.shape)


def _route(x2, shift, scale, w_router, b_router, S):
    T, D = x2.shape
    E = w_router.shape[1]
    tm = min(MOE_TILE, S)
    tpb = S // tm
    tri = (jnp.arange(tm)[:, None] < jnp.arange(tm)[None, :]).astype(BF16)
    epg = E // N_GROUPS
    rows = jnp.array([g * epg + k for k in range(epg) for g in range(N_GROUPS)], jnp.int32)
    return pl.pallas_call(
        _route_kernel,
        grid=(T // tm,),
        in_specs=[pl.BlockSpec((tm, D), lambda i: (i, 0)),
                  pl.BlockSpec((1, 1, D), lambda i: (i // tpb, 0, 0)),
                  pl.BlockSpec((1, 1, D), lambda i: (i // tpb, 0, 0)),
                  pl.BlockSpec((E, D), lambda i: (0, 0)),
                  pl.BlockSpec((E, 1), lambda i: (0, 0)),
                  pl.BlockSpec((tm, tm), lambda i: (0, 0))],
        out_specs=[pl.BlockSpec((8, tm), lambda i: (0, i)),
                   pl.BlockSpec((E, LANES), lambda i: (0, i))],
        out_shape=[jax.ShapeDtypeStruct((8, T), F32), jax.ShapeDtypeStruct((E, (T // tm) * LANES), F32)],
        compiler_params=_params(("parallel",)),
        name="moe_route",
    )(x2, shift, scale, w_router.T[rows], b_router[rows].reshape(E, 1), tri)


def _stage_rows(tm, n_exp):
    return MOE_TOPK * tm + n_exp * RUN_ALIGN


def _chunk_copies(tc_ref, ts_ref, ds_ref, step, n_exp, make):
    out = []
    for e in range(n_exp):
        cnt = tc_ref[step * n_exp + e]
        off = ts_ref[step * n_exp + e]
        dst = ds_ref[step * n_exp + e]
        for k in reversed(range(RUN_ALIGN.bit_length() - 1, CHUNK_BITS)):
            done = (cnt >> (k + 1)) << (k + 1)
            out.append(((cnt & (1 << k)) != 0,
                        make(pl.multiple_of(off + done, RUN_ALIGN), pl.multiple_of(dst + done, RUN_ALIGN), 1 << k)))
    return out


def _dispatch_kernel(tc_ref, ts_ref, ds_ref, pe_ref, x_ref, sh_ref, sc_ref, pos_ref, xs_ref,
                     stage, zbuf, sem, zsem):
    tm = x_ref.shape[0]
    rb = zbuf.shape[0]
    i = pl.program_id(0)
    slot = i % 2

    n_exp = pe_ref.shape[0] - 1
    n_blocks = xs_ref.shape[0] // rb

    def pad_copy(e):
        return pltpu.make_async_copy(zbuf, xs_ref.at[pl.ds(pl.multiple_of(pe_ref[e + 1] - rb, rb), rb)], zsem)

    def tail_copy(j):
        return pltpu.make_async_copy(zbuf, xs_ref.at[pl.ds(pl.multiple_of(pe_ref[n_exp] + j * rb, rb), rb)], zsem)

    def tail_live(j):
        return pe_ref[n_exp] // rb + j < n_blocks

    @pl.when(i == 0)
    def _():
        zbuf[...] = jnp.zeros_like(zbuf)
        for e in range(n_exp):
            @pl.when(pe_ref[e + 1] > pe_ref[e])
            def _():
                pad_copy(e).start()
        for j in range(n_exp):
            @pl.when(tail_live(j))
            def _():
                tail_copy(j).start()
        for e in range(n_exp):
            @pl.when(pe_ref[e + 1] > pe_ref[e])
            def _():
                pad_copy(e).wait()
        for j in range(n_exp):
            @pl.when(tail_live(j))
            def _():
                tail_copy(j).wait()

    h = (x_ref[...] * (1.0 + sc_ref[0]) + sh_ref[0]).astype(BF16)
    pos = pos_ref[0]
    slot_id = lax.broadcasted_iota(jnp.int32, (stage.shape[1], tm), 0)
    perm = jnp.where((slot_id == pos[0:1, :]) | (slot_id == pos[1:2, :]), 1.0, 0.0).astype(BF16)
    stage[slot] = jnp.dot(perm, h, preferred_element_type=F32)

    def copies(step, s):
        def make(src_row, dst_row, size):
            return pltpu.make_async_copy(stage.at[s, pl.ds(src_row, size)], xs_ref.at[pl.ds(dst_row, size)], sem.at[s])
        return _chunk_copies(tc_ref, ts_ref, ds_ref, step, n_exp, make)

    for live, cp in copies(i, slot):
        pl.when(live)(cp.start)
    for live, cp in copies(jnp.maximum(i - 1, 0), 1 - slot):
        pl.when(live & (i > 0))(cp.wait)
    for live, cp in copies(i, slot):
        pl.when(live & (i == pl.num_programs(0) - 1))(cp.wait)


def _dispatch(x2, shift, scale, posrow, tcnt, tstart, dstart, pends0, cap, S):
    T, D = x2.shape
    tm = min(MOE_TILE, S)
    tpb = S // tm
    mod = pl.BlockSpec((1, 1, D), lambda i, *_: (i // tpb, 0, 0))
    grid_spec = pltpu.PrefetchScalarGridSpec(
        num_scalar_prefetch=4,
        grid=(T // tm,),
        in_specs=[pl.BlockSpec((tm, D), lambda i, *_: (i, 0)), mod, mod,
                  pl.BlockSpec((1, 2, tm), lambda i, *_: (i, 0, 0))],
        out_specs=pl.BlockSpec(memory_space=pl.ANY),
        scratch_shapes=[pltpu.VMEM((2, _stage_rows(tm, pends0.shape[0] - 1), D), F32),
                        pltpu.VMEM((MOE_ROW_BLOCK, D), F32),
                        pltpu.SemaphoreType.DMA((2,)), pltpu.SemaphoreType.DMA(())],
    )
    return pl.pallas_call(
        _dispatch_kernel,
        grid_spec=grid_spec,
        out_shape=jax.ShapeDtypeStruct((cap, D), F32),
        compiler_params=_params(("arbitrary",)),
        name="moe_dispatch",
    )(tcnt, tstart, dstart, pends0, x2, shift, scale, posrow)


def _expert_kernel(be_ref, na_ref, xs_ref, wg_ref, wu_ref, wd_ref, ys_ref, wg_b, wu_b, wd_b):
    i = pl.program_id(0)
    active = i < na_ref[0]

    @pl.when(active & ((i == 0) | (be_ref[i] != be_ref[jnp.maximum(i - 1, 0)])))
    def _():
        wg_b[...] = wg_ref[0, 0].astype(BF16)
        wu_b[...] = wu_ref[0, 0].astype(BF16)
        wd_b[...] = wd_ref[0, 0].astype(BF16)

    @pl.when(active)
    def _():
        xb = xs_ref[...].astype(BF16)
        g = jnp.dot(xb, wg_b[...], preferred_element_type=F32)
        u = jnp.dot(xb, wu_b[...], preferred_element_type=F32)
        a = (g * _sigmoid(g)) * u
        ys_ref[...] = jnp.dot(a.astype(BF16), wd_b[...], preferred_element_type=F32)

    @pl.when(jnp.logical_not(active))
    def _():
        ys_ref[...] = jnp.zeros_like(ys_ref)


def _experts(xs, block_expert, n_active, w_gate, w_up, w_down, layer):
    cap, D = xs.shape
    _, E, _, F = w_gate.shape
    rb = MOE_ROW_BLOCK
    nb = cap // rb

    def blk(i, be, na):
        return jnp.maximum(jnp.minimum(i, na[0] - 1), 0)

    grid_spec = pltpu.PrefetchScalarGridSpec(
        num_scalar_prefetch=2,
        grid=(nb,),
        in_specs=[pl.BlockSpec((rb, D), lambda i, be, na: (blk(i, be, na), 0)),
                  pl.BlockSpec((1, 1, D, F), lambda i, be, na: (layer, be[blk(i, be, na)], 0, 0)),
                  pl.BlockSpec((1, 1, D, F), lambda i, be, na: (layer, be[blk(i, be, na)], 0, 0)),
                  pl.BlockSpec((1, 1, F, D), lambda i, be, na: (layer, be[blk(i, be, na)], 0, 0))],
        out_specs=pl.BlockSpec((rb, D), lambda i, be, na: (i, 0)),
        scratch_shapes=[pltpu.VMEM((D, F), BF16), pltpu.VMEM((D, F), BF16), pltpu.VMEM((F, D), BF16)],
    )
    return pl.pallas_call(
        _expert_kernel,
        grid_spec=grid_spec,
        out_shape=jax.ShapeDtypeStruct((cap, D), F32),
        compiler_params=_params(("arbitrary",)),
        name="moe_experts",
    )(block_expert, n_active, xs, w_gate, w_up, w_down)


def _combine_kernel(tc_ref, ts_ref, ds_ref, ys_ref, rt_ref, x_ref, gate_ref, lg_ref, lb_ref, o_ref, stage, sem,
                    *, n_exp):
    tm = x_ref.shape[0]
    i = pl.program_id(0)
    slot = i % 2

    def copies(step, s):
        def make(stage_row, ys_row, size):
            return pltpu.make_async_copy(ys_ref.at[pl.ds(ys_row, size)], stage.at[s, pl.ds(stage_row, size)], sem.at[s])
        return _chunk_copies(tc_ref, ts_ref, ds_ref, step, n_exp, make)

    @pl.when(i == 0)
    def _():
        stage[...] = jnp.zeros_like(stage)

    for live, cp in copies(i, slot):
        pl.when(live & (i == 0))(cp.start)
    nxt = jnp.minimum(i + 1, pl.num_programs(0) - 1)
    for live, cp in copies(nxt, 1 - slot):
        pl.when(live & (i + 1 < pl.num_programs(0)))(cp.start)
    for live, cp in copies(i, slot):
        pl.when(live)(cp.wait)

    rt = rt_ref[...]
    ysb = stage[slot].astype(BF16)
    slot_id = lax.broadcasted_iota(jnp.int32, (tm, stage.shape[1]), 1).astype(F32)
    y = None
    for j in range(MOE_TOPK):
        perm = jnp.where(slot_id == rt[:, 4 + j:5 + j], 1.0, 0.0).astype(BF16)
        part = rt[:, 2 + j:3 + j] * jnp.dot(perm, ysb, preferred_element_type=F32)
        y = part if y is None else y + part
    r = DEEPNORM_ALPHA * x_ref[...] + gate_ref[0] * y
    o_ref[...] = _layer_norm(r, lg_ref[...], lb_ref[...])


def _combine(ys, route_t, x2, gate, ln_g, ln_b, tcnt, tstart, dstart, S):
    T, D = x2.shape
    tm = min(MOE_TILE, S)
    tpb = S // tm
    grid_spec = pltpu.PrefetchScalarGridSpec(
        num_scalar_prefetch=3,
        grid=(T // tm,),
        in_specs=[pl.BlockSpec(memory_space=pl.ANY),
                  pl.BlockSpec((tm, 8), lambda i, *_: (i, 0)),
                  pl.BlockSpec((tm, D), lambda i, *_: (i, 0)),
                  pl.BlockSpec((1, 1, D), lambda i, *_: (i // tpb, 0, 0)),
                  pl.BlockSpec((1, D), lambda i, *_: (0, 0)),
                  pl.BlockSpec((1, D), lambda i, *_: (0, 0))],
        out_specs=pl.BlockSpec((tm, D), lambda i, *_: (i, 0)),
        scratch_shapes=[pltpu.VMEM((2, _stage_rows(tm, tcnt.shape[0] // (T // tm)), D), F32),
                        pltpu.SemaphoreType.DMA((2,))],
    )
    return pl.pallas_call(
        functools.partial(_combine_kernel, n_exp=tcnt.shape[0] // (T // tm)),
        grid_spec=grid_spec,
        out_shape=jax.ShapeDtypeStruct((T, D), F32),
        compiler_params=_params(("arbitrary",)),
        name="moe_combine_ln",
    )(tcnt, tstart, dstart, ys, route_t, x2, gate, ln_g.reshape(1, D), ln_b.reshape(1, D))


def _moe_layer(x2, shift, scale, gate, ln_g, ln_b, w_router, b_router, w_gate, w_up, w_down, layer, S):
    T, D = x2.shape
    E = w_router.shape[1]
    rb = MOE_ROW_BLOCK
    tm = min(MOE_TILE, S)
    nt = T // tm
    assert tm < (1 << CHUNK_BITS)
    route, cnt = _route(x2, shift, scale, w_router, b_router, S)
    tcnt = cnt.reshape(E, nt, LANES)[:, :, 0].T.astype(jnp.int32)
    tcnt = (tcnt + RUN_ALIGN - 1) // RUN_ALIGN * RUN_ALIGN
    base = jnp.cumsum(tcnt, axis=0) - tcnt
    counts = jnp.sum(tcnt, axis=0)
    padded = (counts + rb - 1) // rb * rb
    pends = jnp.cumsum(padded)
    pstarts = pends - padded
    tstart = jnp.cumsum(tcnt, axis=1) - tcnt
    dstart = pstarts[None, :] + base
    posrow = jnp.transpose(route[4:6].astype(jnp.int32).reshape(2, nt, tm), (1, 0, 2))
    nb = -(-(T * MOE_TOPK + nt * E * (RUN_ALIGN - 1)) // rb) + E
    cap = nb * rb
    block_start = jnp.arange(nb, dtype=jnp.int32) * rb
    block_expert = jnp.minimum(jnp.sum(block_start[:, None] >= pends[None, :], axis=1), E - 1).astype(jnp.int32)
    n_active = (pends[-1:] // rb).astype(jnp.int32)
    pends0 = jnp.concatenate([jnp.zeros((1,), jnp.int32), pends.astype(jnp.int32)])
    tables = [t.reshape(-1).astype(jnp.int32) for t in (tcnt, tstart, dstart)]
    xs = _dispatch(x2, shift, scale, posrow, *tables, pends0, cap, S)
    ys = _experts(xs, block_expert, n_active, w_gate, w_up, w_down, layer)
    return _combine(ys, route.T, x2, gate, ln_g, ln_b, *tables, S)


def _moba_proj_kernel(x_ref, shq_ref, scq_ref, shkv_ref, sckv_ref, wq_ref, wkv_ref,
                      q_ref, k_ref, vt_ref, km_ref, *, qscale, tpb, slopes):
    tm, D = x_ref.shape
    hd = HEAD_DIM
    nh = D // hd
    x = x_ref[...]
    hq = x * (1.0 + scq_ref[0]) + shq_ref[0]
    hkv = x * (1.0 + sckv_ref[0]) + shkv_ref[0]
    qf = jnp.dot(hq.astype(BF16), wq_ref[...], preferred_element_type=F32) * qscale
    kv = jnp.dot(hkv.astype(BF16), wkv_ref[...], preferred_element_type=F32)
    kf = kv[:, :D]
    vt = kv[:, D:].T
    ones_pad = jnp.where(lax.broadcasted_iota(jnp.int32, (V_ROWS - hd, tm), 0) == 0, 1.0, 0.0)
    for g in range(nh):
        vt_ref[0, g * V_ROWS:(g + 1) * V_ROWS, :] = jnp.concatenate(
            [vt[g * hd:(g + 1) * hd, :], ones_pad], axis=0).astype(BF16)
    lane = lax.broadcasted_iota(jnp.int32, (1, LANES), 1)
    head_lanes = lane >= hd
    pos = (pl.program_id(0) % tpb) * tm + lax.broadcasted_iota(jnp.int32, (tm, 1), 0)
    pos_f = pos.astype(F32)
    onehot = jnp.where(lane == pos // MOBA_BLOCK, 1.0, 0.0)
    q_fill = jnp.where((lane >= BIAS_LANE) & (lane < BIAS_LANE + 3), 1.0, 0.0)
    nbt = tm // MOBA_BLOCK
    means = [jnp.mean(kf[j * MOBA_BLOCK:(j + 1) * MOBA_BLOCK, :], axis=0, keepdims=True) for j in range(nbt)]
    km = jnp.concatenate(means + [jnp.zeros((8 - nbt, D), F32)], axis=0)
    for j in range(nh // 2):
        cs = slice(j * LANES, (j + 1) * LANES)
        qb, kb, mb = qf[:, cs], kf[:, cs], km[:, cs]
        q_pair = (pltpu.roll(qb, hd, axis=1), qb)
        k_pair = (pltpu.roll(kb, hd, axis=1), kb)
        m_pair = (pltpu.roll(mb, hd, axis=1), mb)
        for e in range(2):
            g = 2 * j + e
            gs = slice(g * LANES, (g + 1) * LANES)
            q_ref[:, gs] = jnp.where(head_lanes, q_pair[e], q_fill).astype(BF16)
            c = (slopes[g] * LOG2E) * pos_f
            hi = c.astype(BF16).astype(F32)
            mid = (c - hi).astype(BF16).astype(F32)
            lo = c - hi - mid
            aug = jnp.where(lane == BIAS_LANE, hi,
                            jnp.where(lane == BIAS_LANE + 1, mid,
                                      jnp.where(lane == BIAS_LANE + 2, lo, onehot)))
            k_ref[:, gs] = jnp.where(head_lanes, k_pair[e], aug).astype(BF16)
            mg = jnp.where(head_lanes, m_pair[e], 0.0)
            for jb in range(nbt):
                km_ref[jb, :, gs] = mg[jb:jb + 1, :]


def _moba_proj(x2, shq, scq, shkv, sckv, wq, wkv, B, S):
    T, D = x2.shape
    nh = D // HEAD_DIM
    tm = min(512, S)
    tpb = S // tm
    nbt = tm // MOBA_BLOCK
    assert S // MOBA_BLOCK <= BIAS_LANE and nbt <= 8
    slopes = tuple(2.0 ** (-8.0 * (h + 1.0) / nh) for h in range(nh))
    mod = pl.BlockSpec((1, 1, D), lambda i: (i // tpb, 0, 0))
    kern = functools.partial(_moba_proj_kernel, qscale=float(HEAD_DIM) ** -0.5 * LOG2E, tpb=tpb, slopes=slopes)
    return pl.pallas_call(
        kern,
        grid=(T // tm,),
        in_specs=[pl.BlockSpec((tm, D), lambda i: (i, 0)), mod, mod, mod, mod,
                  pl.BlockSpec((D, D), lambda i: (0, 0)),
                  pl.BlockSpec((D, 2 * D), lambda i: (0, 0))],
        out_specs=[pl.BlockSpec((tm, nh * LANES), lambda i: (i, 0)),
                   pl.BlockSpec((tm, nh * LANES), lambda i: (i, 0)),
                   pl.BlockSpec((1, nh * V_ROWS, tm), lambda i: (i // tpb, 0, i % tpb)),
                   pl.BlockSpec((nbt, 1, nh * LANES), lambda i: (i, 0, 0))],
        out_shape=[jax.ShapeDtypeStruct((T, nh * LANES), BF16), jax.ShapeDtypeStruct((T, nh * LANES), BF16),
                   jax.ShapeDtypeStruct((B, nh * V_ROWS, S), BF16),
                   jax.ShapeDtypeStruct((T // MOBA_BLOCK, 1, nh * LANES), F32)],
        compiler_params=_params(("parallel",)),
        name="moba_proj",
    )(x2, shq, scq, shkv, sckv, wq, wkv)


def _moba_select_kernel(q_ref, km_ref, qat_ref):
    tq = q_ref.shape[0]
    G = q_ref.shape[1] // LANES
    nblk = km_ref.shape[1]
    blk_id = lax.broadcasted_iota(jnp.int32, (nblk, tq), 0)
    own = (pl.program_id(2) * tq + lax.broadcasted_iota(jnp.int32, (1, tq), 1)) // MOBA_BLOCK
    for g in range(G):
        gs = slice(g * LANES, (g + 1) * LANES)
        qgt = q_ref[:, gs].astype(F32).T
        qb = qgt.astype(BF16)
        km = km_ref[0, :, gs]
        km_hi = km.astype(BF16)
        km_mid = (km - km_hi.astype(F32)).astype(BF16)
        km_lo = (km - km_hi.astype(F32) - km_mid.astype(F32)).astype(BF16)
        gate = (jnp.dot(km_hi, qb, preferred_element_type=F32) + jnp.dot(km_mid, qb, preferred_element_type=F32)
                + jnp.dot(km_lo, qb, preferred_element_type=F32))
        gsc = jnp.where(blk_id < own, gate, -jnp.inf)
        picked = blk_id < 0
        for _ in range(MOBA_TOPK):
            mx = jnp.max(gsc, axis=0, keepdims=True)
            first = jnp.min(jnp.where(gsc == mx, blk_id, nblk), axis=0, keepdims=True)
            pick = blk_id == first
            picked = picked | pick
            gsc = jnp.where(pick, -jnp.inf, gsc)
        visible = (picked & (blk_id < own)) | (blk_id == own)
        pen = jnp.where(visible, 0.0, NEG)
        qat_ref[0, g] = (qgt + jnp.concatenate([pen, jnp.zeros((LANES - nblk, tq), F32)], axis=0)).astype(BF16)


def _moba_select(q, km, B, S):
    nh = q.shape[1] // LANES
    G = MOBA_HEADS_PER_STEP
    tq = min(512, S)
    nblk = S // MOBA_BLOCK
    return pl.pallas_call(
        _moba_select_kernel,
        grid=(B, nh // G, S // tq),
        in_specs=[pl.BlockSpec((tq, G * LANES), lambda b, hp, i: (b * (S // tq) + i, hp)),
                  pl.BlockSpec((1, nblk, G * LANES), lambda b, hp, i: (b, 0, hp))],
        out_specs=pl.BlockSpec((1, G, LANES, tq), lambda b, hp, i: (b, hp, 0, i)),
        out_shape=jax.ShapeDtypeStruct((B, nh, LANES, S), BF16),
        compiler_params=_params(("parallel", "parallel", "parallel")),
        name="moba_select",
    )(q, km)


def _moba_kernel(qat_ref, qat_next_ref, k_ref, vt_ref, o_ref, st_scr, m_scr, acc_scr):
    G = qat_ref.shape[1]
    BLK = qat_ref.shape[3]
    hd = HEAD_DIM
    nhalf = BLK // LANES
    own = pl.program_id(2)
    k_off = lax.broadcasted_iota(jnp.int32, (BLK, 1), 0)
    q_off = lax.broadcasted_iota(jnp.int32, (1, LANES), 1)

    m_scr[...] = jnp.full(m_scr.shape, NEG, F32)
    acc_scr[...] = jnp.zeros_like(acc_scr)

    def scores(n, slot, q_ref=qat_ref):
        start = pl.multiple_of(n * BLK, BLK)
        for g in range(G):
            kn = k_ref[pl.ds(start, BLK), g * LANES:(g + 1) * LANES]
            st_scr[slot, g] = jnp.dot(kn, q_ref[0, g], preferred_element_type=F32)

    def update(n, slot, causal):
        start = pl.multiple_of(n * BLK, BLK)
        for g in range(G):
            alphas, ps = [], []
            for hf in range(nhalf):
                c = g * nhalf + hf
                ls = slice(hf * LANES, (hf + 1) * LANES)
                st = st_scr[slot, g, :, ls]
                if causal:
                    st = jnp.where(k_off <= q_off + hf * LANES, st, NEG)
                m = m_scr[c]
                m_new = jnp.maximum(m, jnp.max(st, axis=0, keepdims=True))
                alphas.append(jnp.exp2(m - m_new))
                ps.append(jnp.exp2(st - m_new).astype(BF16))
                m_scr[c] = m_new
            vtn = vt_ref[0, g * V_ROWS:(g + 1) * V_ROWS, pl.ds(start, BLK)]
            pv = jnp.dot(vtn, jnp.concatenate(ps, axis=1), preferred_element_type=F32)
            acc_scr[g] = jnp.concatenate(alphas, axis=1) * acc_scr[g] + pv

    @pl.when(own == 0)
    def _():
        scores(0, 0)

    def pair(n):
        scores(n + 1, 1)
        update(n, 0, False)
        scores(n + 2, 0)
        update(n + 1, 1, False)

    def two_pairs(i, carry):
        pair(4 * i)
        pair(4 * i + 2)
        return carry

    def one_pair(i, carry):
        pair(4 * (own // 4) + 2 * i)
        return carry

    lax.fori_loop(0, own // 4, two_pairs, 0)
    lax.fori_loop(0, (own % 4) // 2, one_pair, 0)

    @pl.when(own % 2 == 0)
    def _():
        update(own, 0, True)
        scores(0, 0, qat_next_ref)

    @pl.when(own % 2 == 1)
    def _():
        scores(own, 1)
        update(own - 1, 0, False)
        scores(0, 0, qat_next_ref)
        update(own, 1, True)
    for g in range(G):
        acc = acc_scr[g]
        o_ref[0, g * hd:(g + 1) * hd, :] = (acc[:hd, :] * (1.0 / acc[hd:hd + 1, :])).astype(BF16)


def _moba_attn(qat, k, vt, B, S):
    nh = qat.shape[1]
    G = MOBA_HEADS_PER_STEP
    BLK = MOBA_BLOCK
    nblk = S // BLK
    hd = HEAD_DIM
    return pl.pallas_call(
        _moba_kernel,
        grid=(B, nh // G, nblk),
        in_specs=[pl.BlockSpec((1, G, LANES, BLK), lambda b, hp, qi: (b, hp, 0, qi)),
                  pl.BlockSpec((1, G, LANES, BLK), lambda b, hp, qi: (b, hp, 0, jnp.minimum(qi + 1, nblk - 1))),
                  pl.BlockSpec((S, G * LANES), lambda b, hp, qi: (b, hp)),
                  pl.BlockSpec((1, G * V_ROWS, S), lambda b, hp, qi: (b, hp, 0))],
        out_specs=pl.BlockSpec((1, G * hd, BLK), lambda b, hp, qi: (b, hp, qi)),
        out_shape=jax.ShapeDtypeStruct((B, nh * hd, S), BF16),
        scratch_shapes=[pltpu.VMEM((2, G, BLK, BLK), F32),
                        pltpu.VMEM((G * (BLK // LANES), 1, LANES), F32),
                        pltpu.VMEM((G, V_ROWS, BLK), F32)],
        compiler_params=_params(("parallel", "parallel", "arbitrary")),
        name="moba_attn",
    )(qat, qat, k, vt)


def _attn_out_kernel(a_ref, x_ref, w_ref, gate_ref, lg_ref, lb_ref, o_ref):
    y = lax.dot_general(a_ref[0], w_ref[...], (((0,), (0,)), ((), ())), preferred_element_type=F32)
    r = DEEPNORM_ALPHA * x_ref[...] + gate_ref[0] * y
    o_ref[...] = _layer_norm(r, lg_ref[...], lb_ref[...])


def _attn_out(at, x2, w, gate, ln_g, ln_b, S):
    T, D = x2.shape
    K = at.shape[1]
    tm = min(512, S)
    tpb = S // tm
    row = lambda i: (i, 0)
    full2 = lambda i: (0, 0)
    return pl.pallas_call(
        _attn_out_kernel,
        grid=(T // tm,),
        in_specs=[pl.BlockSpec((1, K, tm), lambda i: (i // tpb, 0, i % tpb)), pl.BlockSpec((tm, D), row),
                  pl.BlockSpec((K, D), full2),
                  pl.BlockSpec((1, 1, D), lambda i: (i // tpb, 0, 0)),
                  pl.BlockSpec((1, D), full2), pl.BlockSpec((1, D), full2)],
        out_specs=pl.BlockSpec((tm, D), row),
        out_shape=jax.ShapeDtypeStruct((T, D), F32),
        compiler_params=_params(("parallel",)),
        name="attn_out_ln",
    )(at, x2, w, gate, ln_g.reshape(1, D), ln_b.reshape(1, D))


def kernel(x, c, w_ada, b_ada, ln_g, ln_b, a_w_in, a_conv_w, a_conv_b, a_wq, a_wk, a_wv, a_w_if, a_b_if,
           a_gn_g, a_skip, a_w_out, b_w_kv, b_wq, b_wo, moe_w_router, moe_b_router, moe_w_gate, moe_w_up,
           moe_w_down):
    B, S, D = x.shape
    T = B * S
    assert DEPTH == 2 and S % MOBA_BLOCK == 0 and S % MLSTM_CHUNK == 0
    n_layer_mod = DEPTH * N_MOD_PER_LAYER * D
    cond = _ada_cond(c, w_ada, b_ada)
    mods = cond[:, :n_layer_mod].reshape(B, DEPTH, 2, 3, 1, D)
    kv_mod = cond[:, n_layer_mod:].reshape(B, 2, 1, D)

    def mod3(layer, sub):
        m = mods[:, layer, sub]
        return m[:, 0], m[:, 1], 1.0 + m[:, 2]

    xf = x.reshape(T, D)

    shift, scale, gate = mod3(0, 0)
    nh = MLSTM_HEADS
    xm, z = _inproj(xf, shift, scale, a_w_in[0].astype(BF16), S)
    q, k, v, xc, gts = _qkv(xm, a_conv_w[0], a_conv_b[0], a_wq[0].astype(BF16), a_wk[0].astype(BF16),
                            a_wv[0].astype(BF16), a_w_if[0], a_b_if[0], S)
    g4 = gts.reshape(B, S, 2, nh)
    gcol = jnp.transpose(g4, (0, 3, 1, 2))
    grow = jnp.transpose(g4, (0, 3, 2, 1))
    hn = _mlstm(q, k, v, gcol, grow, B, S)
    xf = _mlstm_out(hn, xc, z, xf, a_gn_g[0], a_skip[0], a_w_out[0].astype(BF16), gate, ln_g[0, 0], ln_b[0, 0], S)
    shift, scale, gate = mod3(0, 1)
    xf = _moe_layer(xf, shift, scale, gate, ln_g[0, 1], ln_b[0, 1], moe_w_router, moe_b_router,
                    moe_w_gate, moe_w_up, moe_w_down, 0, S)

    shift, scale, gate = mod3(1, 0)
    q, k, vt, km = _moba_proj(xf, shift, scale, kv_mod[:, 0], kv_mod[:, 1], b_wq[0].astype(BF16),
                              b_w_kv.astype(BF16), B, S)
    km = km.reshape(B, S // MOBA_BLOCK, km.shape[-1])
    attn_t = _moba_attn(_moba_select(q, km, B, S), k, vt, B, S)
    xf = _attn_out(attn_t, xf, b_wo[0].astype(BF16), gate, ln_g[1, 0], ln_b[1, 0], S)
    shift, scale, gate = mod3(1, 1)
    xf = _moe_layer(xf, shift, scale, gate, ln_g[1, 1], ln_b[1, 1], moe_w_router, moe_b_router,
                    moe_w_gate, moe_w_up, moe_w_down, 1, S)
    return xf.reshape(B, S, D)
```

```python
import functools

import jax
import jax.numpy as jnp
from jax import lax
from jax.experimental import pallas as pl
from jax.experimental.pallas import tpu as pltpu

DEPTH = 2
MLSTM_HEADS = 4
CONV_WIDTH = 4
MLSTM_CHUNK = 512
ATTN_HEADS = 16
MOBA_BLOCK = 256
MOBA_TOPK = 3
N_EXPERTS = 16
N_GROUPS = 4
MOE_TOPK = 2
MOE_ROW_BLOCK = 256
MOE_TILE = 512
CHUNK_BITS = 10
RUN_ALIGN = 8
DEEPNORM_ALPHA = (2.0 * DEPTH) ** 0.25
LN_EPS = 1e-5
N_MOD_PER_LAYER = 6

HEAD_DIM = 64
V_ROWS = 80
MOBA_HEADS_PER_STEP = 4
BIAS_LANE = 32
LOG2E = 1.4426950408889634

LANES = 128
CONV_HALO = 16
NEG = -1e30
VMEM_LIMIT = 56 * 1024 * 1024

F32 = jnp.float32
BF16 = jnp.bfloat16


def _sigmoid(x):
    return 1.0 / (1.0 + jnp.exp(-x))


def _params(sem, vmem=VMEM_LIMIT):
    return pltpu.CompilerParams(dimension_semantics=sem, vmem_limit_bytes=vmem)


def _layer_norm(r, g, b):
    mu = jnp.mean(r, axis=-1, keepdims=True)
    d = r - mu
    var = jnp.mean(d * d, axis=-1, keepdims=True)
    return d * lax.rsqrt(var + LN_EPS) * g + b


def _ada_kernel(c_ref, w_ref, b_ref, o_ref):
    c = c_ref[...]
    s = c * _sigmoid(c)
    o_ref[...] = jnp.dot(s.astype(BF16), w_ref[...].astype(BF16),
                         preferred_element_type=F32) + b_ref[...]


def _ada_cond(c, w_ada, b_ada):
    B, D = c.shape
    N = w_ada.shape[1]
    tn = 2048 if N % 2048 == 0 else N
    cp = jnp.zeros((8, D), F32).at[:B].set(c)
    out = pl.pallas_call(
        _ada_kernel,
        grid=(N // tn,),
        in_specs=[pl.BlockSpec((8, D), lambda j: (0, 0)),
                  pl.BlockSpec((D, tn), lambda j: (0, j)),
                  pl.BlockSpec((1, tn), lambda j: (0, j))],
        out_specs=pl.BlockSpec((8, tn), lambda j: (0, j)),
        out_shape=jax.ShapeDtypeStruct((8, N), F32),
        compiler_params=_params(("arbitrary",)),
        name="ada_cond",
    )(cp, w_ada, b_ada.reshape(1, N))
    return out[:B]


def _inproj_kernel(x_ref, sh_ref, sc_ref, w_ref, xm_ref, z_ref):
    di = xm_ref.shape[-1]
    h = x_ref[...] * (1.0 + sc_ref[0]) + sh_ref[0]
    r = jnp.dot(h.astype(BF16), w_ref[...], preferred_element_type=F32)
    xm_ref[...] = r[:, :di].astype(BF16)
    z_ref[...] = r[:, di:].astype(BF16)


def _inproj(x2, shift, scale, w_in, S):
    T, D = x2.shape
    di = w_in.shape[1] // 2
    tm = min(512, S)
    tpb = S // tm
    return pl.pallas_call(
        _inproj_kernel,
        grid=(T // tm,),
        in_specs=[pl.BlockSpec((tm, D), lambda i: (i, 0)),
                  pl.BlockSpec((1, 1, D), lambda i: (i // tpb, 0, 0)),
                  pl.BlockSpec((1, 1, D), lambda i: (i // tpb, 0, 0)),
                  pl.BlockSpec((D, 2 * di), lambda i: (0, 0))],
        out_specs=[pl.BlockSpec((tm, di), lambda i: (i, 0)),
                   pl.BlockSpec((tm, di), lambda i: (i, 0))],
        out_shape=[jax.ShapeDtypeStruct((T, di), BF16)] * 2,
        compiler_params=_params(("parallel",)),
        name="mlstm_inproj",
    )(x2, shift, scale, w_in)


def _qkv_kernel(xm_ref, halo_ref, cw_ref, cb_ref, wq_ref, wk_ref, wv_ref, wif_ref, bif_ref,
                q_ref, k_ref, v_ref, xc_ref, g_ref, *, tpb, nh, kscale):
    tm, di = xm_ref.shape
    dh = di // nh
    i = pl.program_id(0)
    xm = xm_ref[...]
    xf = xm.astype(F32)
    halo = halo_ref[...].astype(F32)
    halo = jnp.where(i % tpb == 0, 0.0, halo)
    ext = jnp.concatenate([halo, xf], axis=0)
    acc = cb_ref[...] + cw_ref[CONV_WIDTH - 1:CONV_WIDTH, :] * xf
    for s in range(1, CONV_WIDTH):
        acc = acc + cw_ref[CONV_WIDTH - 1 - s:CONV_WIDTH - s, :] * ext[CONV_HALO - s:CONV_HALO - s + tm, :]
    xc = acc * _sigmoid(acc)
    xcb = xc.astype(BF16)
    xc_ref[...] = xcb
    for h in range(nh):
        sl = slice(h * dh, (h + 1) * dh)
        q_ref[:, sl] = jnp.dot(xcb[:, sl], wq_ref[h], preferred_element_type=F32).astype(BF16)
        k_ref[:, sl] = (jnp.dot(xcb[:, sl], wk_ref[h], preferred_element_type=F32) * kscale).astype(BF16)
        v_ref[:, sl] = jnp.dot(xm[:, sl], wv_ref[h], preferred_element_type=F32).astype(BF16)
    g = jnp.dot(xcb, wif_ref[...], preferred_element_type=F32) + bif_ref[...]
    col = lax.broadcasted_iota(jnp.int32, g.shape, 1)
    logsig = jnp.minimum(g, 0.0) - jnp.log(1.0 + jnp.exp(-jnp.abs(g)))
    g = jnp.where(col >= nh, logsig, g)
    g_ref[...] = g[:, :2 * nh]


def _qkv(xm, conv_w, conv_b, wq, wk, wv, w_if, b_if, S):
    T, di = xm.shape
    nh = wq.shape[0]
    dh = di // nh
    tm = min(512, S)
    tpb = S // tm
    hb = tm // CONV_HALO
    wif = jnp.zeros((di, LANES), BF16).at[:, :2 * nh].set(w_if.astype(BF16))
    bif = jnp.zeros((1, LANES), F32).at[0, :2 * nh].set(b_if)
    kern = functools.partial(_qkv_kernel, tpb=tpb, nh=nh, kscale=float(dh) ** -0.5)
    full2 = lambda i: (0, 0)
    full3 = lambda i: (0, 0, 0)
    row = lambda i: (i, 0)
    return pl.pallas_call(
        kern,
        grid=(T // tm,),
        in_specs=[pl.BlockSpec((tm, di), row),
                  pl.BlockSpec((CONV_HALO, di), lambda i: (jnp.maximum(i * hb - 1, 0), 0)),
                  pl.BlockSpec((CONV_WIDTH, di), full2),
                  pl.BlockSpec((1, di), full2),
                  pl.BlockSpec((nh, dh, dh), full3),
                  pl.BlockSpec((nh, dh, dh), full3),
                  pl.BlockSpec((nh, dh, dh), full3),
                  pl.BlockSpec((di, LANES), full2),
                  pl.BlockSpec((1, LANES), full2)],
        out_specs=[pl.BlockSpec((tm, di), row)] * 4 + [pl.BlockSpec((tm, 2 * nh), row)],
        out_shape=[jax.ShapeDtypeStruct((T, di), BF16)] * 4 + [jax.ShapeDtypeStruct((T, 2 * nh), F32)],
        compiler_params=_params(("parallel",)),
        name="mlstm_qkv",
    )(xm, xm, conv_w, conv_b.reshape(1, di), wq, wk, wv, wif, bif)


def _mlstm_kernel(q_ref, k_ref, v_ref, gc_ref, gr_ref, o_ref, c_scr, n_scr, m_scr):
    L = q_ref.shape[0]
    nh = c_scr.shape[0]
    dh = c_scr.shape[1]

    @pl.when(pl.program_id(1) == 0)
    def _():
        c_scr[...] = jnp.zeros_like(c_scr)
        n_scr[...] = jnp.zeros_like(n_scr)
        m_scr[...] = jnp.zeros_like(m_scr)

    t_idx = lax.broadcasted_iota(jnp.int32, (L, L), 0)
    s_idx = lax.broadcasted_iota(jnp.int32, (L, L), 1)
    causal = s_idx <= t_idx
    for h in range(nh):
        hs = slice(h * dh, (h + 1) * dh)
        _mlstm_head(q_ref[:, hs], k_ref[:, hs], v_ref[:, hs], gc_ref[0, h], gr_ref[0, h],
                    o_ref.at[:, hs], c_scr.at[h], n_scr.at[h], m_scr.at[h], causal, t_idx, s_idx)


def _mlstm_head(q, k, v, gc, gr, o_ref, c_scr, n_scr, m_scr, causal, t_idx, s_idx):
    i_col, f_col = gc[:, 0:1], gc[:, 1:2]
    i_row, f_row = gr[0:1, :], gr[1:2, :]
    b_col = jnp.sum(jnp.where(causal, f_row, 0.0), axis=1, keepdims=True)
    b_row = jnp.sum(jnp.where(t_idx <= s_idx, f_col, 0.0), axis=0, keepdims=True)
    m_prev = m_scr[...]
    d = jnp.where(causal, b_col - b_row + i_row, -jnp.inf)
    a_col = b_col + m_prev
    m_t = jnp.maximum(a_col, jnp.max(d, axis=1, keepdims=True))
    w_intra = jnp.exp(d - m_t)
    w_inter = jnp.exp(a_col - m_t)
    qk = lax.dot_general(q, k, (((1,), (1,)), ((), ())), preferred_element_type=F32)
    s_mat = qk * w_intra
    c_b = c_scr[...].astype(BF16)
    inter = lax.dot_general(q, c_b, (((1,), (1,)), ((), ())), preferred_element_type=F32)
    num = jnp.dot(s_mat.astype(BF16), v, preferred_element_type=F32) + w_inter * inter
    qn = jnp.sum(q.astype(F32) * n_scr[...], axis=1, keepdims=True)
    den = jnp.sum(s_mat, axis=1, keepdims=True) + w_inter * qn
    hcap = num / jnp.maximum(jnp.abs(den), jnp.exp(-m_t))
    mu = jnp.mean(hcap, axis=1, keepdims=True)
    dv = hcap - mu
    var = jnp.mean(dv * dv, axis=1, keepdims=True)
    o_ref[...] = (dv * lax.rsqrt(var + LN_EPS)).astype(BF16)

    b_last = jnp.sum(f_row, axis=1, keepdims=True)
    ws_col = b_last - b_col + i_col
    m_new = jnp.maximum(b_last + m_prev, jnp.max(ws_col, axis=0, keepdims=True))
    decay = jnp.exp(b_last + m_prev - m_new)
    ws = jnp.exp(ws_col - m_new)
    vw = (v.astype(F32) * ws).astype(BF16)
    upd = lax.dot_general(vw, k, (((0,), (0,)), ((), ())), preferred_element_type=F32)
    c_scr[...] = decay * c_scr[...] + upd
    n_scr[...] = decay * n_scr[...] + jnp.sum(k.astype(F32) * ws, axis=0, keepdims=True)
    m_scr[...] = m_new


def _mlstm(q, k, v, gcol, grow, B, S):
    T, di = q.shape
    nh = gcol.shape[1]
    dh = di // nh
    L = min(MLSTM_CHUNK, S)
    nc = S // L
    blk = pl.BlockSpec((L, di), lambda b, c: (b * nc + c, 0))
    return pl.pallas_call(
        _mlstm_kernel,
        grid=(B, nc),
        in_specs=[blk, blk, blk,
                  pl.BlockSpec((1, nh, L, 2), lambda b, c: (b, 0, c, 0)),
                  pl.BlockSpec((1, nh, 2, L), lambda b, c: (b, 0, 0, c))],
        out_specs=blk,
        out_shape=jax.ShapeDtypeStruct((T, di), BF16),
        scratch_shapes=[pltpu.VMEM((nh, dh, dh), F32), pltpu.VMEM((nh, 1, dh), F32), pltpu.VMEM((nh, 1, 1), F32)],
        compiler_params=_params(("parallel", "arbitrary")),
        name="mlstm_scan",
    )(q, k, v, gcol, grow)


def _mlstm_out_kernel(hn_ref, xc_ref, z_ref, x_ref, gn_ref, skip_ref, w_ref, gate_ref, lg_ref, lb_ref, o_ref):
    z = z_ref[...].astype(F32)
    u = (hn_ref[...].astype(F32) * gn_ref[...] + skip_ref[...] * xc_ref[...].astype(F32)) * (z * _sigmoid(z))
    y = jnp.dot(u.astype(BF16), w_ref[...], preferred_element_type=F32)
    r = DEEPNORM_ALPHA * x_ref[...] + gate_ref[0] * y
    o_ref[...] = _layer_norm(r, lg_ref[...], lb_ref[...])


def _mlstm_out(hn, xc, z, x2, gn_g, skip, w_out, gate, ln_g, ln_b, S):
    T, D = x2.shape
    di = hn.shape[1]
    tm = min(512, S)
    tpb = S // tm
    row = lambda i: (i, 0)
    full2 = lambda i: (0, 0)
    return pl.pallas_call(
        _mlstm_out_kernel,
        grid=(T // tm,),
        in_specs=[pl.BlockSpec((tm, di), row), pl.BlockSpec((tm, di), row), pl.BlockSpec((tm, di), row),
                  pl.BlockSpec((tm, D), row),
                  pl.BlockSpec((1, di), full2), pl.BlockSpec((1, di), full2),
                  pl.BlockSpec((di, D), full2),
                  pl.BlockSpec((1, 1, D), lambda i: (i // tpb, 0, 0)),
                  pl.BlockSpec((1, D), full2), pl.BlockSpec((1, D), full2)],
        out_specs=pl.BlockSpec((tm, D), row),
        out_shape=jax.ShapeDtypeStruct((T, D), F32),
        compiler_params=_params(("parallel",)),
        name="mlstm_out_ln",
    )(hn, xc, z, x2, gn_g.reshape(1, di), skip.reshape(1, di), w_out, gate, ln_g.reshape(1, D), ln_b.reshape(1, D))


def _route_kernel(x_ref, sh_ref, sc_ref, wrt_ref, br_ref, tri_ref, out_ref, cnt_ref):
    E = wrt_ref.shape[0]
    ng = N_GROUPS
    epg = E // ng
    tm = x_ref.shape[0]
    h = x_ref[...] * (1.0 + sc_ref[0]) + sh_ref[0]
    nt = (((1,), (1,)), ((), ()))
    w = wrt_ref[...]
    w_hi = w.astype(BF16)
    w_lo = (w - w_hi.astype(F32)).astype(BF16)
    h_hi = h.astype(BF16)
    h_lo = (h - h_hi.astype(F32)).astype(BF16)
    lt = (lax.dot_general(w_hi, h_hi, nt, preferred_element_type=F32)
          + lax.dot_general(w_hi, h_lo, nt, preferred_element_type=F32)
          + lax.dot_general(w_lo, h_hi, nt, preferred_element_type=F32))
    ex = jnp.exp(lt - jnp.max(lt, axis=0, keepdims=True))
    probs = ex / jnp.sum(ex, axis=0, keepdims=True)
    sel = probs + br_ref[...]
    member = [sel[k * ng:(k + 1) * ng, :] for k in range(epg)]
    pmember = [probs[k * ng:(k + 1) * ng, :] for k in range(epg)]
    gscore = None
    for a in range(epg):
        for b in range(a + 1, epg):
            pair = member[a] + member[b]
            gscore = pair if gscore is None else jnp.maximum(gscore, pair)
    gidx = lax.broadcasted_iota(jnp.int32, (ng, tm), 0)
    gmax = jnp.max(gscore, axis=0, keepdims=True)
    chosen = gidx == jnp.min(jnp.where(gscore == gmax, gidx, ng), axis=0, keepdims=True)
    zero_g = jnp.zeros((ng, tm), F32)
    e1, e2, p1, p2 = zero_g, zero_g, zero_g, zero_g
    firsts, seconds = [], []
    for k in range(epg):
        rank = zero_g
        for o in range(epg):
            if o != k:
                beats = (member[o] >= member[k]) if o < k else (member[o] > member[k])
                rank = rank + jnp.where(beats, 1.0, 0.0)
        is1 = chosen & (rank == 0.0)
        is2 = chosen & (rank == 1.0)
        firsts.append(jnp.where(is1, 1.0, 0.0))
        seconds.append(jnp.where(is2, 1.0, 0.0))
        eid = (gidx * epg + k).astype(F32)
        e1 = e1 + jnp.where(is1, eid, 0.0)
        e2 = e2 + jnp.where(is2, eid, 0.0)
        p1 = p1 + jnp.where(is1, pmember[k], 0.0)
        p2 = p2 + jnp.where(is2, pmember[k], 0.0)
    e1, e2, p1, p2 = [jnp.sum(v, axis=0, keepdims=True) for v in (e1, e2, p1, p2)]
    first = jnp.concatenate(firsts, axis=0)
    second = jnp.concatenate(seconds, axis=0)
    mask = first + second
    prefix = jnp.dot(mask.astype(BF16), tri_ref[...], preferred_element_type=F32)
    counts = jnp.sum(mask, axis=1, keepdims=True)
    padded = jnp.floor((counts + (RUN_ALIGN - 1)) * (1.0 / RUN_ALIGN)) * RUN_ALIGN
    row_of = [(e % epg) * ng + e // epg for e in range(E)]
    starts = [None] * E
    acc = jnp.zeros((1, 1), F32)
    for e in range(E):
        starts[row_of[e]] = acc
        acc = acc + padded[row_of[e]:row_of[e] + 1, :]
    slot = prefix + jnp.concatenate(starts, axis=0)
    s1 = jnp.sum(first * slot, axis=0, keepdims=True)
    s2 = jnp.sum(second * slot, axis=0, keepdims=True)
    psum = p1 + p2
    zero = jnp.zeros((1, tm), F32)
    out_ref[...] = jnp.concatenate([e1, e2, p1 / psum, p2 / psum, s1, s2, zero, zero], axis=0)
    counts_by_expert = jnp.concatenate([counts[row_of[e]:row_of[e] + 1, :] for e in range(E)], axis=0)
    cnt_ref[...] = jnp.broadcast_to(counts_by_expert, cnt_ref.shape)


def _route(x2, shift, scale, w_router, b_router, S):
    T, D = x2.shape
    E = w_router.shape[1]
    tm = min(MOE_TILE, S)
    tpb = S // tm
    tri = (jnp.arange(tm)[:, None] < jnp.arange(tm)[None, :]).astype(BF16)
    epg = E // N_GROUPS
    rows = jnp.array([g * epg + k for k in range(epg) for g in range(N_GROUPS)], jnp.int32)
    return pl.pallas_call(
        _route_kernel,
        grid=(T // tm,),
        in_specs=[pl.BlockSpec((tm, D), lambda i: (i, 0)),
                  pl.BlockSpec((1, 1, D), lambda i: (i // tpb, 0, 0)),
                  pl.BlockSpec((1, 1, D), lambda i: (i // tpb, 0, 0)),
                  pl.BlockSpec((E, D), lambda i: (0, 0)),
                  pl.BlockSpec((E, 1), lambda i: (0, 0)),
                  pl.BlockSpec((tm, tm), lambda i: (0, 0))],
        out_specs=[pl.BlockSpec((8, tm), lambda i: (0, i)),
                   pl.BlockSpec((E, LANES), lambda i: (0, i))],
        out_shape=[jax.ShapeDtypeStruct((8, T), F32), jax.ShapeDtypeStruct((E, (T // tm) * LANES), F32)],
        compiler_params=_params(("parallel",)),
        name="moe_route",
    )(x2, shift, scale, w_router.T[rows], b_router[rows].reshape(E, 1), tri)


def _stage_rows(tm, n_exp):
    return MOE_TOPK * tm + n_exp * RUN_ALIGN


def _chunk_copies(tc_ref, ts_ref, ds_ref, step, n_exp, make):
    out = []
    for e in range(n_exp):
        cnt = tc_ref[step * n_exp + e]
        off = ts_ref[step * n_exp + e]
        dst = ds_ref[step * n_exp + e]
        for k in reversed(range(RUN_ALIGN.bit_length() - 1, CHUNK_BITS)):
            done = (cnt >> (k + 1)) << (k + 1)
            out.append(((cnt & (1 << k)) != 0,
                        make(pl.multiple_of(off + done, RUN_ALIGN), pl.multiple_of(dst + done, RUN_ALIGN), 1 << k)))
    return out


def _dispatch_kernel(tc_ref, ts_ref, ds_ref, pe_ref, x_ref, sh_ref, sc_ref, pos_ref, xs_ref,
                     stage, zbuf, sem, zsem):
    tm = x_ref.shape[0]
    rb = zbuf.shape[0]
    i = pl.program_id(0)
    slot = i % 2

    n_exp = pe_ref.shape[0] - 1
    n_blocks = xs_ref.shape[0] // rb

    def pad_copy(e):
        return pltpu.make_async_copy(zbuf, xs_ref.at[pl.ds(pl.multiple_of(pe_ref[e + 1] - rb, rb), rb)], zsem)

    def tail_copy(j):
        return pltpu.make_async_copy(zbuf, xs_ref.at[pl.ds(pl.multiple_of(pe_ref[n_exp] + j * rb, rb), rb)], zsem)

    def tail_live(j):
        return pe_ref[n_exp] // rb + j < n_blocks

    @pl.when(i == 0)
    def _():
        zbuf[...] = jnp.zeros_like(zbuf)
        for e in range(n_exp):
            @pl.when(pe_ref[e + 1] > pe_ref[e])
            def _():
                pad_copy(e).start()
        for j in range(n_exp):
            @pl.when(tail_live(j))
            def _():
                tail_copy(j).start()
        for e in range(n_exp):
            @pl.when(pe_ref[e + 1] > pe_ref[e])
            def _():
                pad_copy(e).wait()
        for j in range(n_exp):
            @pl.when(tail_live(j))
            def _():
                tail_copy(j).wait()

    n_tiles = pl.num_programs(0) - 1

    @pl.when(i < n_tiles)
    def _():
        h = (x_ref[...] * (1.0 + sc_ref[0]) + sh_ref[0]).astype(BF16)
        pos = pos_ref[0]
        slot_id = lax.broadcasted_iota(jnp.int32, (stage.shape[1], tm), 0)
        perm = jnp.where((slot_id == pos[0:1, :]) | (slot_id == pos[1:2, :]), 1.0, 0.0).astype(BF16)
        stage[slot] = jnp.dot(perm, h, preferred_element_type=F32)

    def copies(step, s):
        def make(src_row, dst_row, size):
            return pltpu.make_async_copy(stage.at[s, pl.ds(src_row, size)], xs_ref.at[pl.ds(dst_row, size)], sem.at[s])
        return _chunk_copies(tc_ref, ts_ref, ds_ref, step, n_exp, make)

    for live, cp in copies(jnp.minimum(i, n_tiles - 1), slot):
        pl.when(live & (i < n_tiles))(cp.start)
    for live, cp in copies(jnp.maximum(i - 1, 0), 1 - slot):
        pl.when(live & (i > 0))(cp.wait)


def _dispatch(x2, shift, scale, posrow, tcnt, tstart, dstart, pends0, cap, S):
    T, D = x2.shape
    tm = min(MOE_TILE, S)
    tpb = S // tm
    nt = T // tm
    tile = lambda i: jnp.minimum(i, nt - 1)
    mod = pl.BlockSpec((1, 1, D), lambda i, *_: (tile(i) // tpb, 0, 0))
    grid_spec = pltpu.PrefetchScalarGridSpec(
        num_scalar_prefetch=4,
        grid=(nt + 1,),
        in_specs=[pl.BlockSpec((tm, D), lambda i, *_: (tile(i), 0)), mod, mod,
                  pl.BlockSpec((1, 2, tm), lambda i, *_: (tile(i), 0, 0))],
        out_specs=pl.BlockSpec(memory_space=pl.ANY),
        scratch_shapes=[pltpu.VMEM((2, _stage_rows(tm, pends0.shape[0] - 1), D), F32),
                        pltpu.VMEM((MOE_ROW_BLOCK, D), F32),
                        pltpu.SemaphoreType.DMA((2,)), pltpu.SemaphoreType.DMA(())],
    )
    return pl.pallas_call(
        _dispatch_kernel,
        grid_spec=grid_spec,
        out_shape=jax.ShapeDtypeStruct((cap, D), F32),
        compiler_params=_params(("arbitrary",)),
        name="moe_dispatch",
    )(tcnt, tstart, dstart, pends0, x2, shift, scale, posrow)


def _expert_kernel(be_ref, na_ref, xs_ref, wg_ref, wu_ref, wd_ref, ys_ref, wg_b, wu_b, wd_b):
    i = pl.program_id(0)
    active = i < na_ref[0]

    @pl.when(active & ((i == 0) | (be_ref[i] != be_ref[jnp.maximum(i - 1, 0)])))
    def _():
        wg_b[...] = wg_ref[0, 0].astype(BF16)
        wu_b[...] = wu_ref[0, 0].astype(BF16)
        wd_b[...] = wd_ref[0, 0].astype(BF16)

    @pl.when(active)
    def _():
        xb = xs_ref[...].astype(BF16)
        g = jnp.dot(xb, wg_b[...], preferred_element_type=F32)
        u = jnp.dot(xb, wu_b[...], preferred_element_type=F32)
        a = (g * _sigmoid(g)) * u
        ys_ref[...] = jnp.dot(a.astype(BF16), wd_b[...], preferred_element_type=F32)

    @pl.when(jnp.logical_not(active))
    def _():
        ys_ref[...] = jnp.zeros_like(ys_ref)


def _experts(xs, block_expert, n_active, w_gate, w_up, w_down, layer):
    cap, D = xs.shape
    _, E, _, F = w_gate.shape
    rb = MOE_ROW_BLOCK
    nb = cap // rb

    def blk(i, be, na):
        return jnp.maximum(jnp.minimum(i, na[0] - 1), 0)

    grid_spec = pltpu.PrefetchScalarGridSpec(
        num_scalar_prefetch=2,
        grid=(nb,),
        in_specs=[pl.BlockSpec((rb, D), lambda i, be, na: (blk(i, be, na), 0)),
                  pl.BlockSpec((1, 1, D, F), lambda i, be, na: (layer, be[blk(i, be, na)], 0, 0)),
                  pl.BlockSpec((1, 1, D, F), lambda i, be, na: (layer, be[blk(i, be, na)], 0, 0)),
                  pl.BlockSpec((1, 1, F, D), lambda i, be, na: (layer, be[blk(i, be, na)], 0, 0))],
        out_specs=pl.BlockSpec((rb, D), lambda i, be, na: (i, 0)),
        scratch_shapes=[pltpu.VMEM((D, F), BF16), pltpu.VMEM((D, F), BF16), pltpu.VMEM((F, D), BF16)],
    )
    return pl.pallas_call(
        _expert_kernel,
        grid_spec=grid_spec,
        out_shape=jax.ShapeDtypeStruct((cap, D), F32),
        compiler_params=_params(("arbitrary",)),
        name="moe_experts",
    )(block_expert, n_active, xs, w_gate, w_up, w_down)


def _combine_kernel(tc_ref, ts_ref, ds_ref, ys_ref, rt_ref, x_ref, gate_ref, lg_ref, lb_ref, o_ref, stage, sem,
                    *, n_exp):
    tm = x_ref.shape[0]
    i = pl.program_id(0)
    n_tiles = pl.num_programs(0) - 1
    slot = i % 2

    def copies(step, s):
        def make(stage_row, ys_row, size):
            return pltpu.make_async_copy(ys_ref.at[pl.ds(ys_row, size)], stage.at[s, pl.ds(stage_row, size)], sem.at[s])
        return _chunk_copies(tc_ref, ts_ref, ds_ref, step, n_exp, make)

    @pl.when(i == 0)
    def _():
        stage[...] = jnp.zeros_like(stage)

    for live, cp in copies(jnp.minimum(i, n_tiles - 1), slot):
        pl.when(live & (i < n_tiles))(cp.start)
    for live, cp in copies(jnp.maximum(i - 1, 0), 1 - slot):
        pl.when(live & (i > 0))(cp.wait)

    @pl.when(i > 0)
    def _():
        rt = rt_ref[...]
        ysb = stage[1 - slot].astype(BF16)
        slot_id = lax.broadcasted_iota(jnp.int32, (tm, stage.shape[1]), 1).astype(F32)
        y = None
        for j in range(MOE_TOPK):
            perm = jnp.where(slot_id == rt[:, 4 + j:5 + j], 1.0, 0.0).astype(BF16)
            part = rt[:, 2 + j:3 + j] * jnp.dot(perm, ysb, preferred_element_type=F32)
            y = part if y is None else y + part
        r = DEEPNORM_ALPHA * x_ref[...] + gate_ref[0] * y
        o_ref[...] = _layer_norm(r, lg_ref[...], lb_ref[...])


def _combine(ys, route_t, x2, gate, ln_g, ln_b, tcnt, tstart, dstart, S):
    T, D = x2.shape
    tm = min(MOE_TILE, S)
    tpb = S // tm
    tile = lambda i: jnp.maximum(i - 1, 0)
    grid_spec = pltpu.PrefetchScalarGridSpec(
        num_scalar_prefetch=3,
        grid=(T // tm + 1,),
        in_specs=[pl.BlockSpec(memory_space=pl.ANY),
                  pl.BlockSpec((tm, 8), lambda i, *_: (tile(i), 0)),
                  pl.BlockSpec((tm, D), lambda i, *_: (tile(i), 0)),
                  pl.BlockSpec((1, 1, D), lambda i, *_: (tile(i) // tpb, 0, 0)),
                  pl.BlockSpec((1, D), lambda i, *_: (0, 0)),
                  pl.BlockSpec((1, D), lambda i, *_: (0, 0))],
        out_specs=pl.BlockSpec((tm, D), lambda i, *_: (tile(i), 0)),
        scratch_shapes=[pltpu.VMEM((2, _stage_rows(tm, tcnt.shape[0] // (T // tm)), D), F32),
                        pltpu.SemaphoreType.DMA((2,))],
    )
    return pl.pallas_call(
        functools.partial(_combine_kernel, n_exp=tcnt.shape[0] // (T // tm)),
        grid_spec=grid_spec,
        out_shape=jax.ShapeDtypeStruct((T, D), F32),
        compiler_params=_params(("arbitrary",)),
        name="moe_combine_ln",
    )(tcnt, tstart, dstart, ys, route_t, x2, gate, ln_g.reshape(1, D), ln_b.reshape(1, D))


def _moe_layer(x2, shift, scale, gate, ln_g, ln_b, w_router, b_router, w_gate, w_up, w_down, layer, S):
    T, D = x2.shape
    E = w_router.shape[1]
    rb = MOE_ROW_BLOCK
    tm = min(MOE_TILE, S)
    nt = T // tm
    assert tm < (1 << CHUNK_BITS)
    route, cnt = _route(x2, shift, scale, w_router, b_router, S)
    tcnt = cnt.reshape(E, nt, LANES)[:, :, 0].T.astype(jnp.int32)
    tcnt = (tcnt + RUN_ALIGN - 1) // RUN_ALIGN * RUN_ALIGN
    base = jnp.cumsum(tcnt, axis=0) - tcnt
    counts = jnp.sum(tcnt, axis=0)
    padded = (counts + rb - 1) // rb * rb
    pends = jnp.cumsum(padded)
    pstarts = pends - padded
    tstart = jnp.cumsum(tcnt, axis=1) - tcnt
    dstart = pstarts[None, :] + base
    posrow = jnp.transpose(route[4:6].astype(jnp.int32).reshape(2, nt, tm), (1, 0, 2))
    nb = -(-(T * MOE_TOPK + nt * E * (RUN_ALIGN - 1)) // rb) + E
    cap = nb * rb
    block_start = jnp.arange(nb, dtype=jnp.int32) * rb
    block_expert = jnp.minimum(jnp.sum(block_start[:, None] >= pends[None, :], axis=1), E - 1).astype(jnp.int32)
    n_active = (pends[-1:] // rb).astype(jnp.int32)
    pends0 = jnp.concatenate([jnp.zeros((1,), jnp.int32), pends.astype(jnp.int32)])
    tables = [t.reshape(-1).astype(jnp.int32) for t in (tcnt, tstart, dstart)]
    xs = _dispatch(x2, shift, scale, posrow, *tables, pends0, cap, S)
    ys = _experts(xs, block_expert, n_active, w_gate, w_up, w_down, layer)
    return _combine(ys, route.T, x2, gate, ln_g, ln_b, *tables, S)


def _moba_proj_kernel(x_ref, shq_ref, scq_ref, shkv_ref, sckv_ref, wq_ref, wkv_ref,
                      q_ref, k_ref, vt_ref, km_ref, *, qscale, tpb, slopes):
    tm, D = x_ref.shape
    hd = HEAD_DIM
    nh = D // hd
    x = x_ref[...]
    hq = x * (1.0 + scq_ref[0]) + shq_ref[0]
    hkv = x * (1.0 + sckv_ref[0]) + shkv_ref[0]
    qf = jnp.dot(hq.astype(BF16), wq_ref[...], preferred_element_type=F32) * qscale
    kv = jnp.dot(hkv.astype(BF16), wkv_ref[...], preferred_element_type=F32)
    kf = kv[:, :D]
    vt = kv[:, D:].T
    ones_pad = jnp.where(lax.broadcasted_iota(jnp.int32, (V_ROWS - hd, tm), 0) == 0, 1.0, 0.0)
    for g in range(nh):
        vt_ref[0, g * V_ROWS:(g + 1) * V_ROWS, :] = jnp.concatenate(
            [vt[g * hd:(g + 1) * hd, :], ones_pad], axis=0).astype(BF16)
    lane = lax.broadcasted_iota(jnp.int32, (1, LANES), 1)
    head_lanes = lane >= hd
    pos = (pl.program_id(0) % tpb) * tm + lax.broadcasted_iota(jnp.int32, (tm, 1), 0)
    pos_f = pos.astype(F32)
    onehot = jnp.where(lane == pos // MOBA_BLOCK, 1.0, 0.0)
    q_fill = jnp.where((lane >= BIAS_LANE) & (lane < BIAS_LANE + 3), 1.0, 0.0)
    nbt = tm // MOBA_BLOCK
    means = [jnp.mean(kf[j * MOBA_BLOCK:(j + 1) * MOBA_BLOCK, :], axis=0, keepdims=True) for j in range(nbt)]
    km = jnp.concatenate(means + [jnp.zeros((8 - nbt, D), F32)], axis=0)
    for j in range(nh // 2):
        cs = slice(j * LANES, (j + 1) * LANES)
        qb, kb, mb = qf[:, cs], kf[:, cs], km[:, cs]
        q_pair = (pltpu.roll(qb, hd, axis=1), qb)
        k_pair = (pltpu.roll(kb, hd, axis=1), kb)
        m_pair = (pltpu.roll(mb, hd, axis=1), mb)
        for e in range(2):
            g = 2 * j + e
            gs = slice(g * LANES, (g + 1) * LANES)
            q_ref[:, gs] = jnp.where(head_lanes, q_pair[e], q_fill).astype(BF16)
            c = (slopes[g] * LOG2E) * pos_f
            hi = c.astype(BF16).astype(F32)
            mid = (c - hi).astype(BF16).astype(F32)
            lo = c - hi - mid
            aug = jnp.where(lane == BIAS_LANE, hi,
                            jnp.where(lane == BIAS_LANE + 1, mid,
                                      jnp.where(lane == BIAS_LANE + 2, lo, onehot)))
            k_ref[:, gs] = jnp.where(head_lanes, k_pair[e], aug).astype(BF16)
            mg = jnp.where(head_lanes, m_pair[e], 0.0)
            for jb in range(nbt):
                km_ref[jb, :, gs] = mg[jb:jb + 1, :]


def _moba_proj(x2, shq, scq, shkv, sckv, wq, wkv, B, S):
    T, D = x2.shape
    nh = D // HEAD_DIM
    tm = min(512, S)
    tpb = S // tm
    nbt = tm // MOBA_BLOCK
    assert S // MOBA_BLOCK <= BIAS_LANE and nbt <= 8
    slopes = tuple(2.0 ** (-8.0 * (h + 1.0) / nh) for h in range(nh))
    mod = pl.BlockSpec((1, 1, D), lambda i: (i // tpb, 0, 0))
    kern = functools.partial(_moba_proj_kernel, qscale=float(HEAD_DIM) ** -0.5 * LOG2E, tpb=tpb, slopes=slopes)
    return pl.pallas_call(
        kern,
        grid=(T // tm,),
        in_specs=[pl.BlockSpec((tm, D), lambda i: (i, 0)), mod, mod, mod, mod,
                  pl.BlockSpec((D, D), lambda i: (0, 0)),
                  pl.BlockSpec((D, 2 * D), lambda i: (0, 0))],
        out_specs=[pl.BlockSpec((tm, nh * LANES), lambda i: (i, 0)),
                   pl.BlockSpec((tm, nh * LANES), lambda i: (i, 0)),
                   pl.BlockSpec((1, nh * V_ROWS, tm), lambda i: (i // tpb, 0, i % tpb)),
                   pl.BlockSpec((nbt, 1, nh * LANES), lambda i: (i, 0, 0))],
        out_shape=[jax.ShapeDtypeStruct((T, nh * LANES), BF16), jax.ShapeDtypeStruct((T, nh * LANES), BF16),
                   jax.ShapeDtypeStruct((B, nh * V_ROWS, S), BF16),
                   jax.ShapeDtypeStruct((T // MOBA_BLOCK, 1, nh * LANES), F32)],
        compiler_params=_params(("parallel",)),
        name="moba_proj",
    )(x2, shq, scq, shkv, sckv, wq, wkv)


def _moba_select_kernel(q_ref, km_ref, qat_ref):
    tq = q_ref.shape[0]
    G = q_ref.shape[1] // LANES
    nblk = km_ref.shape[1]
    blk_id = lax.broadcasted_iota(jnp.int32, (nblk, tq), 0)
    own = (pl.program_id(2) * tq + lax.broadcasted_iota(jnp.int32, (1, tq), 1)) // MOBA_BLOCK
    for g in range(G):
        gs = slice(g * LANES, (g + 1) * LANES)
        qgt = q_ref[:, gs].astype(F32).T
        qb = qgt.astype(BF16)
        km = km_ref[0, :, gs]
        km_hi = km.astype(BF16)
        km_mid = (km - km_hi.astype(F32)).astype(BF16)
        km_lo = (km - km_hi.astype(F32) - km_mid.astype(F32)).astype(BF16)
        gate = (jnp.dot(km_hi, qb, preferred_element_type=F32) + jnp.dot(km_mid, qb, preferred_element_type=F32)
                + jnp.dot(km_lo, qb, preferred_element_type=F32))
        gsc = jnp.where(blk_id < own, gate, -jnp.inf)
        picked = blk_id < 0
        for _ in range(MOBA_TOPK):
            mx = jnp.max(gsc, axis=0, keepdims=True)
            first = jnp.min(jnp.where(gsc == mx, blk_id, nblk), axis=0, keepdims=True)
            pick = blk_id == first
            picked = picked | pick
            gsc = jnp.where(pick, -jnp.inf, gsc)
        visible = (picked & (blk_id < own)) | (blk_id == own)
        pen = jnp.where(visible, 0.0, NEG)
        qat_ref[0, g] = (qgt + jnp.concatenate([pen, jnp.zeros((LANES - nblk, tq), F32)], axis=0)).astype(BF16)


def _moba_select(q, km, B, S):
    nh = q.shape[1] // LANES
    G = MOBA_HEADS_PER_STEP
    tq = min(512, S)
    nblk = S // MOBA_BLOCK
    return pl.pallas_call(
        _moba_select_kernel,
        grid=(B, nh // G, S // tq),
        in_specs=[pl.BlockSpec((tq, G * LANES), lambda b, hp, i: (b * (S // tq) + i, hp)),
                  pl.BlockSpec((1, nblk, G * LANES), lambda b, hp, i: (b, 0, hp))],
        out_specs=pl.BlockSpec((1, G, LANES, tq), lambda b, hp, i: (b, hp, 0, i)),
        out_shape=jax.ShapeDtypeStruct((B, nh, LANES, S), BF16),
        compiler_params=_params(("parallel", "parallel", "parallel")),
        name="moba_select",
    )(q, km)


def _moba_kernel(qat_ref, qat_next_ref, k_ref, vt_ref, o_ref, st_scr, m_scr, acc_scr):
    G = qat_ref.shape[1]
    BLK = qat_ref.shape[3]
    hd = HEAD_DIM
    nhalf = BLK // LANES
    own = pl.program_id(2)
    k_off = lax.broadcasted_iota(jnp.int32, (BLK, 1), 0)
    q_off = lax.broadcasted_iota(jnp.int32, (1, LANES), 1)

    m_scr[...] = jnp.full(m_scr.shape, NEG, F32)
    acc_scr[...] = jnp.zeros_like(acc_scr)

    def scores(n, slot, q_ref=qat_ref):
        start = pl.multiple_of(n * BLK, BLK)
        for g in range(G):
            kn = k_ref[pl.ds(start, BLK), g * LANES:(g + 1) * LANES]
            st_scr[slot, g] = jnp.dot(kn, q_ref[0, g], preferred_element_type=F32)

    def update(n, slot, causal):
        start = pl.multiple_of(n * BLK, BLK)
        for g in range(G):
            alphas, ps = [], []
            for hf in range(nhalf):
                c = g * nhalf + hf
                ls = slice(hf * LANES, (hf + 1) * LANES)
                st = st_scr[slot, g, :, ls]
                if causal:
                    st = jnp.where(k_off <= q_off + hf * LANES, st, NEG)
                m = m_scr[c]
                m_new = jnp.maximum(m, jnp.max(st, axis=0, keepdims=True))
                alphas.append(jnp.exp2(m - m_new))
                ps.append(jnp.exp2(st - m_new).astype(BF16))
                m_scr[c] = m_new
            vtn = vt_ref[0, g * V_ROWS:(g + 1) * V_ROWS, pl.ds(start, BLK)]
            pv = jnp.dot(vtn, jnp.concatenate(ps, axis=1), preferred_element_type=F32)
            acc_scr[g] = jnp.concatenate(alphas, axis=1) * acc_scr[g] + pv

    @pl.when(own == 0)
    def _():
        scores(0, 0)

    def pair(n):
        scores(n + 1, 1)
        update(n, 0, False)
        scores(n + 2, 0)
        update(n + 1, 1, False)

    def two_pairs(i, carry):
        pair(4 * i)
        pair(4 * i + 2)
        return carry

    def one_pair(i, carry):
        pair(4 * (own // 4) + 2 * i)
        return carry

    lax.fori_loop(0, own // 4, two_pairs, 0)
    lax.fori_loop(0, (own % 4) // 2, one_pair, 0)

    @pl.when(own % 2 == 0)
    def _():
        update(own, 0, True)
        scores(0, 0, qat_next_ref)

    @pl.when(own % 2 == 1)
    def _():
        scores(own, 1)
        update(own - 1, 0, False)
        scores(0, 0, qat_next_ref)
        update(own, 1, True)
    for g in range(G):
        acc = acc_scr[g]
        o_ref[0, g * hd:(g + 1) * hd, :] = (acc[:hd, :] * (1.0 / acc[hd:hd + 1, :])).astype(BF16)


def _moba_attn(qat, k, vt, B, S):
    nh = qat.shape[1]
    G = MOBA_HEADS_PER_STEP
    BLK = MOBA_BLOCK
    nblk = S // BLK
    hd = HEAD_DIM
    return pl.pallas_call(
        _moba_kernel,
        grid=(B, nh // G, nblk),
        in_specs=[pl.BlockSpec((1, G, LANES, BLK), lambda b, hp, qi: (b, hp, 0, qi)),
                  pl.BlockSpec((1, G, LANES, BLK), lambda b, hp, qi: (b, hp, 0, jnp.minimum(qi + 1, nblk - 1))),
                  pl.BlockSpec((S, G * LANES), lambda b, hp, qi: (b, hp)),
                  pl.BlockSpec((1, G * V_ROWS, S), lambda b, hp, qi: (b, hp, 0))],
        out_specs=pl.BlockSpec((1, G * hd, BLK), lambda b, hp, qi: (b, hp, qi)),
        out_shape=jax.ShapeDtypeStruct((B, nh * hd, S), BF16),
        scratch_shapes=[pltpu.VMEM((2, G, BLK, BLK), F32),
                        pltpu.VMEM((G * (BLK // LANES), 1, LANES), F32),
                        pltpu.VMEM((G, V_ROWS, BLK), F32)],
        compiler_params=_params(("parallel", "parallel", "arbitrary")),
        name="moba_attn",
    )(qat, qat, k, vt)


def _attn_out_kernel(a_ref, x_ref, w_ref, gate_ref, lg_ref, lb_ref, o_ref):
    y = lax.dot_general(a_ref[0], w_ref[...], (((0,), (0,)), ((), ())), preferred_element_type=F32)
    r = DEEPNORM_ALPHA * x_ref[...] + gate_ref[0] * y
    o_ref[...] = _layer_norm(r, lg_ref[...], lb_ref[...])


def _attn_out(at, x2, w, gate, ln_g, ln_b, S):
    T, D = x2.shape
    K = at.shape[1]
    tm = min(512, S)
    tpb = S // tm
    row = lambda i: (i, 0)
    full2 = lambda i: (0, 0)
    return pl.pallas_call(
        _attn_out_kernel,
        grid=(T // tm,),
        in_specs=[pl.BlockSpec((1, K, tm), lambda i: (i // tpb, 0, i % tpb)), pl.BlockSpec((tm, D), row),
                  pl.BlockSpec((K, D), full2),
                  pl.BlockSpec((1, 1, D), lambda i: (i // tpb, 0, 0)),
                  pl.BlockSpec((1, D), full2), pl.BlockSpec((1, D), full2)],
        out_specs=pl.BlockSpec((tm, D), row),
        out_shape=jax.ShapeDtypeStruct((T, D), F32),
        compiler_params=_params(("parallel",)),
        name="attn_out_ln",
    )(at, x2, w, gate, ln_g.reshape(1, D), ln_b.reshape(1, D))


def kernel(x, c, w_ada, b_ada, ln_g, ln_b, a_w_in, a_conv_w, a_conv_b, a_wq, a_wk, a_wv, a_w_if, a_b_if,
           a_gn_g, a_skip, a_w_out, b_w_kv, b_wq, b_wo, moe_w_router, moe_b_router, moe_w_gate, moe_w_up,
           moe_w_down):
    B, S, D = x.shape
    T = B * S
    assert DEPTH == 2 and S % MOBA_BLOCK == 0 and S % MLSTM_CHUNK == 0
    n_layer_mod = DEPTH * N_MOD_PER_LAYER * D
    cond = _ada_cond(c, w_ada, b_ada)
    mods = cond[:, :n_layer_mod].reshape(B, DEPTH, 2, 3, 1, D)
    kv_mod = cond[:, n_layer_mod:].reshape(B, 2, 1, D)

    def mod3(layer, sub):
        m = mods[:, layer, sub]
        return m[:, 0], m[:, 1], 1.0 + m[:, 2]

    xf = x.reshape(T, D)

    shift, scale, gate = mod3(0, 0)
    nh = MLSTM_HEADS
    xm, z = _inproj(xf, shift, scale, a_w_in[0].astype(BF16), S)
    q, k, v, xc, gts = _qkv(xm, a_conv_w[0], a_conv_b[0], a_wq[0].astype(BF16), a_wk[0].astype(BF16),
                            a_wv[0].astype(BF16), a_w_if[0], a_b_if[0], S)
    g4 = gts.reshape(B, S, 2, nh)
    gcol = jnp.transpose(g4, (0, 3, 1, 2))
    grow = jnp.transpose(g4, (0, 3, 2, 1))
    hn = _mlstm(q, k, v, gcol, grow, B, S)
    xf = _mlstm_out(hn, xc, z, xf, a_gn_g[0], a_skip[0], a_w_out[0].astype(BF16), gate, ln_g[0, 0], ln_b[0, 0], S)
    shift, scale, gate = mod3(0, 1)
    xf = _moe_layer(xf, shift, scale, gate, ln_g[0, 1], ln_b[0, 1], moe_w_router, moe_b_router,
                    moe_w_gate, moe_w_up, moe_w_down, 0, S)

    shift, scale, gate = mod3(1, 0)
    q, k, vt, km = _moba_proj(xf, shift, scale, kv_mod[:, 0], kv_mod[:, 1], b_wq[0].astype(BF16),
                              b_w_kv.astype(BF16), B, S)
    km = km.reshape(B, S // MOBA_BLOCK, km.shape[-1])
    attn_t = _moba_attn(_moba_select(q, km, B, S), k, vt, B, S)
    xf = _attn_out(attn_t, xf, b_wo[0].astype(BF16), gate, ln_g[1, 0], ln_b[1, 0], S)
    shift, scale, gate = mod3(1, 1)
    xf = _moe_layer(xf, shift, scale, gate, ln_g[1, 1], ln_b[1, 1], moe_w_router, moe_b_router,
                    moe_w_gate, moe_w_up, moe_w_down, 1, S)
    return xf.reshape(B, S, D)
```

```python
import functools

import jax
import jax.numpy as jnp
from jax import lax
from jax.experimental import pallas as pl
from jax.experimental.pallas import tpu as pltpu

DEPTH = 2
MLSTM_HEADS = 4
CONV_WIDTH = 4
MLSTM_CHUNK = 512
MOBA_BLOCK = 256
MOBA_TOPK = 3
N_GROUPS = 4
MOE_TOPK = 2
MOE_ROW_BLOCK = 256
MOE_TILE = 512
CHUNK_BITS = 10
RUN_ALIGN = 8
DEEPNORM_ALPHA = (2.0 * DEPTH) ** 0.25
LN_EPS = 1e-5
N_MOD_PER_LAYER = 6

HEAD_DIM = 64
V_ROWS = 80
MOBA_HEADS_PER_STEP = 4
BIAS_LANE = 32
LOG2E = 1.4426950408889634

LANES = 128
CONV_HALO = 16
NEG = -1e30
VMEM_LIMIT = 56 * 1024 * 1024

F32 = jnp.float32
BF16 = jnp.bfloat16


def _sigmoid(x):
    return 1.0 / (1.0 + jnp.exp(-x))


def _params(sem, vmem=VMEM_LIMIT):
    return pltpu.CompilerParams(dimension_semantics=sem, vmem_limit_bytes=vmem)


def _layer_norm(r, g, b):
    mu = jnp.mean(r, axis=-1, keepdims=True)
    d = r - mu
    var = jnp.mean(d * d, axis=-1, keepdims=True)
    return d * lax.rsqrt(var + LN_EPS) * g + b


def _ada_kernel(c_ref, w_ref, b_ref, o_ref):
    c = c_ref[...]
    s = c * _sigmoid(c)
    o_ref[...] = jnp.dot(s.astype(BF16), w_ref[...].astype(BF16),
                         preferred_element_type=F32) + b_ref[...]


def _ada_cond(c, w_ada, b_ada):
    B, D = c.shape
    N = w_ada.shape[1]
    tn = 2048 if N % 2048 == 0 else N
    cp = jnp.zeros((8, D), F32).at[:B].set(c)
    out = pl.pallas_call(
        _ada_kernel,
        grid=(N // tn,),
        in_specs=[pl.BlockSpec((8, D), lambda j: (0, 0)),
                  pl.BlockSpec((D, tn), lambda j: (0, j)),
                  pl.BlockSpec((1, tn), lambda j: (0, j))],
        out_specs=pl.BlockSpec((8, tn), lambda j: (0, j)),
        out_shape=jax.ShapeDtypeStruct((8, N), F32),
        compiler_params=_params(("arbitrary",)),
        name="ada_cond",
    )(cp, w_ada, b_ada.reshape(1, N))
    return out[:B]


def _inproj_kernel(x_ref, sh_ref, sc_ref, w_ref, xm_ref, z_ref):
    di = xm_ref.shape[-1]
    h = x_ref[...] * (1.0 + sc_ref[0]) + sh_ref[0]
    r = jnp.dot(h.astype(BF16), w_ref[...], preferred_element_type=F32)
    xm_ref[...] = r[:, :di].astype(BF16)
    z_ref[...] = r[:, di:].astype(BF16)


def _inproj(x2, shift, scale, w_in, S):
    T, D = x2.shape
    di = w_in.shape[1] // 2
    tm = min(512, S)
    tpb = S // tm
    return pl.pallas_call(
        _inproj_kernel,
        grid=(T // tm,),
        in_specs=[pl.BlockSpec((tm, D), lambda i: (i, 0)),
                  pl.BlockSpec((1, 1, D), lambda i: (i // tpb, 0, 0)),
                  pl.BlockSpec((1, 1, D), lambda i: (i // tpb, 0, 0)),
                  pl.BlockSpec((D, 2 * di), lambda i: (0, 0))],
        out_specs=[pl.BlockSpec((tm, di), lambda i: (i, 0)),
                   pl.BlockSpec((tm, di), lambda i: (i, 0))],
        out_shape=[jax.ShapeDtypeStruct((T, di), BF16)] * 2,
        compiler_params=_params(("parallel",)),
        name="mlstm_inproj",
    )(x2, shift, scale, w_in)


def _qkv_kernel(xm_ref, halo_ref, cw_ref, cb_ref, wq_ref, wk_ref, wv_ref, wif_ref, bif_ref,
                q_ref, k_ref, v_ref, xc_ref, g_ref, *, tpb, nh, kscale):
    tm, di = xm_ref.shape
    dh = di // nh
    i = pl.program_id(0)
    xm = xm_ref[...]
    xf = xm.astype(F32)
    halo = halo_ref[...].astype(F32)
    halo = jnp.where(i % tpb == 0, 0.0, halo)
    ext = jnp.concatenate([halo, xf], axis=0)
    acc = cb_ref[...] + cw_ref[CONV_WIDTH - 1:CONV_WIDTH, :] * xf
    for s in range(1, CONV_WIDTH):
        acc = acc + cw_ref[CONV_WIDTH - 1 - s:CONV_WIDTH - s, :] * ext[CONV_HALO - s:CONV_HALO - s + tm, :]
    xc = acc * _sigmoid(acc)
    xcb = xc.astype(BF16)
    xc_ref[...] = xcb
    for h in range(nh):
        sl = slice(h * dh, (h + 1) * dh)
        q_ref[:, sl] = jnp.dot(xcb[:, sl], wq_ref[h], preferred_element_type=F32).astype(BF16)
        k_ref[:, sl] = (jnp.dot(xcb[:, sl], wk_ref[h], preferred_element_type=F32) * kscale).astype(BF16)
        v_ref[:, sl] = jnp.dot(xm[:, sl], wv_ref[h], preferred_element_type=F32).astype(BF16)
    g = jnp.dot(xcb, wif_ref[...], preferred_element_type=F32) + bif_ref[...]
    col = lax.broadcasted_iota(jnp.int32, g.shape, 1)
    logsig = jnp.minimum(g, 0.0) - jnp.log(1.0 + jnp.exp(-jnp.abs(g)))
    g = jnp.where(col >= nh, logsig, g)
    g_ref[...] = g[:, :2 * nh]


def _qkv(xm, conv_w, conv_b, wq, wk, wv, w_if, b_if, S):
    T, di = xm.shape
    nh = wq.shape[0]
    dh = di // nh
    tm = min(512, S)
    tpb = S // tm
    hb = tm // CONV_HALO
    wif = jnp.zeros((di, LANES), BF16).at[:, :2 * nh].set(w_if.astype(BF16))
    bif = jnp.zeros((1, LANES), F32).at[0, :2 * nh].set(b_if)
    kern = functools.partial(_qkv_kernel, tpb=tpb, nh=nh, kscale=float(dh) ** -0.5)
    full2 = lambda i: (0, 0)
    full3 = lambda i: (0, 0, 0)
    row = lambda i: (i, 0)
    return pl.pallas_call(
        kern,
        grid=(T // tm,),
        in_specs=[pl.BlockSpec((tm, di), row),
                  pl.BlockSpec((CONV_HALO, di), lambda i: (jnp.maximum(i * hb - 1, 0), 0)),
                  pl.BlockSpec((CONV_WIDTH, di), full2),
                  pl.BlockSpec((1, di), full2),
                  pl.BlockSpec((nh, dh, dh), full3),
                  pl.BlockSpec((nh, dh, dh), full3),
                  pl.BlockSpec((nh, dh, dh), full3),
                  pl.BlockSpec((di, LANES), full2),
                  pl.BlockSpec((1, LANES), full2)],
        out_specs=[pl.BlockSpec((tm, di), row)] * 4 + [pl.BlockSpec((tm, 2 * nh), row)],
        out_shape=[jax.ShapeDtypeStruct((T, di), BF16)] * 4 + [jax.ShapeDtypeStruct((T, 2 * nh), F32)],
        compiler_params=_params(("parallel",)),
        name="mlstm_qkv",
    )(xm, xm, conv_w, conv_b.reshape(1, di), wq, wk, wv, wif, bif)


def _mlstm_kernel(q_ref, k_ref, v_ref, gc_ref, gr_ref, o_ref, c_scr, n_scr, m_scr):
    L = q_ref.shape[0]
    nh = c_scr.shape[0]
    dh = c_scr.shape[1]

    @pl.when(pl.program_id(1) == 0)
    def _():
        c_scr[...] = jnp.zeros_like(c_scr)
        n_scr[...] = jnp.zeros_like(n_scr)
        m_scr[...] = jnp.zeros_like(m_scr)

    t_idx = lax.broadcasted_iota(jnp.int32, (L, L), 0)
    s_idx = lax.broadcasted_iota(jnp.int32, (L, L), 1)
    causal = s_idx <= t_idx
    for h in range(nh):
        hs = slice(h * dh, (h + 1) * dh)
        _mlstm_head(q_ref[:, hs], k_ref[:, hs], v_ref[:, hs], gc_ref[0, h], gr_ref[0, h],
                    o_ref.at[:, hs], c_scr.at[h], n_scr.at[h], m_scr.at[h], causal, t_idx, s_idx)


def _mlstm_head(q, k, v, gc, gr, o_ref, c_scr, n_scr, m_scr, causal, t_idx, s_idx):
    i_col, f_col = gc[:, 0:1], gc[:, 1:2]
    i_row, f_row = gr[0:1, :], gr[1:2, :]
    b_col = jnp.sum(jnp.where(causal, f_row, 0.0), axis=1, keepdims=True)
    b_row = jnp.sum(jnp.where(t_idx <= s_idx, f_col, 0.0), axis=0, keepdims=True)
    m_prev = m_scr[...]
    d = jnp.where(causal, b_col - b_row + i_row, -jnp.inf)
    a_col = b_col + m_prev
    m_t = jnp.maximum(a_col, jnp.max(d, axis=1, keepdims=True))
    w_intra = jnp.exp(d - m_t)
    w_inter = jnp.exp(a_col - m_t)
    qk = lax.dot_general(q, k, (((1,), (1,)), ((), ())), preferred_element_type=F32)
    s_mat = qk * w_intra
    c_b = c_scr[...].astype(BF16)
    inter = lax.dot_general(q, c_b, (((1,), (1,)), ((), ())), preferred_element_type=F32)
    num = jnp.dot(s_mat.astype(BF16), v, preferred_element_type=F32) + w_inter * inter
    qn = jnp.sum(q.astype(F32) * n_scr[...], axis=1, keepdims=True)
    den = jnp.sum(s_mat, axis=1, keepdims=True) + w_inter * qn
    hcap = num / jnp.maximum(jnp.abs(den), jnp.exp(-m_t))
    mu = jnp.mean(hcap, axis=1, keepdims=True)
    dv = hcap - mu
    var = jnp.mean(dv * dv, axis=1, keepdims=True)
    o_ref[...] = (dv * lax.rsqrt(var + LN_EPS)).astype(BF16)

    b_last = jnp.sum(f_row, axis=1, keepdims=True)
    ws_col = b_last - b_col + i_col
    m_new = jnp.maximum(b_last + m_prev, jnp.max(ws_col, axis=0, keepdims=True))
    decay = jnp.exp(b_last + m_prev - m_new)
    ws = jnp.exp(ws_col - m_new)
    vw = (v.astype(F32) * ws).astype(BF16)
    upd = lax.dot_general(vw, k, (((0,), (0,)), ((), ())), preferred_element_type=F32)
    c_scr[...] = decay * c_scr[...] + upd
    n_scr[...] = decay * n_scr[...] + jnp.sum(k.astype(F32) * ws, axis=0, keepdims=True)
    m_scr[...] = m_new


def _mlstm(q, k, v, gcol, grow, B, S):
    T, di = q.shape
    nh = gcol.shape[1]
    dh = di // nh
    L = min(MLSTM_CHUNK, S)
    nc = S // L
    blk = pl.BlockSpec((L, di), lambda b, c: (b * nc + c, 0))
    return pl.pallas_call(
        _mlstm_kernel,
        grid=(B, nc),
        in_specs=[blk, blk, blk,
                  pl.BlockSpec((1, nh, L, 2), lambda b, c: (b, 0, c, 0)),
                  pl.BlockSpec((1, nh, 2, L), lambda b, c: (b, 0, 0, c))],
        out_specs=blk,
        out_shape=jax.ShapeDtypeStruct((T, di), BF16),
        scratch_shapes=[pltpu.VMEM((nh, dh, dh), F32), pltpu.VMEM((nh, 1, dh), F32), pltpu.VMEM((nh, 1, 1), F32)],
        compiler_params=_params(("parallel", "arbitrary")),
        name="mlstm_scan",
    )(q, k, v, gcol, grow)


def _mlstm_out_kernel(hn_ref, xc_ref, z_ref, x_ref, gn_ref, skip_ref, w_ref, gate_ref, lg_ref, lb_ref, o_ref):
    z = z_ref[...].astype(F32)
    u = (hn_ref[...].astype(F32) * gn_ref[...] + skip_ref[...] * xc_ref[...].astype(F32)) * (z * _sigmoid(z))
    y = jnp.dot(u.astype(BF16), w_ref[...], preferred_element_type=F32)
    r = DEEPNORM_ALPHA * x_ref[...] + gate_ref[0] * y
    o_ref[...] = _layer_norm(r, lg_ref[...], lb_ref[...])


def _mlstm_out(hn, xc, z, x2, gn_g, skip, w_out, gate, ln_g, ln_b, S):
    T, D = x2.shape
    di = hn.shape[1]
    tm = min(512, S)
    tpb = S // tm
    row = lambda i: (i, 0)
    full2 = lambda i: (0, 0)
    return pl.pallas_call(
        _mlstm_out_kernel,
        grid=(T // tm,),
        in_specs=[pl.BlockSpec((tm, di), row), pl.BlockSpec((tm, di), row), pl.BlockSpec((tm, di), row),
                  pl.BlockSpec((tm, D), row),
                  pl.BlockSpec((1, di), full2), pl.BlockSpec((1, di), full2),
                  pl.BlockSpec((di, D), full2),
                  pl.BlockSpec((1, 1, D), lambda i: (i // tpb, 0, 0)),
                  pl.BlockSpec((1, D), full2), pl.BlockSpec((1, D), full2)],
        out_specs=pl.BlockSpec((tm, D), row),
        out_shape=jax.ShapeDtypeStruct((T, D), F32),
        compiler_params=_params(("parallel",)),
        name="mlstm_out_ln",
    )(hn, xc, z, x2, gn_g.reshape(1, di), skip.reshape(1, di), w_out, gate, ln_g.reshape(1, D), ln_b.reshape(1, D))


def _route_kernel(x_ref, sh_ref, sc_ref, wrt_ref, br_ref, tri_ref, out_ref, cnt_ref):
    E = wrt_ref.shape[0]
    ng = N_GROUPS
    epg = E // ng
    tm = x_ref.shape[0]
    h = x_ref[...] * (1.0 + sc_ref[0]) + sh_ref[0]
    nt = (((1,), (1,)), ((), ()))
    w = wrt_ref[...]
    w_hi = w.astype(BF16)
    w_lo = (w - w_hi.astype(F32)).astype(BF16)
    h_hi = h.astype(BF16)
    h_lo = (h - h_hi.astype(F32)).astype(BF16)
    lt = (lax.dot_general(w_hi, h_hi, nt, preferred_element_type=F32)
          + lax.dot_general(w_hi, h_lo, nt, preferred_element_type=F32)
          + lax.dot_general(w_lo, h_hi, nt, preferred_element_type=F32))
    ex = jnp.exp(lt - jnp.max(lt, axis=0, keepdims=True))
    probs = ex / jnp.sum(ex, axis=0, keepdims=True)
    sel = probs + br_ref[...]
    member = [sel[k * ng:(k + 1) * ng, :] for k in range(epg)]
    pmember = [probs[k * ng:(k + 1) * ng, :] for k in range(epg)]
    gscore = None
    for a in range(epg):
        for b in range(a + 1, epg):
            pair = member[a] + member[b]
            gscore = pair if gscore is None else jnp.maximum(gscore, pair)
    gidx = lax.broadcasted_iota(jnp.int32, (ng, tm), 0)
    gmax = jnp.max(gscore, axis=0, keepdims=True)
    chosen = gidx == jnp.min(jnp.where(gscore == gmax, gidx, ng), axis=0, keepdims=True)
    zero_g = jnp.zeros((ng, tm), F32)
    e1, e2, p1, p2 = zero_g, zero_g, zero_g, zero_g
    firsts, seconds = [], []
    for k in range(epg):
        rank = zero_g
        for o in range(epg):
            if o != k:
                beats = (member[o] >= member[k]) if o < k else (member[o] > member[k])
                rank = rank + jnp.where(beats, 1.0, 0.0)
        is1 = chosen & (rank == 0.0)
        is2 = chosen & (rank == 1.0)
        firsts.append(jnp.where(is1, 1.0, 0.0))
        seconds.append(jnp.where(is2, 1.0, 0.0))
        eid = (gidx * epg + k).astype(F32)
        e1 = e1 + jnp.where(is1, eid, 0.0)
        e2 = e2 + jnp.where(is2, eid, 0.0)
        p1 = p1 + jnp.where(is1, pmember[k], 0.0)
        p2 = p2 + jnp.where(is2, pmember[k], 0.0)
    e1, e2, p1, p2 = [jnp.sum(v, axis=0, keepdims=True) for v in (e1, e2, p1, p2)]
    first = jnp.concatenate(firsts, axis=0)
    second = jnp.concatenate(seconds, axis=0)
    mask = first + second
    prefix = jnp.dot(mask.astype(BF16), tri_ref[...], preferred_element_type=F32)
    counts = jnp.sum(mask, axis=1, keepdims=True)
    padded = jnp.floor((counts + (RUN_ALIGN - 1)) * (1.0 / RUN_ALIGN)) * RUN_ALIGN
    row_of = [(e % epg) * ng + e // epg for e in range(E)]
    starts = [None] * E
    acc = jnp.zeros((1, 1), F32)
    for e in range(E):
        starts[row_of[e]] = acc
        acc = acc + padded[row_of[e]:row_of[e] + 1, :]
    slot = prefix + jnp.concatenate(starts, axis=0)
    s1 = jnp.sum(first * slot, axis=0, keepdims=True)
    s2 = jnp.sum(second * slot, axis=0, keepdims=True)
    psum = p1 + p2
    zero = jnp.zeros((1, tm), F32)
    out_ref[...] = jnp.concatenate([e1, e2, p1 / psum, p2 / psum, s1, s2, zero, zero], axis=0)
    counts_by_expert = jnp.concatenate([counts[row_of[e]:row_of[e] + 1, :] for e in range(E)], axis=0)
    cnt_ref[...] = jnp.broadcast_to(counts_by_expert, cnt_ref.shape)


def _route(x2, shift, scale, w_router, b_router, S):
    T, D = x2.shape
    E = w_router.shape[1]
    tm = min(MOE_TILE, S)
    tpb = S // tm
    tri = (jnp.arange(tm)[:, None] < jnp.arange(tm)[None, :]).astype(BF16)
    epg = E // N_GROUPS
    rows = jnp.array([g * epg + k for k in range(epg) for g in range(N_GROUPS)], jnp.int32)
    return pl.pallas_call(
        _route_kernel,
        grid=(T // tm,),
        in_specs=[pl.BlockSpec((tm, D), lambda i: (i, 0)),
                  pl.BlockSpec((1, 1, D), lambda i: (i // tpb, 0, 0)),
                  pl.BlockSpec((1, 1, D), lambda i: (i // tpb, 0, 0)),
                  pl.BlockSpec((E, D), lambda i: (0, 0)),
                  pl.BlockSpec((E, 1), lambda i: (0, 0)),
                  pl.BlockSpec((tm, tm), lambda i: (0, 0))],
        out_specs=[pl.BlockSpec((8, tm), lambda i: (0, i)),
                   pl.BlockSpec((E, LANES), lambda i: (0, i))],
        out_shape=[jax.ShapeDtypeStruct((8, T), F32), jax.ShapeDtypeStruct((E, (T // tm) * LANES), F32)],
        compiler_params=_params(("parallel",)),
        name="moe_route",
    )(x2, shift, scale, w_router.T[rows], b_router[rows].reshape(E, 1), tri)


def _stage_rows(tm, n_exp):
    return MOE_TOPK * tm + n_exp * RUN_ALIGN


def _chunk_copies(tc_ref, ts_ref, ds_ref, step, n_exp, make):
    out = []
    for e in range(n_exp):
        cnt = tc_ref[step * n_exp + e]
        off = ts_ref[step * n_exp + e]
        dst = ds_ref[step * n_exp + e]
        for k in reversed(range(RUN_ALIGN.bit_length() - 1, CHUNK_BITS)):
            done = (cnt >> (k + 1)) << (k + 1)
            out.append(((cnt & (1 << k)) != 0,
                        make(pl.multiple_of(off + done, RUN_ALIGN), pl.multiple_of(dst + done, RUN_ALIGN), 1 << k)))
    return out


def _dispatch_kernel(tc_ref, ts_ref, ds_ref, pe_ref, x_ref, sh_ref, sc_ref, pos_ref, xs_ref,
                     stage, zbuf, sem, zsem):
    tm = x_ref.shape[0]
    rb = zbuf.shape[0]
    i = pl.program_id(0)
    slot = i % 2

    n_exp = pe_ref.shape[0] - 1
    n_blocks = xs_ref.shape[0] // rb

    def pad_copy(e):
        return pltpu.make_async_copy(zbuf, xs_ref.at[pl.ds(pl.multiple_of(pe_ref[e + 1] - rb, rb), rb)], zsem)

    def tail_copy(j):
        return pltpu.make_async_copy(zbuf, xs_ref.at[pl.ds(pl.multiple_of(pe_ref[n_exp] + j * rb, rb), rb)], zsem)

    def tail_live(j):
        return pe_ref[n_exp] // rb + j < n_blocks

    @pl.when(i == 0)
    def _():
        zbuf[...] = jnp.zeros_like(zbuf)
        for e in range(n_exp):
            @pl.when(pe_ref[e + 1] > pe_ref[e])
            def _():
                pad_copy(e).start()
        for j in range(n_exp):
            @pl.when(tail_live(j))
            def _():
                tail_copy(j).start()
        for e in range(n_exp):
            @pl.when(pe_ref[e + 1] > pe_ref[e])
            def _():
                pad_copy(e).wait()
        for j in range(n_exp):
            @pl.when(tail_live(j))
            def _():
                tail_copy(j).wait()

    n_tiles = pl.num_programs(0) - 1

    @pl.when(i < n_tiles)
    def _():
        h = (x_ref[...] * (1.0 + sc_ref[0]) + sh_ref[0]).astype(BF16)
        pos = pos_ref[0]
        slot_id = lax.broadcasted_iota(jnp.int32, (stage.shape[1], tm), 0)
        perm = jnp.where((slot_id == pos[0:1, :]) | (slot_id == pos[1:2, :]), 1.0, 0.0).astype(BF16)
        stage[slot] = jnp.dot(perm, h, preferred_element_type=F32)

    def copies(step, s):
        def make(src_row, dst_row, size):
            return pltpu.make_async_copy(stage.at[s, pl.ds(src_row, size)], xs_ref.at[pl.ds(dst_row, size)], sem.at[s])
        return _chunk_copies(tc_ref, ts_ref, ds_ref, step, n_exp, make)

    for live, cp in copies(jnp.minimum(i, n_tiles - 1), slot):
        pl.when(live & (i < n_tiles))(cp.start)
    for live, cp in copies(jnp.maximum(i - 1, 0), 1 - slot):
        pl.when(live & (i > 0))(cp.wait)


def _dispatch(x2, shift, scale, posrow, tcnt, tstart, dstart, pends0, cap, S):
    T, D = x2.shape
    tm = min(MOE_TILE, S)
    tpb = S // tm
    nt = T // tm
    tile = lambda i: jnp.minimum(i, nt - 1)
    mod = pl.BlockSpec((1, 1, D), lambda i, *_: (tile(i) // tpb, 0, 0))
    grid_spec = pltpu.PrefetchScalarGridSpec(
        num_scalar_prefetch=4,
        grid=(nt + 1,),
        in_specs=[pl.BlockSpec((tm, D), lambda i, *_: (tile(i), 0)), mod, mod,
                  pl.BlockSpec((1, 2, tm), lambda i, *_: (tile(i), 0, 0))],
        out_specs=pl.BlockSpec(memory_space=pl.ANY),
        scratch_shapes=[pltpu.VMEM((2, _stage_rows(tm, pends0.shape[0] - 1), D), F32),
                        pltpu.VMEM((MOE_ROW_BLOCK, D), F32),
                        pltpu.SemaphoreType.DMA((2,)), pltpu.SemaphoreType.DMA(())],
    )
    return pl.pallas_call(
        _dispatch_kernel,
        grid_spec=grid_spec,
        out_shape=jax.ShapeDtypeStruct((cap, D), F32),
        compiler_params=_params(("arbitrary",)),
        name="moe_dispatch",
    )(tcnt, tstart, dstart, pends0, x2, shift, scale, posrow)


def _expert_kernel(be_ref, na_ref, xs_ref, wg_ref, wu_ref, wd_ref, ys_ref, wg_b, wu_b, wd_b):
    i = pl.program_id(0)
    active = i < na_ref[0]

    @pl.when(active & ((i == 0) | (be_ref[i] != be_ref[jnp.maximum(i - 1, 0)])))
    def _():
        wg_b[...] = wg_ref[0, 0].astype(BF16)
        wu_b[...] = wu_ref[0, 0].astype(BF16)
        wd_b[...] = wd_ref[0, 0].astype(BF16)

    @pl.when(active)
    def _():
        xb = xs_ref[...].astype(BF16)
        g = jnp.dot(xb, wg_b[...], preferred_element_type=F32)
        u = jnp.dot(xb, wu_b[...], preferred_element_type=F32)
        a = (g * _sigmoid(g)) * u
        ys_ref[...] = jnp.dot(a.astype(BF16), wd_b[...], preferred_element_type=F32)

    @pl.when(jnp.logical_not(active))
    def _():
        ys_ref[...] = jnp.zeros_like(ys_ref)


def _experts(xs, block_expert, n_active, w_gate, w_up, w_down, layer):
    cap, D = xs.shape
    _, E, _, F = w_gate.shape
    rb = MOE_ROW_BLOCK
    nb = cap // rb

    def blk(i, be, na):
        return jnp.maximum(jnp.minimum(i, na[0] - 1), 0)

    grid_spec = pltpu.PrefetchScalarGridSpec(
        num_scalar_prefetch=2,
        grid=(nb,),
        in_specs=[pl.BlockSpec((rb, D), lambda i, be, na: (blk(i, be, na), 0)),
                  pl.BlockSpec((1, 1, D, F), lambda i, be, na: (layer, be[blk(i, be, na)], 0, 0)),
                  pl.BlockSpec((1, 1, D, F), lambda i, be, na: (layer, be[blk(i, be, na)], 0, 0)),
                  pl.BlockSpec((1, 1, F, D), lambda i, be, na: (layer, be[blk(i, be, na)], 0, 0))],
        out_specs=pl.BlockSpec((rb, D), lambda i, be, na: (i, 0)),
        scratch_shapes=[pltpu.VMEM((D, F), BF16), pltpu.VMEM((D, F), BF16), pltpu.VMEM((F, D), BF16)],
    )
    return pl.pallas_call(
        _expert_kernel,
        grid_spec=grid_spec,
        out_shape=jax.ShapeDtypeStruct((cap, D), F32),
        compiler_params=_params(("arbitrary",)),
        name="moe_experts",
    )(block_expert, n_active, xs, w_gate, w_up, w_down)


def _combine_kernel(tc_ref, ts_ref, ds_ref, ys_ref, rt_ref, x_ref, gate_ref, lg_ref, lb_ref, o_ref, stage, sem,
                    *, n_exp):
    tm = x_ref.shape[0]
    i = pl.program_id(0)
    n_tiles = pl.num_programs(0) - 1
    slot = i % 2

    def copies(step, s):
        def make(stage_row, ys_row, size):
            return pltpu.make_async_copy(ys_ref.at[pl.ds(ys_row, size)], stage.at[s, pl.ds(stage_row, size)], sem.at[s])
        return _chunk_copies(tc_ref, ts_ref, ds_ref, step, n_exp, make)

    @pl.when(i == 0)
    def _():
        stage[...] = jnp.zeros_like(stage)

    for live, cp in copies(jnp.minimum(i, n_tiles - 1), slot):
        pl.when(live & (i < n_tiles))(cp.start)
    for live, cp in copies(jnp.maximum(i - 1, 0), 1 - slot):
        pl.when(live & (i > 0))(cp.wait)

    @pl.when(i > 0)
    def _():
        rt = rt_ref[...]
        ysb = stage[1 - slot].astype(BF16)
        slot_id = lax.broadcasted_iota(jnp.int32, (tm, stage.shape[1]), 1).astype(F32)
        y = None
        for j in range(MOE_TOPK):
            perm = jnp.where(slot_id == rt[:, 4 + j:5 + j], 1.0, 0.0).astype(BF16)
            part = rt[:, 2 + j:3 + j] * jnp.dot(perm, ysb, preferred_element_type=F32)
            y = part if y is None else y + part
        r = DEEPNORM_ALPHA * x_ref[...] + gate_ref[0] * y
        o_ref[...] = _layer_norm(r, lg_ref[...], lb_ref[...])


def _combine(ys, route_t, x2, gate, ln_g, ln_b, tcnt, tstart, dstart, S):
    T, D = x2.shape
    tm = min(MOE_TILE, S)
    tpb = S // tm
    tile = lambda i: jnp.maximum(i - 1, 0)
    grid_spec = pltpu.PrefetchScalarGridSpec(
        num_scalar_prefetch=3,
        grid=(T // tm + 1,),
        in_specs=[pl.BlockSpec(memory_space=pl.ANY),
                  pl.BlockSpec((tm, 8), lambda i, *_: (tile(i), 0)),
                  pl.BlockSpec((tm, D), lambda i, *_: (tile(i), 0)),
                  pl.BlockSpec((1, 1, D), lambda i, *_: (tile(i) // tpb, 0, 0)),
                  pl.BlockSpec((1, D), lambda i, *_: (0, 0)),
                  pl.BlockSpec((1, D), lambda i, *_: (0, 0))],
        out_specs=pl.BlockSpec((tm, D), lambda i, *_: (tile(i), 0)),
        scratch_shapes=[pltpu.VMEM((2, _stage_rows(tm, tcnt.shape[0] // (T // tm)), D), F32),
                        pltpu.SemaphoreType.DMA((2,))],
    )
    return pl.pallas_call(
        functools.partial(_combine_kernel, n_exp=tcnt.shape[0] // (T // tm)),
        grid_spec=grid_spec,
        out_shape=jax.ShapeDtypeStruct((T, D), F32),
        compiler_params=_params(("arbitrary",)),
        name="moe_combine_ln",
    )(tcnt, tstart, dstart, ys, route_t, x2, gate, ln_g.reshape(1, D), ln_b.reshape(1, D))


def _moe_layer(x2, shift, scale, gate, ln_g, ln_b, w_router, b_router, w_gate, w_up, w_down, layer, S):
    T, D = x2.shape
    E = w_router.shape[1]
    rb = MOE_ROW_BLOCK
    tm = min(MOE_TILE, S)
    nt = T // tm
    assert tm < (1 << CHUNK_BITS)
    route, cnt = _route(x2, shift, scale, w_router, b_router, S)
    tcnt = cnt.reshape(E, nt, LANES)[:, :, 0].T.astype(jnp.int32)
    tcnt = (tcnt + RUN_ALIGN - 1) // RUN_ALIGN * RUN_ALIGN
    base = jnp.cumsum(tcnt, axis=0) - tcnt
    counts = jnp.sum(tcnt, axis=0)
    padded = (counts + rb - 1) // rb * rb
    pends = jnp.cumsum(padded)
    pstarts = pends - padded
    tstart = jnp.cumsum(tcnt, axis=1) - tcnt
    dstart = pstarts[None, :] + base
    posrow = jnp.transpose(route[4:6].astype(jnp.int32).reshape(2, nt, tm), (1, 0, 2))
    nb = -(-(T * MOE_TOPK + nt * E * (RUN_ALIGN - 1)) // rb) + E
    cap = nb * rb
    block_start = jnp.arange(nb, dtype=jnp.int32) * rb
    block_expert = jnp.minimum(jnp.sum(block_start[:, None] >= pends[None, :], axis=1), E - 1).astype(jnp.int32)
    n_active = (pends[-1:] // rb).astype(jnp.int32)
    pends0 = jnp.concatenate([jnp.zeros((1,), jnp.int32), pends.astype(jnp.int32)])
    tables = [t.reshape(-1).astype(jnp.int32) for t in (tcnt, tstart, dstart)]
    xs = _dispatch(x2, shift, scale, posrow, *tables, pends0, cap, S)
    ys = _experts(xs, block_expert, n_active, w_gate, w_up, w_down, layer)
    return _combine(ys, route.T, x2, gate, ln_g, ln_b, *tables, S)


def _moba_proj_kernel(x_ref, shq_ref, scq_ref, shkv_ref, sckv_ref, wq_ref, wkv_ref,
                      q_ref, k_ref, vt_ref, km_ref, *, qscale, tpb, slopes):
    tm, D = x_ref.shape
    hd = HEAD_DIM
    nh = D // hd
    x = x_ref[...]
    hq = x * (1.0 + scq_ref[0]) + shq_ref[0]
    hkv = x * (1.0 + sckv_ref[0]) + shkv_ref[0]
    qf = jnp.dot(hq.astype(BF16), wq_ref[...], preferred_element_type=F32) * qscale
    kv = jnp.dot(hkv.astype(BF16), wkv_ref[...], preferred_element_type=F32)
    kf = kv[:, :D]
    vt = kv[:, D:].T
    ones_pad = jnp.where(lax.broadcasted_iota(jnp.int32, (V_ROWS - hd, tm), 0) == 0, 1.0, 0.0)
    for g in range(nh):
        vt_ref[0, g * V_ROWS:(g + 1) * V_ROWS, :] = jnp.concatenate(
            [vt[g * hd:(g + 1) * hd, :], ones_pad], axis=0).astype(BF16)
    lane = lax.broadcasted_iota(jnp.int32, (1, LANES), 1)
    head_lanes = lane >= hd
    pos = (pl.program_id(0) % tpb) * tm + lax.broadcasted_iota(jnp.int32, (tm, 1), 0)
    pos_f = pos.astype(F32)
    onehot = jnp.where(lane == pos // MOBA_BLOCK, 1.0, 0.0)
    q_fill = jnp.where((lane >= BIAS_LANE) & (lane < BIAS_LANE + 3), 1.0, 0.0)
    nbt = tm // MOBA_BLOCK
    means = [jnp.mean(kf[j * MOBA_BLOCK:(j + 1) * MOBA_BLOCK, :], axis=0, keepdims=True) for j in range(nbt)]
    km = jnp.concatenate(means + [jnp.zeros((8 - nbt, D), F32)], axis=0)
    for j in range(nh // 2):
        cs = slice(j * LANES, (j + 1) * LANES)
        qb, kb, mb = qf[:, cs], kf[:, cs], km[:, cs]
        q_pair = (pltpu.roll(qb, hd, axis=1), qb)
        k_pair = (pltpu.roll(kb, hd, axis=1), kb)
        m_pair = (pltpu.roll(mb, hd, axis=1), mb)
        for e in range(2):
            g = 2 * j + e
            gs = slice(g * LANES, (g + 1) * LANES)
            q_ref[:, gs] = jnp.where(head_lanes, q_pair[e], q_fill).astype(BF16)
            c = (slopes[g] * LOG2E) * pos_f
            hi = c.astype(BF16).astype(F32)
            mid = (c - hi).astype(BF16).astype(F32)
            lo = c - hi - mid
            aug = jnp.where(lane == BIAS_LANE, hi,
                            jnp.where(lane == BIAS_LANE + 1, mid,
                                      jnp.where(lane == BIAS_LANE + 2, lo, onehot)))
            k_ref[:, gs] = jnp.where(head_lanes, k_pair[e], aug).astype(BF16)
            mg = jnp.where(head_lanes, m_pair[e], 0.0)
            for jb in range(nbt):
                km_ref[jb, :, gs] = mg[jb:jb + 1, :]


def _moba_proj(x2, shq, scq, shkv, sckv, wq, wkv, B, S):
    T, D = x2.shape
    nh = D // HEAD_DIM
    tm = min(512, S)
    tpb = S // tm
    nbt = tm // MOBA_BLOCK
    assert S // MOBA_BLOCK <= BIAS_LANE and nbt <= 8
    slopes = tuple(2.0 ** (-8.0 * (h + 1.0) / nh) for h in range(nh))
    mod = pl.BlockSpec((1, 1, D), lambda i: (i // tpb, 0, 0))
    kern = functools.partial(_moba_proj_kernel, qscale=float(HEAD_DIM) ** -0.5 * LOG2E, tpb=tpb, slopes=slopes)
    return pl.pallas_call(
        kern,
        grid=(T // tm,),
        in_specs=[pl.BlockSpec((tm, D), lambda i: (i, 0)), mod, mod, mod, mod,
                  pl.BlockSpec((D, D), lambda i: (0, 0)),
                  pl.BlockSpec((D, 2 * D), lambda i: (0, 0))],
        out_specs=[pl.BlockSpec((tm, nh * LANES), lambda i: (i, 0)),
                   pl.BlockSpec((tm, nh * LANES), lambda i: (i, 0)),
                   pl.BlockSpec((1, nh * V_ROWS, tm), lambda i: (i // tpb, 0, i % tpb)),
                   pl.BlockSpec((nbt, 1, nh * LANES), lambda i: (i, 0, 0))],
        out_shape=[jax.ShapeDtypeStruct((T, nh * LANES), BF16), jax.ShapeDtypeStruct((T, nh * LANES), BF16),
                   jax.ShapeDtypeStruct((B, nh * V_ROWS, S), BF16),
                   jax.ShapeDtypeStruct((T // MOBA_BLOCK, 1, nh * LANES), F32)],
        compiler_params=_params(("parallel",)),
        name="moba_proj",
    )(x2, shq, scq, shkv, sckv, wq, wkv)


def _moba_select_kernel(q_ref, km_ref, qat_ref):
    tq = q_ref.shape[0]
    G = q_ref.shape[1] // LANES
    nblk = km_ref.shape[1]
    blk_id = lax.broadcasted_iota(jnp.int32, (nblk, tq), 0)
    own = (pl.program_id(2) * tq + lax.broadcasted_iota(jnp.int32, (1, tq), 1)) // MOBA_BLOCK
    for g in range(G):
        gs = slice(g * LANES, (g + 1) * LANES)
        qgt = q_ref[:, gs].astype(F32).T
        qb = qgt.astype(BF16)
        km = km_ref[0, :, gs]
        km_hi = km.astype(BF16)
        km_mid = (km - km_hi.astype(F32)).astype(BF16)
        km_lo = (km - km_hi.astype(F32) - km_mid.astype(F32)).astype(BF16)
        gate = (jnp.dot(km_hi, qb, preferred_element_type=F32) + jnp.dot(km_mid, qb, preferred_element_type=F32)
                + jnp.dot(km_lo, qb, preferred_element_type=F32))
        gsc = jnp.where(blk_id < own, gate, -jnp.inf)
        picked = blk_id < 0
        for _ in range(MOBA_TOPK):
            mx = jnp.max(gsc, axis=0, keepdims=True)
            first = jnp.min(jnp.where(gsc == mx, blk_id, nblk), axis=0, keepdims=True)
            pick = blk_id == first
            picked = picked | pick
            gsc = jnp.where(pick, -jnp.inf, gsc)
        visible = (picked & (blk_id < own)) | (blk_id == own)
        pen = jnp.where(visible, 0.0, NEG)
        qat_ref[0, g] = (qgt + jnp.concatenate([pen, jnp.zeros((LANES - nblk, tq), F32)], axis=0)).astype(BF16)


def _moba_select(q, km, B, S):
    nh = q.shape[1] // LANES
    G = MOBA_HEADS_PER_STEP
    tq = min(1024, S)
    nblk = S // MOBA_BLOCK
    return pl.pallas_call(
        _moba_select_kernel,
        grid=(B, nh // G, S // tq),
        in_specs=[pl.BlockSpec((tq, G * LANES), lambda b, hp, i: (b * (S // tq) + i, hp)),
                  pl.BlockSpec((1, nblk, G * LANES), lambda b, hp, i: (b, 0, hp))],
        out_specs=pl.BlockSpec((1, G, LANES, tq), lambda b, hp, i: (b, hp, 0, i)),
        out_shape=jax.ShapeDtypeStruct((B, nh, LANES, S), BF16),
        compiler_params=_params(("parallel", "parallel", "parallel")),
        name="moba_select",
    )(q, km)


def _moba_kernel(qat_ref, qat_next_ref, k_ref, vt_ref, o_ref, st_scr, m_scr, acc_scr):
    G = qat_ref.shape[1]
    BLK = qat_ref.shape[3]
    hd = HEAD_DIM
    nhalf = BLK // LANES
    own = pl.program_id(2)
    k_off = lax.broadcasted_iota(jnp.int32, (BLK, 1), 0)
    q_off = lax.broadcasted_iota(jnp.int32, (1, LANES), 1)

    m_scr[...] = jnp.full(m_scr.shape, NEG, F32)
    acc_scr[...] = jnp.zeros_like(acc_scr)

    def scores(n, slot, q_ref=qat_ref):
        start = pl.multiple_of(n * BLK, BLK)
        for g in range(G):
            kn = k_ref[pl.ds(start, BLK), g * LANES:(g + 1) * LANES]
            st_scr[slot, g] = jnp.dot(kn, q_ref[0, g], preferred_element_type=F32)

    def update(n, slot, causal):
        start = pl.multiple_of(n * BLK, BLK)
        for g in range(G):
            alphas, ps = [], []
            for hf in range(nhalf):
                c = g * nhalf + hf
                ls = slice(hf * LANES, (hf + 1) * LANES)
                st = st_scr[slot, g, :, ls]
                if causal:
                    st = jnp.where(k_off <= q_off + hf * LANES, st, NEG)
                m = m_scr[c]
                m_new = jnp.maximum(m, jnp.max(st, axis=0, keepdims=True))
                alphas.append(jnp.exp2(m - m_new))
                ps.append(jnp.exp2(st - m_new).astype(BF16))
                m_scr[c] = m_new
            vtn = vt_ref[0, g * V_ROWS:(g + 1) * V_ROWS, pl.ds(start, BLK)]
            pv = jnp.dot(vtn, jnp.concatenate(ps, axis=1), preferred_element_type=F32)
            acc_scr[g] = jnp.concatenate(alphas, axis=1) * acc_scr[g] + pv

    @pl.when(own == 0)
    def _():
        scores(0, 0)

    def pair(n):
        scores(n + 1, 1)
        update(n, 0, False)
        scores(n + 2, 0)
        update(n + 1, 1, False)

    def two_pairs(i, carry):
        pair(4 * i)
        pair(4 * i + 2)
        return carry

    def one_pair(i, carry):
        pair(4 * (own // 4) + 2 * i)
        return carry

    lax.fori_loop(0, own // 4, two_pairs, 0)
    lax.fori_loop(0, (own % 4) // 2, one_pair, 0)

    @pl.when(own % 2 == 0)
    def _():
        update(own, 0, True)
        scores(0, 0, qat_next_ref)

    @pl.when(own % 2 == 1)
    def _():
        scores(own, 1)
        update(own - 1, 0, False)
        scores(0, 0, qat_next_ref)
        update(own, 1, True)
    for g in range(G):
        acc = acc_scr[g]
        o_ref[0, g * hd:(g + 1) * hd, :] = (acc[:hd, :] * (1.0 / acc[hd:hd + 1, :])).astype(BF16)


def _moba_attn(qat, k, vt, B, S):
    nh = qat.shape[1]
    G = MOBA_HEADS_PER_STEP
    BLK = MOBA_BLOCK
    nblk = S // BLK
    hd = HEAD_DIM
    return pl.pallas_call(
        _moba_kernel,
        grid=(B, nh // G, nblk),
        in_specs=[pl.BlockSpec((1, G, LANES, BLK), lambda b, hp, qi: (b, hp, 0, qi)),
                  pl.BlockSpec((1, G, LANES, BLK), lambda b, hp, qi: (b, hp, 0, jnp.minimum(qi + 1, nblk - 1))),
                  pl.BlockSpec((S, G * LANES), lambda b, hp, qi: (b, hp)),
                  pl.BlockSpec((1, G * V_ROWS, S), lambda b, hp, qi: (b, hp, 0))],
        out_specs=pl.BlockSpec((1, G * hd, BLK), lambda b, hp, qi: (b, hp, qi)),
        out_shape=jax.ShapeDtypeStruct((B, nh * hd, S), BF16),
        scratch_shapes=[pltpu.VMEM((2, G, BLK, BLK), F32),
                        pltpu.VMEM((G * (BLK // LANES), 1, LANES), F32),
                        pltpu.VMEM((G, V_ROWS, BLK), F32)],
        compiler_params=_params(("parallel", "parallel", "arbitrary")),
        name="moba_attn",
    )(qat, qat, k, vt)


def _attn_out_kernel(a_ref, x_ref, w_ref, gate_ref, lg_ref, lb_ref, o_ref):
    y = lax.dot_general(a_ref[0], w_ref[...], (((0,), (0,)), ((), ())), preferred_element_type=F32)
    r = DEEPNORM_ALPHA * x_ref[...] + gate_ref[0] * y
    o_ref[...] = _layer_norm(r, lg_ref[...], lb_ref[...])


def _attn_out(at, x2, w, gate, ln_g, ln_b, S):
    T, D = x2.shape
    K = at.shape[1]
    tm = min(1024, S)
    tpb = S // tm
    row = lambda i: (i, 0)
    full2 = lambda i: (0, 0)
    return pl.pallas_call(
        _attn_out_kernel,
        grid=(T // tm,),
        in_specs=[pl.BlockSpec((1, K, tm), lambda i: (i // tpb, 0, i % tpb)), pl.BlockSpec((tm, D), row),
                  pl.BlockSpec((K, D), full2),
                  pl.BlockSpec((1, 1, D), lambda i: (i // tpb, 0, 0)),
                  pl.BlockSpec((1, D), full2), pl.BlockSpec((1, D), full2)],
        out_specs=pl.BlockSpec((tm, D), row),
        out_shape=jax.ShapeDtypeStruct((T, D), F32),
        compiler_params=_params(("parallel",)),
        name="attn_out_ln",
    )(at, x2, w, gate, ln_g.reshape(1, D), ln_b.reshape(1, D))


def kernel(x, c, w_ada, b_ada, ln_g, ln_b, a_w_in, a_conv_w, a_conv_b, a_wq, a_wk, a_wv, a_w_if, a_b_if,
           a_gn_g, a_skip, a_w_out, b_w_kv, b_wq, b_wo, moe_w_router, moe_b_router, moe_w_gate, moe_w_up,
           moe_w_down):
    B, S, D = x.shape
    T = B * S
    assert DEPTH == 2 and S % MOBA_BLOCK == 0 and S % MLSTM_CHUNK == 0
    n_layer_mod = DEPTH * N_MOD_PER_LAYER * D
    cond = _ada_cond(c, w_ada, b_ada)
    mods = cond[:, :n_layer_mod].reshape(B, DEPTH, 2, 3, 1, D)
    kv_mod = cond[:, n_layer_mod:].reshape(B, 2, 1, D)

    def mod3(layer, sub):
        m = mods[:, layer, sub]
        return m[:, 0], m[:, 1], 1.0 + m[:, 2]

    xf = x.reshape(T, D)

    shift, scale, gate = mod3(0, 0)
    nh = MLSTM_HEADS
    xm, z = _inproj(xf, shift, scale, a_w_in[0].astype(BF16), S)
    q, k, v, xc, gts = _qkv(xm, a_conv_w[0], a_conv_b[0], a_wq[0].astype(BF16), a_wk[0].astype(BF16),
                            a_wv[0].astype(BF16), a_w_if[0], a_b_if[0], S)
    g4 = gts.reshape(B, S, 2, nh)
    gcol = jnp.transpose(g4, (0, 3, 1, 2))
    grow = jnp.transpose(g4, (0, 3, 2, 1))
    hn = _mlstm(q, k, v, gcol, grow, B, S)
    xf = _mlstm_out(hn, xc, z, xf, a_gn_g[0], a_skip[0], a_w_out[0].astype(BF16), gate, ln_g[0, 0], ln_b[0, 0], S)
    shift, scale, gate = mod3(0, 1)
    xf = _moe_layer(xf, shift, scale, gate, ln_g[0, 1], ln_b[0, 1], moe_w_router, moe_b_router,
                    moe_w_gate, moe_w_up, moe_w_down, 0, S)

    shift, scale, gate = mod3(1, 0)
    q, k, vt, km = _moba_proj(xf, shift, scale, kv_mod[:, 0], kv_mod[:, 1], b_wq[0].astype(BF16),
                              b_w_kv.astype(BF16), B, S)
    km = km.reshape(B, S // MOBA_BLOCK, km.shape[-1])
    attn_t = _moba_attn(_moba_select(q, km, B, S), k, vt, B, S)
    xf = _attn_out(attn_t, xf, b_wo[0].astype(BF16), gate, ln_g[1, 0], ln_b[1, 0], S)
    shift, scale, gate = mod3(1, 1)
    xf = _moe_layer(xf, shift, scale, gate, ln_g[1, 1], ln_b[1, 1], moe_w_router, moe_b_router,
                    moe_w_gate, moe_w_up, moe_w_down, 1, S)
    return xf.reshape(B, S, D)
```

```python
import functools

import jax
import jax.numpy as jnp
from jax import lax
from jax.experimental import pallas as pl
from jax.experimental.pallas import tpu as pltpu

DEPTH = 2
MLSTM_HEADS = 4
CONV_WIDTH = 4
MLSTM_CHUNK = 512
MOBA_BLOCK = 256
MOBA_TOPK = 3
N_GROUPS = 4
MOE_TOPK = 2
MOE_ROW_BLOCK = 256
MOE_TILE = 512
CHUNK_BITS = 10
RUN_ALIGN = 8
DEEPNORM_ALPHA = (2.0 * DEPTH) ** 0.25
LN_EPS = 1e-5
N_MOD_PER_LAYER = 6

HEAD_DIM = 64
V_ROWS = 80
MOBA_HEADS_PER_STEP = 4
BIAS_LANE = 32
LOG2E = 1.4426950408889634

LANES = 128
CONV_HALO = 16
NEG = -1e30
VMEM_LIMIT = 56 * 1024 * 1024

F32 = jnp.float32
BF16 = jnp.bfloat16


def _sigmoid(x):
    return 1.0 / (1.0 + jnp.exp(-x))


def _params(sem, vmem=VMEM_LIMIT):
    return pltpu.CompilerParams(dimension_semantics=sem, vmem_limit_bytes=vmem)


def _layer_norm(r, g, b):
    mu = jnp.mean(r, axis=-1, keepdims=True)
    d = r - mu
    var = jnp.mean(d * d, axis=-1, keepdims=True)
    return d * lax.rsqrt(var + LN_EPS) * g + b


def _ada_kernel(c_ref, w_ref, b_ref, o_ref):
    c = c_ref[...]
    s = c * _sigmoid(c)
    o_ref[...] = jnp.dot(s.astype(BF16), w_ref[...].astype(BF16),
                         preferred_element_type=F32) + b_ref[...]


def _ada_cond(c, w_ada, b_ada):
    B, D = c.shape
    N = w_ada.shape[1]
    tn = 2048 if N % 2048 == 0 else N
    cp = jnp.zeros((8, D), F32).at[:B].set(c)
    out = pl.pallas_call(
        _ada_kernel,
        grid=(N // tn,),
        in_specs=[pl.BlockSpec((8, D), lambda j: (0, 0)),
                  pl.BlockSpec((D, tn), lambda j: (0, j)),
                  pl.BlockSpec((1, tn), lambda j: (0, j))],
        out_specs=pl.BlockSpec((8, tn), lambda j: (0, j)),
        out_shape=jax.ShapeDtypeStruct((8, N), F32),
        compiler_params=_params(("arbitrary",)),
        name="ada_cond",
    )(cp, w_ada, b_ada.reshape(1, N))
    return out[:B]


def _inproj_kernel(x_ref, sh_ref, sc_ref, w_ref, xm_ref, z_ref):
    di = xm_ref.shape[-1]
    h = x_ref[...] * (1.0 + sc_ref[0]) + sh_ref[0]
    r = jnp.dot(h.astype(BF16), w_ref[...], preferred_element_type=F32)
    xm_ref[...] = r[:, :di].astype(BF16)
    z_ref[...] = r[:, di:].astype(BF16)


def _inproj(x2, shift, scale, w_in, S):
    T, D = x2.shape
    di = w_in.shape[1] // 2
    tm = min(512, S)
    tpb = S // tm
    return pl.pallas_call(
        _inproj_kernel,
        grid=(T // tm,),
        in_specs=[pl.BlockSpec((tm, D), lambda i: (i, 0)),
                  pl.BlockSpec((1, 1, D), lambda i: (i // tpb, 0, 0)),
                  pl.BlockSpec((1, 1, D), lambda i: (i // tpb, 0, 0)),
                  pl.BlockSpec((D, 2 * di), lambda i: (0, 0))],
        out_specs=[pl.BlockSpec((tm, di), lambda i: (i, 0)),
                   pl.BlockSpec((tm, di), lambda i: (i, 0))],
        out_shape=[jax.ShapeDtypeStruct((T, di), BF16)] * 2,
        compiler_params=_params(("parallel",)),
        name="mlstm_inproj",
    )(x2, shift, scale, w_in)


def _qkv_kernel(xm_ref, halo_ref, cw_ref, cb_ref, wq_ref, wk_ref, wv_ref, wif_ref, bif_ref,
                q_ref, k_ref, v_ref, xc_ref, g_ref, *, tpb, nh, kscale):
    tm, di = xm_ref.shape
    dh = di // nh
    i = pl.program_id(0)
    xm = xm_ref[...]
    xf = xm.astype(F32)
    halo = halo_ref[...].astype(F32)
    halo = jnp.where(i % tpb == 0, 0.0, halo)
    ext = jnp.concatenate([halo, xf], axis=0)
    acc = cb_ref[...] + cw_ref[CONV_WIDTH - 1:CONV_WIDTH, :] * xf
    for s in range(1, CONV_WIDTH):
        acc = acc + cw_ref[CONV_WIDTH - 1 - s:CONV_WIDTH - s, :] * ext[CONV_HALO - s:CONV_HALO - s + tm, :]
    xc = acc * _sigmoid(acc)
    xcb = xc.astype(BF16)
    xc_ref[...] = xcb
    for h in range(nh):
        sl = slice(h * dh, (h + 1) * dh)
        q_ref[:, sl] = jnp.dot(xcb[:, sl], wq_ref[h], preferred_element_type=F32).astype(BF16)
        k_ref[:, sl] = (jnp.dot(xcb[:, sl], wk_ref[h], preferred_element_type=F32) * kscale).astype(BF16)
        v_ref[:, sl] = jnp.dot(xm[:, sl], wv_ref[h], preferred_element_type=F32).astype(BF16)
    g = jnp.dot(xcb, wif_ref[...], preferred_element_type=F32) + bif_ref[...]
    col = lax.broadcasted_iota(jnp.int32, g.shape, 1)
    logsig = jnp.minimum(g, 0.0) - jnp.log(1.0 + jnp.exp(-jnp.abs(g)))
    g = jnp.where(col >= nh, logsig, g)
    g_ref[...] = g[:, :2 * nh]


def _qkv(xm, conv_w, conv_b, wq, wk, wv, w_if, b_if, S):
    T, di = xm.shape
    nh = wq.shape[0]
    dh = di // nh
    tm = min(512, S)
    tpb = S // tm
    hb = tm // CONV_HALO
    wif = jnp.zeros((di, LANES), BF16).at[:, :2 * nh].set(w_if.astype(BF16))
    bif = jnp.zeros((1, LANES), F32).at[0, :2 * nh].set(b_if)
    kern = functools.partial(_qkv_kernel, tpb=tpb, nh=nh, kscale=float(dh) ** -0.5)
    full2 = lambda i: (0, 0)
    full3 = lambda i: (0, 0, 0)
    row = lambda i: (i, 0)
    return pl.pallas_call(
        kern,
        grid=(T // tm,),
        in_specs=[pl.BlockSpec((tm, di), row),
                  pl.BlockSpec((CONV_HALO, di), lambda i: (jnp.maximum(i * hb - 1, 0), 0)),
                  pl.BlockSpec((CONV_WIDTH, di), full2),
                  pl.BlockSpec((1, di), full2),
                  pl.BlockSpec((nh, dh, dh), full3),
                  pl.BlockSpec((nh, dh, dh), full3),
                  pl.BlockSpec((nh, dh, dh), full3),
                  pl.BlockSpec((di, LANES), full2),
                  pl.BlockSpec((1, LANES), full2)],
        out_specs=[pl.BlockSpec((tm, di), row)] * 4 + [pl.BlockSpec((tm, 2 * nh), row)],
        out_shape=[jax.ShapeDtypeStruct((T, di), BF16)] * 4 + [jax.ShapeDtypeStruct((T, 2 * nh), F32)],
        compiler_params=_params(("parallel",)),
        name="mlstm_qkv",
    )(xm, xm, conv_w, conv_b.reshape(1, di), wq, wk, wv, wif, bif)


def _mlstm_kernel(q_ref, k_ref, v_ref, gc_ref, gr_ref, o_ref, c_scr, n_scr, m_scr):
    L = q_ref.shape[0]
    nh = c_scr.shape[0]
    dh = c_scr.shape[1]

    @pl.when(pl.program_id(1) == 0)
    def _():
        c_scr[...] = jnp.zeros_like(c_scr)
        n_scr[...] = jnp.zeros_like(n_scr)
        m_scr[...] = jnp.zeros_like(m_scr)

    t_idx = lax.broadcasted_iota(jnp.int32, (L, L), 0)
    s_idx = lax.broadcasted_iota(jnp.int32, (L, L), 1)
    causal = s_idx <= t_idx
    for h in range(nh):
        hs = slice(h * dh, (h + 1) * dh)
        _mlstm_head(q_ref[:, hs], k_ref[:, hs], v_ref[:, hs], gc_ref[0, h], gr_ref[0, h],
                    o_ref.at[:, hs], c_scr.at[h], n_scr.at[h], m_scr.at[h], causal, t_idx, s_idx)


def _mlstm_head(q, k, v, gc, gr, o_ref, c_scr, n_scr, m_scr, causal, t_idx, s_idx):
    i_col, f_col = gc[:, 0:1], gc[:, 1:2]
    i_row, f_row = gr[0:1, :], gr[1:2, :]
    b_col = jnp.sum(jnp.where(causal, f_row, 0.0), axis=1, keepdims=True)
    b_row = jnp.sum(jnp.where(t_idx <= s_idx, f_col, 0.0), axis=0, keepdims=True)
    m_prev = m_scr[...]
    d = jnp.where(causal, b_col - b_row + i_row, -jnp.inf)
    a_col = b_col + m_prev
    m_t = jnp.maximum(a_col, jnp.max(d, axis=1, keepdims=True))
    w_intra = jnp.exp(d - m_t)
    w_inter = jnp.exp(a_col - m_t)
    qk = lax.dot_general(q, k, (((1,), (1,)), ((), ())), preferred_element_type=F32)
    s_mat = qk * w_intra
    c_b = c_scr[...].astype(BF16)
    inter = lax.dot_general(q, c_b, (((1,), (1,)), ((), ())), preferred_element_type=F32)
    num = jnp.dot(s_mat.astype(BF16), v, preferred_element_type=F32) + w_inter * inter
    qn = jnp.sum(q.astype(F32) * n_scr[...], axis=1, keepdims=True)
    den = jnp.sum(s_mat, axis=1, keepdims=True) + w_inter * qn
    hcap = num / jnp.maximum(jnp.abs(den), jnp.exp(-m_t))
    mu = jnp.mean(hcap, axis=1, keepdims=True)
    dv = hcap - mu
    var = jnp.mean(dv * dv, axis=1, keepdims=True)
    o_ref[...] = (dv * lax.rsqrt(var + LN_EPS)).astype(BF16)

    b_last = jnp.sum(f_row, axis=1, keepdims=True)
    ws_col = b_last - b_col + i_col
    m_new = jnp.maximum(b_last + m_prev, jnp.max(ws_col, axis=0, keepdims=True))
    decay = jnp.exp(b_last + m_prev - m_new)
    ws = jnp.exp(ws_col - m_new)
    vw = (v.astype(F32) * ws).astype(BF16)
    upd = lax.dot_general(vw, k, (((0,), (0,)), ((), ())), preferred_element_type=F32)
    c_scr[...] = decay * c_scr[...] + upd
    n_scr[...] = decay * n_scr[...] + jnp.sum(k.astype(F32) * ws, axis=0, keepdims=True)
    m_scr[...] = m_new


def _mlstm(q, k, v, gcol, grow, B, S):
    T, di = q.shape
    nh = gcol.shape[1]
    dh = di // nh
    L = min(MLSTM_CHUNK, S)
    nc = S // L
    blk = pl.BlockSpec((L, di), lambda b, c: (b * nc + c, 0))
    return pl.pallas_call(
        _mlstm_kernel,
        grid=(B, nc),
        in_specs=[blk, blk, blk,
                  pl.BlockSpec((1, nh, L, 2), lambda b, c: (b, 0, c, 0)),
                  pl.BlockSpec((1, nh, 2, L), lambda b, c: (b, 0, 0, c))],
        out_specs=blk,
        out_shape=jax.ShapeDtypeStruct((T, di), BF16),
        scratch_shapes=[pltpu.VMEM((nh, dh, dh), F32), pltpu.VMEM((nh, 1, dh), F32), pltpu.VMEM((nh, 1, 1), F32)],
        compiler_params=_params(("parallel", "arbitrary")),
        name="mlstm_scan",
    )(q, k, v, gcol, grow)


def _mlstm_out_kernel(hn_ref, xc_ref, z_ref, x_ref, gn_ref, skip_ref, w_ref, gate_ref, lg_ref, lb_ref, o_ref):
    z = z_ref[...].astype(F32)
    u = (hn_ref[...].astype(F32) * gn_ref[...] + skip_ref[...] * xc_ref[...].astype(F32)) * (z * _sigmoid(z))
    y = jnp.dot(u.astype(BF16), w_ref[...], preferred_element_type=F32)
    r = DEEPNORM_ALPHA * x_ref[...] + gate_ref[0] * y
    o_ref[...] = _layer_norm(r, lg_ref[...], lb_ref[...])


def _mlstm_out(hn, xc, z, x2, gn_g, skip, w_out, gate, ln_g, ln_b, S):
    T, D = x2.shape
    di = hn.shape[1]
    tm = min(512, S)
    tpb = S // tm
    row = lambda i: (i, 0)
    full2 = lambda i: (0, 0)
    return pl.pallas_call(
        _mlstm_out_kernel,
        grid=(T // tm,),
        in_specs=[pl.BlockSpec((tm, di), row), pl.BlockSpec((tm, di), row), pl.BlockSpec((tm, di), row),
                  pl.BlockSpec((tm, D), row),
                  pl.BlockSpec((1, di), full2), pl.BlockSpec((1, di), full2),
                  pl.BlockSpec((di, D), full2),
                  pl.BlockSpec((1, 1, D), lambda i: (i // tpb, 0, 0)),
                  pl.BlockSpec((1, D), full2), pl.BlockSpec((1, D), full2)],
        out_specs=pl.BlockSpec((tm, D), row),
        out_shape=jax.ShapeDtypeStruct((T, D), F32),
        compiler_params=_params(("parallel",)),
        name="mlstm_out_ln",
    )(hn, xc, z, x2, gn_g.reshape(1, di), skip.reshape(1, di), w_out, gate, ln_g.reshape(1, D), ln_b.reshape(1, D))


def _route_kernel(x_ref, sh_ref, sc_ref, wrt_ref, br_ref, tri_ref, out_ref, cnt_ref):
    E = wrt_ref.shape[0]
    ng = N_GROUPS
    epg = E // ng
    tm = x_ref.shape[0]
    h = x_ref[...] * (1.0 + sc_ref[0]) + sh_ref[0]
    nt = (((1,), (1,)), ((), ()))
    w = wrt_ref[...]
    w_hi = w.astype(BF16)
    w_lo = (w - w_hi.astype(F32)).astype(BF16)
    h_hi = h.astype(BF16)
    h_lo = (h - h_hi.astype(F32)).astype(BF16)
    lt = (lax.dot_general(w_hi, h_hi, nt, preferred_element_type=F32)
          + lax.dot_general(w_hi, h_lo, nt, preferred_element_type=F32)
          + lax.dot_general(w_lo, h_hi, nt, preferred_element_type=F32))
    ex = jnp.exp(lt - jnp.max(lt, axis=0, keepdims=True))
    probs = ex / jnp.sum(ex, axis=0, keepdims=True)
    sel = probs + br_ref[...]
    member = [sel[k * ng:(k + 1) * ng, :] for k in range(epg)]
    pmember = [probs[k * ng:(k + 1) * ng, :] for k in range(epg)]
    gscore = None
    for a in range(epg):
        for b in range(a + 1, epg):
            pair = member[a] + member[b]
            gscore = pair if gscore is None else jnp.maximum(gscore, pair)
    gidx = lax.broadcasted_iota(jnp.int32, (ng, tm), 0)
    gmax = jnp.max(gscore, axis=0, keepdims=True)
    chosen = gidx == jnp.min(jnp.where(gscore == gmax, gidx, ng), axis=0, keepdims=True)
    zero_g = jnp.zeros((ng, tm), F32)
    e1, e2, p1, p2 = zero_g, zero_g, zero_g, zero_g
    firsts, seconds = [], []
    for k in range(epg):
        rank = zero_g
        for o in range(epg):
            if o != k:
                beats = (member[o] >= member[k]) if o < k else (member[o] > member[k])
                rank = rank + jnp.where(beats, 1.0, 0.0)
        is1 = chosen & (rank == 0.0)
        is2 = chosen & (rank == 1.0)
        firsts.append(jnp.where(is1, 1.0, 0.0))
        seconds.append(jnp.where(is2, 1.0, 0.0))
        eid = (gidx * epg + k).astype(F32)
        e1 = e1 + jnp.where(is1, eid, 0.0)
        e2 = e2 + jnp.where(is2, eid, 0.0)
        p1 = p1 + jnp.where(is1, pmember[k], 0.0)
        p2 = p2 + jnp.where(is2, pmember[k], 0.0)
    e1, e2, p1, p2 = [jnp.sum(v, axis=0, keepdims=True) for v in (e1, e2, p1, p2)]
    first = jnp.concatenate(firsts, axis=0)
    second = jnp.concatenate(seconds, axis=0)
    mask = first + second
    prefix = jnp.dot(mask.astype(BF16), tri_ref[...], preferred_element_type=F32)
    counts = jnp.sum(mask, axis=1, keepdims=True)
    padded = jnp.floor((counts + (RUN_ALIGN - 1)) * (1.0 / RUN_ALIGN)) * RUN_ALIGN
    row_of = [(e % epg) * ng + e // epg for e in range(E)]
    starts = [None] * E
    acc = jnp.zeros((1, 1), F32)
    for e in range(E):
        starts[row_of[e]] = acc
        acc = acc + padded[row_of[e]:row_of[e] + 1, :]
    slot = prefix + jnp.concatenate(starts, axis=0)
    s1 = jnp.sum(first * slot, axis=0, keepdims=True)
    s2 = jnp.sum(second * slot, axis=0, keepdims=True)
    psum = p1 + p2
    zero = jnp.zeros((1, tm), F32)
    out_ref[...] = jnp.concatenate([e1, e2, p1 / psum, p2 / psum, s1, s2, zero, zero], axis=0)
    counts_by_expert = jnp.concatenate([counts[row_of[e]:row_of[e] + 1, :] for e in range(E)], axis=0)
    cnt_ref[...] = jnp.broadcast_to(counts_by_expert, cnt_ref.shape)


def _route(x2, shift, scale, w_router, b_router, S):
    T, D = x2.shape
    E = w_router.shape[1]
    tm = min(MOE_TILE, S)
    tpb = S // tm
    tri = (jnp.arange(tm)[:, None] < jnp.arange(tm)[None, :]).astype(BF16)
    epg = E // N_GROUPS
    rows = jnp.array([g * epg + k for k in range(epg) for g in range(N_GROUPS)], jnp.int32)
    return pl.pallas_call(
        _route_kernel,
        grid=(T // tm,),
        in_specs=[pl.BlockSpec((tm, D), lambda i: (i, 0)),
                  pl.BlockSpec((1, 1, D), lambda i: (i // tpb, 0, 0)),
                  pl.BlockSpec((1, 1, D), lambda i: (i // tpb, 0, 0)),
                  pl.BlockSpec((E, D), lambda i: (0, 0)),
                  pl.BlockSpec((E, 1), lambda i: (0, 0)),
                  pl.BlockSpec((tm, tm), lambda i: (0, 0))],
        out_specs=[pl.BlockSpec((8, tm), lambda i: (0, i)),
                   pl.BlockSpec((E, LANES), lambda i: (0, i))],
        out_shape=[jax.ShapeDtypeStruct((8, T), F32), jax.ShapeDtypeStruct((E, (T // tm) * LANES), F32)],
        compiler_params=_params(("parallel",)),
        name="moe_route",
    )(x2, shift, scale, w_router.T[rows], b_router[rows].reshape(E, 1), tri)


def _stage_rows(tm, n_exp):
    return MOE_TOPK * tm + n_exp * RUN_ALIGN


def _chunk_copies(tc_ref, ts_ref, ds_ref, step, n_exp, make):
    out = []
    for e in range(n_exp):
        cnt = tc_ref[step * n_exp + e]
        off = ts_ref[step * n_exp + e]
        dst = ds_ref[step * n_exp + e]
        for k in reversed(range(RUN_ALIGN.bit_length() - 1, CHUNK_BITS)):
            done = (cnt >> (k + 1)) << (k + 1)
            out.append(((cnt & (1 << k)) != 0,
                        make(pl.multiple_of(off + done, RUN_ALIGN), pl.multiple_of(dst + done, RUN_ALIGN), 1 << k)))
    return out


def _dispatch_kernel(tc_ref, ts_ref, ds_ref, pe_ref, x_ref, sh_ref, sc_ref, pos_ref, xs_ref,
                     stage, zbuf, sem, zsem):
    tm = x_ref.shape[0]
    rb = zbuf.shape[0]
    i = pl.program_id(0)
    slot = i % 2

    n_exp = pe_ref.shape[0] - 1
    n_blocks = xs_ref.shape[0] // rb

    def pad_copy(e):
        return pltpu.make_async_copy(zbuf, xs_ref.at[pl.ds(pl.multiple_of(pe_ref[e + 1] - rb, rb), rb)], zsem)

    def tail_copy(j):
        return pltpu.make_async_copy(zbuf, xs_ref.at[pl.ds(pl.multiple_of(pe_ref[n_exp] + j * rb, rb), rb)], zsem)

    def tail_live(j):
        return pe_ref[n_exp] // rb + j < n_blocks

    @pl.when(i == 0)
    def _():
        zbuf[...] = jnp.zeros_like(zbuf)
        for e in range(n_exp):
            @pl.when(pe_ref[e + 1] > pe_ref[e])
            def _():
                pad_copy(e).start()
        for j in range(n_exp):
            @pl.when(tail_live(j))
            def _():
                tail_copy(j).start()
        for e in range(n_exp):
            @pl.when(pe_ref[e + 1] > pe_ref[e])
            def _():
                pad_copy(e).wait()
        for j in range(n_exp):
            @pl.when(tail_live(j))
            def _():
                tail_copy(j).wait()

    n_tiles = pl.num_programs(0) - 1

    @pl.when(i < n_tiles)
    def _():
        h = (x_ref[...] * (1.0 + sc_ref[0]) + sh_ref[0]).astype(BF16)
        pos = pos_ref[0]
        slot_id = lax.broadcasted_iota(jnp.int32, (stage.shape[1], tm), 0)
        perm = jnp.where((slot_id == pos[0:1, :]) | (slot_id == pos[1:2, :]), 1.0, 0.0).astype(BF16)
        stage[slot] = jnp.dot(perm, h, preferred_element_type=F32)

    def copies(step, s):
        def make(src_row, dst_row, size):
            return pltpu.make_async_copy(stage.at[s, pl.ds(src_row, size)], xs_ref.at[pl.ds(dst_row, size)], sem.at[s])
        return _chunk_copies(tc_ref, ts_ref, ds_ref, step, n_exp, make)

    for live, cp in copies(jnp.minimum(i, n_tiles - 1), slot):
        pl.when(live & (i < n_tiles))(cp.start)
    for live, cp in copies(jnp.maximum(i - 1, 0), 1 - slot):
        pl.when(live & (i > 0))(cp.wait)


def _dispatch(x2, shift, scale, posrow, tcnt, tstart, dstart, pends0, cap, S):
    T, D = x2.shape
    tm = min(MOE_TILE, S)
    tpb = S // tm
    nt = T // tm
    tile = lambda i: jnp.minimum(i, nt - 1)
    mod = pl.BlockSpec((1, 1, D), lambda i, *_: (tile(i) // tpb, 0, 0))
    grid_spec = pltpu.PrefetchScalarGridSpec(
        num_scalar_prefetch=4,
        grid=(nt + 1,),
        in_specs=[pl.BlockSpec((tm, D), lambda i, *_: (tile(i), 0)), mod, mod,
                  pl.BlockSpec((1, 2, tm), lambda i, *_: (tile(i), 0, 0))],
        out_specs=pl.BlockSpec(memory_space=pl.ANY),
        scratch_shapes=[pltpu.VMEM((2, _stage_rows(tm, pends0.shape[0] - 1), D), F32),
                        pltpu.VMEM((MOE_ROW_BLOCK, D), F32),
                        pltpu.SemaphoreType.DMA((2,)), pltpu.SemaphoreType.DMA(())],
    )
    return pl.pallas_call(
        _dispatch_kernel,
        grid_spec=grid_spec,
        out_shape=jax.ShapeDtypeStruct((cap, D), F32),
        compiler_params=_params(("arbitrary",)),
        name="moe_dispatch",
    )(tcnt, tstart, dstart, pends0, x2, shift, scale, posrow)


def _expert_kernel(be_ref, na_ref, xs_ref, wg_ref, wu_ref, wd_ref, ys_ref, wg_b, wu_b, wd_b):
    i = pl.program_id(0)
    active = i < na_ref[0]

    @pl.when(active & ((i == 0) | (be_ref[i] != be_ref[jnp.maximum(i - 1, 0)])))
    def _():
        wg_b[...] = wg_ref[0, 0].astype(BF16)
        wu_b[...] = wu_ref[0, 0].astype(BF16)
        wd_b[...] = wd_ref[0, 0].astype(BF16)

    @pl.when(active)
    def _():
        xb = xs_ref[...].astype(BF16)
        g = jnp.dot(xb, wg_b[...], preferred_element_type=F32)
        u = jnp.dot(xb, wu_b[...], preferred_element_type=F32)
        a = (g * _sigmoid(g)) * u
        ys_ref[...] = jnp.dot(a.astype(BF16), wd_b[...], preferred_element_type=F32)

    @pl.when(jnp.logical_not(active))
    def _():
        ys_ref[...] = jnp.zeros_like(ys_ref)


def _experts(xs, block_expert, n_active, w_gate, w_up, w_down, layer):
    cap, D = xs.shape
    _, E, _, F = w_gate.shape
    rb = MOE_ROW_BLOCK
    nb = cap // rb

    def blk(i, be, na):
        return jnp.maximum(jnp.minimum(i, na[0] - 1), 0)

    grid_spec = pltpu.PrefetchScalarGridSpec(
        num_scalar_prefetch=2,
        grid=(nb,),
        in_specs=[pl.BlockSpec((rb, D), lambda i, be, na: (blk(i, be, na), 0)),
                  pl.BlockSpec((1, 1, D, F), lambda i, be, na: (layer, be[blk(i, be, na)], 0, 0)),
                  pl.BlockSpec((1, 1, D, F), lambda i, be, na: (layer, be[blk(i, be, na)], 0, 0)),
                  pl.BlockSpec((1, 1, F, D), lambda i, be, na: (layer, be[blk(i, be, na)], 0, 0))],
        out_specs=pl.BlockSpec((rb, D), lambda i, be, na: (i, 0)),
        scratch_shapes=[pltpu.VMEM((D, F), BF16), pltpu.VMEM((D, F), BF16), pltpu.VMEM((F, D), BF16)],
    )
    return pl.pallas_call(
        _expert_kernel,
        grid_spec=grid_spec,
        out_shape=jax.ShapeDtypeStruct((cap, D), F32),
        compiler_params=_params(("arbitrary",)),
        name="moe_experts",
    )(block_expert, n_active, xs, w_gate, w_up, w_down)


def _combine_kernel(tc_ref, ts_ref, ds_ref, ys_ref, rt_ref, x_ref, gate_ref, lg_ref, lb_ref, o_ref, stage, sem,
                    *, n_exp):
    tm = x_ref.shape[0]
    i = pl.program_id(0)
    n_tiles = pl.num_programs(0) - 1
    slot = i % 2

    def copies(step, s):
        def make(stage_row, ys_row, size):
            return pltpu.make_async_copy(ys_ref.at[pl.ds(ys_row, size)], stage.at[s, pl.ds(stage_row, size)], sem.at[s])
        return _chunk_copies(tc_ref, ts_ref, ds_ref, step, n_exp, make)

    @pl.when(i == 0)
    def _():
        stage[...] = jnp.zeros_like(stage)

    for live, cp in copies(jnp.minimum(i, n_tiles - 1), slot):
        pl.when(live & (i < n_tiles))(cp.start)
    for live, cp in copies(jnp.maximum(i - 1, 0), 1 - slot):
        pl.when(live & (i > 0))(cp.wait)

    @pl.when(i > 0)
    def _():
        rt = rt_ref[...]
        ysb = stage[1 - slot].astype(BF16)
        slot_id = lax.broadcasted_iota(jnp.int32, (tm, stage.shape[1]), 1).astype(F32)
        y = None
        for j in range(MOE_TOPK):
            perm = jnp.where(slot_id == rt[:, 4 + j:5 + j], 1.0, 0.0).astype(BF16)
            part = rt[:, 2 + j:3 + j] * jnp.dot(perm, ysb, preferred_element_type=F32)
            y = part if y is None else y + part
        r = DEEPNORM_ALPHA * x_ref[...] + gate_ref[0] * y
        o_ref[...] = _layer_norm(r, lg_ref[...], lb_ref[...])


def _combine(ys, route_t, x2, gate, ln_g, ln_b, tcnt, tstart, dstart, S):
    T, D = x2.shape
    tm = min(MOE_TILE, S)
    tpb = S // tm
    tile = lambda i: jnp.maximum(i - 1, 0)
    grid_spec = pltpu.PrefetchScalarGridSpec(
        num_scalar_prefetch=3,
        grid=(T // tm + 1,),
        in_specs=[pl.BlockSpec(memory_space=pl.ANY),
                  pl.BlockSpec((tm, 8), lambda i, *_: (tile(i), 0)),
                  pl.BlockSpec((tm, D), lambda i, *_: (tile(i), 0)),
                  pl.BlockSpec((1, 1, D), lambda i, *_: (tile(i) // tpb, 0, 0)),
                  pl.BlockSpec((1, D), lambda i, *_: (0, 0)),
                  pl.BlockSpec((1, D), lambda i, *_: (0, 0))],
        out_specs=pl.BlockSpec((tm, D), lambda i, *_: (tile(i), 0)),
        scratch_shapes=[pltpu.VMEM((2, _stage_rows(tm, tcnt.shape[0] // (T // tm)), D), F32),
                        pltpu.SemaphoreType.DMA((2,))],
    )
    return pl.pallas_call(
        functools.partial(_combine_kernel, n_exp=tcnt.shape[0] // (T // tm)),
        grid_spec=grid_spec,
        out_shape=jax.ShapeDtypeStruct((T, D), F32),
        compiler_params=_params(("arbitrary",)),
        name="moe_combine_ln",
    )(tcnt, tstart, dstart, ys, route_t, x2, gate, ln_g.reshape(1, D), ln_b.reshape(1, D))


def _moe_layer(x2, shift, scale, gate, ln_g, ln_b, w_router, b_router, w_gate, w_up, w_down, layer, S):
    T, D = x2.shape
    E = w_router.shape[1]
    rb = MOE_ROW_BLOCK
    tm = min(MOE_TILE, S)
    nt = T // tm
    assert tm < (1 << CHUNK_BITS)
    route, cnt = _route(x2, shift, scale, w_router, b_router, S)
    tcnt = cnt.reshape(E, nt, LANES)[:, :, 0].T.astype(jnp.int32)
    tcnt = (tcnt + RUN_ALIGN - 1) // RUN_ALIGN * RUN_ALIGN
    base = jnp.cumsum(tcnt, axis=0) - tcnt
    counts = jnp.sum(tcnt, axis=0)
    padded = (counts + rb - 1) // rb * rb
    pends = jnp.cumsum(padded)
    pstarts = pends - padded
    tstart = jnp.cumsum(tcnt, axis=1) - tcnt
    dstart = pstarts[None, :] + base
    posrow = jnp.transpose(route[4:6].astype(jnp.int32).reshape(2, nt, tm), (1, 0, 2))
    nb = -(-(T * MOE_TOPK + nt * E * (RUN_ALIGN - 1)) // rb) + E
    cap = nb * rb
    block_start = jnp.arange(nb, dtype=jnp.int32) * rb
    block_expert = jnp.minimum(jnp.sum(block_start[:, None] >= pends[None, :], axis=1), E - 1).astype(jnp.int32)
    n_active = (pends[-1:] // rb).astype(jnp.int32)
    pends0 = jnp.concatenate([jnp.zeros((1,), jnp.int32), pends.astype(jnp.int32)])
    tables = [t.reshape(-1).astype(jnp.int32) for t in (tcnt, tstart, dstart)]
    xs = _dispatch(x2, shift, scale, posrow, *tables, pends0, cap, S)
    ys = _experts(xs, block_expert, n_active, w_gate, w_up, w_down, layer)
    return _combine(ys, route.T, x2, gate, ln_g, ln_b, *tables, S)


def _moba_proj_kernel(x_ref, shq_ref, scq_ref, shkv_ref, sckv_ref, wq_ref, wkv_ref,
                      q_ref, k_ref, vt_ref, km_ref, *, qscale, tpb, slopes):
    tm, D = x_ref.shape
    hd = HEAD_DIM
    nh = D // hd
    x = x_ref[...]
    hq = x * (1.0 + scq_ref[0]) + shq_ref[0]
    hkv = x * (1.0 + sckv_ref[0]) + shkv_ref[0]
    qf = jnp.dot(hq.astype(BF16), wq_ref[...], preferred_element_type=F32) * qscale
    kv = jnp.dot(hkv.astype(BF16), wkv_ref[...], preferred_element_type=F32)
    kf = kv[:, :D]
    vt = kv[:, D:].T
    ones_pad = jnp.where(lax.broadcasted_iota(jnp.int32, (V_ROWS - hd, tm), 0) == 0, 1.0, 0.0)
    for g in range(nh):
        vt_ref[0, g * V_ROWS:(g + 1) * V_ROWS, :] = jnp.concatenate(
            [vt[g * hd:(g + 1) * hd, :], ones_pad], axis=0).astype(BF16)
    lane = lax.broadcasted_iota(jnp.int32, (1, LANES), 1)
    head_lanes = lane >= hd
    pos = (pl.program_id(0) % tpb) * tm + lax.broadcasted_iota(jnp.int32, (tm, 1), 0)
    pos_f = pos.astype(F32)
    onehot = jnp.where(lane == pos // MOBA_BLOCK, 1.0, 0.0)
    q_fill = jnp.where((lane >= BIAS_LANE) & (lane < BIAS_LANE + 3), 1.0, 0.0)
    slope_row = functools.reduce(
        lambda acc, g: jnp.where(lane == g, slopes[g] * LOG2E, acc), range(nh), jnp.zeros((1, LANES), F32))
    c_all = pos_f * slope_row
    c_hi = c_all.astype(BF16).astype(F32)
    c_mid = (c_all - c_hi).astype(BF16).astype(F32)
    c_lo = c_all - c_hi - c_mid
    nbt = tm // MOBA_BLOCK
    means = [jnp.mean(kf[j * MOBA_BLOCK:(j + 1) * MOBA_BLOCK, :], axis=0, keepdims=True) for j in range(nbt)]
    km = jnp.concatenate(means + [jnp.zeros((8 - nbt, D), F32)], axis=0)
    for j in range(nh // 2):
        cs = slice(j * LANES, (j + 1) * LANES)
        qb, kb, mb = qf[:, cs], kf[:, cs], km[:, cs]
        q_pair = (pltpu.roll(qb, hd, axis=1), qb)
        k_pair = (pltpu.roll(kb, hd, axis=1), kb)
        m_pair = (pltpu.roll(mb, hd, axis=1), mb)
        for e in range(2):
            g = 2 * j + e
            gs = slice(g * LANES, (g + 1) * LANES)
            q_ref[:, gs] = jnp.where(head_lanes, q_pair[e], q_fill).astype(BF16)
            aug = jnp.where(lane == BIAS_LANE, c_hi[:, g:g + 1],
                            jnp.where(lane == BIAS_LANE + 1, c_mid[:, g:g + 1],
                                      jnp.where(lane == BIAS_LANE + 2, c_lo[:, g:g + 1], onehot)))
            k_ref[:, gs] = jnp.where(head_lanes, k_pair[e], aug).astype(BF16)
            mg = jnp.where(head_lanes, m_pair[e], 0.0)
            for jb in range(nbt):
                km_ref[jb, :, gs] = mg[jb:jb + 1, :]


def _moba_proj(x2, shq, scq, shkv, sckv, wq, wkv, B, S):
    T, D = x2.shape
    nh = D // HEAD_DIM
    tm = min(512, S)
    tpb = S // tm
    nbt = tm // MOBA_BLOCK
    assert S // MOBA_BLOCK <= BIAS_LANE and nbt <= 8
    slopes = tuple(2.0 ** (-8.0 * (h + 1.0) / nh) for h in range(nh))
    mod = pl.BlockSpec((1, 1, D), lambda i: (i // tpb, 0, 0))
    kern = functools.partial(_moba_proj_kernel, qscale=float(HEAD_DIM) ** -0.5 * LOG2E, tpb=tpb, slopes=slopes)
    return pl.pallas_call(
        kern,
        grid=(T // tm,),
        in_specs=[pl.BlockSpec((tm, D), lambda i: (i, 0)), mod, mod, mod, mod,
                  pl.BlockSpec((D, D), lambda i: (0, 0)),
                  pl.BlockSpec((D, 2 * D), lambda i: (0, 0))],
        out_specs=[pl.BlockSpec((tm, nh * LANES), lambda i: (i, 0)),
                   pl.BlockSpec((tm, nh * LANES), lambda i: (i, 0)),
                   pl.BlockSpec((1, nh * V_ROWS, tm), lambda i: (i // tpb, 0, i % tpb)),
                   pl.BlockSpec((nbt, 1, nh * LANES), lambda i: (i, 0, 0))],
        out_shape=[jax.ShapeDtypeStruct((T, nh * LANES), BF16), jax.ShapeDtypeStruct((T, nh * LANES), BF16),
                   jax.ShapeDtypeStruct((B, nh * V_ROWS, S), BF16),
                   jax.ShapeDtypeStruct((T // MOBA_BLOCK, 1, nh * LANES), F32)],
        compiler_params=_params(("parallel",)),
        name="moba_proj",
    )(x2, shq, scq, shkv, sckv, wq, wkv)


def _moba_select_kernel(q_ref, km_ref, qat_ref):
    tq = q_ref.shape[0]
    G = q_ref.shape[1] // LANES
    nblk = km_ref.shape[1]
    blk_id = lax.broadcasted_iota(jnp.int32, (nblk, tq), 0)
    own = (pl.program_id(2) * tq + lax.broadcasted_iota(jnp.int32, (1, tq), 1)) // MOBA_BLOCK
    for g in range(G):
        gs = slice(g * LANES, (g + 1) * LANES)
        qgt = q_ref[:, gs].astype(F32).T
        qb = qgt.astype(BF16)
        km = km_ref[0, :, gs]
        km_hi = km.astype(BF16)
        km_mid = (km - km_hi.astype(F32)).astype(BF16)
        km_lo = (km - km_hi.astype(F32) - km_mid.astype(F32)).astype(BF16)
        gate = (jnp.dot(km_hi, qb, preferred_element_type=F32) + jnp.dot(km_mid, qb, preferred_element_type=F32)
                + jnp.dot(km_lo, qb, preferred_element_type=F32))
        gsc = jnp.where(blk_id < own, gate, -jnp.inf)
        picked = blk_id < 0
        for _ in range(MOBA_TOPK):
            mx = jnp.max(gsc, axis=0, keepdims=True)
            first = jnp.min(jnp.where(gsc == mx, blk_id, nblk), axis=0, keepdims=True)
            pick = blk_id == first
            picked = picked | pick
            gsc = jnp.where(pick, -jnp.inf, gsc)
        visible = (picked & (blk_id < own)) | (blk_id == own)
        pen = jnp.where(visible, 0.0, NEG)
        qat_ref[0, g] = (qgt + jnp.concatenate([pen, jnp.zeros((LANES - nblk, tq), F32)], axis=0)).astype(BF16)


def _moba_select(q, km, B, S):
    nh = q.shape[1] // LANES
    G = MOBA_HEADS_PER_STEP
    tq = min(1024, S)
    nblk = S // MOBA_BLOCK
    return pl.pallas_call(
        _moba_select_kernel,
        grid=(B, nh // G, S // tq),
        in_specs=[pl.BlockSpec((tq, G * LANES), lambda b, hp, i: (b * (S // tq) + i, hp)),
                  pl.BlockSpec((1, nblk, G * LANES), lambda b, hp, i: (b, 0, hp))],
        out_specs=pl.BlockSpec((1, G, LANES, tq), lambda b, hp, i: (b, hp, 0, i)),
        out_shape=jax.ShapeDtypeStruct((B, nh, LANES, S), BF16),
        compiler_params=_params(("parallel", "parallel", "parallel")),
        name="moba_select",
    )(q, km)


def _moba_kernel(qat_ref, qat_next_ref, k_ref, vt_ref, o_ref, st_scr, m_scr, acc_scr):
    G = qat_ref.shape[1]
    BLK = qat_ref.shape[3]
    hd = HEAD_DIM
    nhalf = BLK // LANES
    own = pl.program_id(2)
    k_off = lax.broadcasted_iota(jnp.int32, (BLK, 1), 0)
    q_off = lax.broadcasted_iota(jnp.int32, (1, LANES), 1)

    m_scr[...] = jnp.full(m_scr.shape, NEG, F32)
    acc_scr[...] = jnp.zeros_like(acc_scr)

    def scores(n, slot, q_ref=qat_ref):
        start = pl.multiple_of(n * BLK, BLK)
        for g in range(G):
            kn = k_ref[pl.ds(start, BLK), g * LANES:(g + 1) * LANES]
            st_scr[slot, g] = jnp.dot(kn, q_ref[0, g], preferred_element_type=F32)

    def update(n, slot, causal):
        start = pl.multiple_of(n * BLK, BLK)
        for g in range(G):
            alphas, ps = [], []
            for hf in range(nhalf):
                c = g * nhalf + hf
                ls = slice(hf * LANES, (hf + 1) * LANES)
                st = st_scr[slot, g, :, ls]
                if causal:
                    st = jnp.where(k_off <= q_off + hf * LANES, st, NEG)
                m = m_scr[c]
                m_new = jnp.maximum(m, jnp.max(st, axis=0, keepdims=True))
                alphas.append(jnp.exp2(m - m_new))
                ps.append(jnp.exp2(st - m_new).astype(BF16))
                m_scr[c] = m_new
            vtn = vt_ref[0, g * V_ROWS:(g + 1) * V_ROWS, pl.ds(start, BLK)]
            pv = jnp.dot(vtn, jnp.concatenate(ps, axis=1), preferred_element_type=F32)
            acc_scr[g] = jnp.concatenate(alphas, axis=1) * acc_scr[g] + pv

    @pl.when(own == 0)
    def _():
        scores(0, 0)

    def pair(n):
        scores(n + 1, 1)
        update(n, 0, False)
        scores(n + 2, 0)
        update(n + 1, 1, False)

    def two_pairs(i, carry):
        pair(4 * i)
        pair(4 * i + 2)
        return carry

    def one_pair(i, carry):
        pair(4 * (own // 4) + 2 * i)
        return carry

    lax.fori_loop(0, own // 4, two_pairs, 0)
    lax.fori_loop(0, (own % 4) // 2, one_pair, 0)

    @pl.when(own % 2 == 0)
    def _():
        update(own, 0, True)
        scores(0, 0, qat_next_ref)

    @pl.when(own % 2 == 1)
    def _():
        scores(own, 1)
        update(own - 1, 0, False)
        scores(0, 0, qat_next_ref)
        update(own, 1, True)
    for g in range(G):
        acc = acc_scr[g]
        o_ref[0, g * hd:(g + 1) * hd, :] = (acc[:hd, :] * (1.0 / acc[hd:hd + 1, :])).astype(BF16)


def _moba_attn(qat, k, vt, B, S):
    nh = qat.shape[1]
    G = MOBA_HEADS_PER_STEP
    BLK = MOBA_BLOCK
    nblk = S // BLK
    hd = HEAD_DIM
    return pl.pallas_call(
        _moba_kernel,
        grid=(B, nh // G, nblk),
        in_specs=[pl.BlockSpec((1, G, LANES, BLK), lambda b, hp, qi: (b, hp, 0, qi)),
                  pl.BlockSpec((1, G, LANES, BLK), lambda b, hp, qi: (b, hp, 0, jnp.minimum(qi + 1, nblk - 1))),
                  pl.BlockSpec((S, G * LANES), lambda b, hp, qi: (b, hp)),
                  pl.BlockSpec((1, G * V_ROWS, S), lambda b, hp, qi: (b, hp, 0))],
        out_specs=pl.BlockSpec((1, G * hd, BLK), lambda b, hp, qi: (b, hp, qi)),
        out_shape=jax.ShapeDtypeStruct((B, nh * hd, S), BF16),
        scratch_shapes=[pltpu.VMEM((2, G, BLK, BLK), F32),
                        pltpu.VMEM((G * (BLK // LANES), 1, LANES), F32),
                        pltpu.VMEM((G, V_ROWS, BLK), F32)],
        compiler_params=_params(("parallel", "parallel", "arbitrary")),
        name="moba_attn",
    )(qat, qat, k, vt)


def _attn_out_kernel(a_ref, x_ref, w_ref, gate_ref, lg_ref, lb_ref, o_ref):
    y = lax.dot_general(a_ref[0], w_ref[...], (((0,), (0,)), ((), ())), preferred_element_type=F32)
    r = DEEPNORM_ALPHA * x_ref[...] + gate_ref[0] * y
    o_ref[...] = _layer_norm(r, lg_ref[...], lb_ref[...])


def _attn_out(at, x2, w, gate, ln_g, ln_b, S):
    T, D = x2.shape
    K = at.shape[1]
    tm = min(1024, S)
    tpb = S // tm
    row = lambda i: (i, 0)
    full2 = lambda i: (0, 0)
    return pl.pallas_call(
        _attn_out_kernel,
        grid=(T // tm,),
        in_specs=[pl.BlockSpec((1, K, tm), lambda i: (i // tpb, 0, i % tpb)), pl.BlockSpec((tm, D), row),
                  pl.BlockSpec((K, D), full2),
                  pl.BlockSpec((1, 1, D), lambda i: (i // tpb, 0, 0)),
                  pl.BlockSpec((1, D), full2), pl.BlockSpec((1, D), full2)],
        out_specs=pl.BlockSpec((tm, D), row),
        out_shape=jax.ShapeDtypeStruct((T, D), F32),
        compiler_params=_params(("parallel",)),
        name="attn_out_ln",
    )(at, x2, w, gate, ln_g.reshape(1, D), ln_b.reshape(1, D))


def kernel(x, c, w_ada, b_ada, ln_g, ln_b, a_w_in, a_conv_w, a_conv_b, a_wq, a_wk, a_wv, a_w_if, a_b_if,
           a_gn_g, a_skip, a_w_out, b_w_kv, b_wq, b_wo, moe_w_router, moe_b_router, moe_w_gate, moe_w_up,
           moe_w_down):
    B, S, D = x.shape
    T = B * S
    assert DEPTH == 2 and S % MOBA_BLOCK == 0 and S % MLSTM_CHUNK == 0
    n_layer_mod = DEPTH * N_MOD_PER_LAYER * D
    cond = _ada_cond(c, w_ada, b_ada)
    mods = cond[:, :n_layer_mod].reshape(B, DEPTH, 2, 3, 1, D)
    kv_mod = cond[:, n_layer_mod:].reshape(B, 2, 1, D)

    def mod3(layer, sub):
        m = mods[:, layer, sub]
        return m[:, 0], m[:, 1], 1.0 + m[:, 2]

    xf = x.reshape(T, D)

    shift, scale, gate = mod3(0, 0)
    nh = MLSTM_HEADS
    xm, z = _inproj(xf, shift, scale, a_w_in[0].astype(BF16), S)
    q, k, v, xc, gts = _qkv(xm, a_conv_w[0], a_conv_b[0], a_wq[0].astype(BF16), a_wk[0].astype(BF16),
                            a_wv[0].astype(BF16), a_w_if[0], a_b_if[0], S)
    g4 = gts.reshape(B, S, 2, nh)
    gcol = jnp.transpose(g4, (0, 3, 1, 2))
    grow = jnp.transpose(g4, (0, 3, 2, 1))
    hn = _mlstm(q, k, v, gcol, grow, B, S)
    xf = _mlstm_out(hn, xc, z, xf, a_gn_g[0], a_skip[0], a_w_out[0].astype(BF16), gate, ln_g[0, 0], ln_b[0, 0], S)
    shift, scale, gate = mod3(0, 1)
    xf = _moe_layer(xf, shift, scale, gate, ln_g[0, 1], ln_b[0, 1], moe_w_router, moe_b_router,
                    moe_w_gate, moe_w_up, moe_w_down, 0, S)

    shift, scale, gate = mod3(1, 0)
    q, k, vt, km = _moba_proj(xf, shift, scale, kv_mod[:, 0], kv_mod[:, 1], b_wq[0].astype(BF16),
                              b_w_kv.astype(BF16), B, S)
    km = km.reshape(B, S // MOBA_BLOCK, km.shape[-1])
    attn_t = _moba_attn(_moba_select(q, km, B, S), k, vt, B, S)
    xf = _attn_out(attn_t, xf, b_wo[0].astype(BF16), gate, ln_g[1, 0], ln_b[1, 0], S)
    shift, scale, gate = mod3(1, 1)
    xf = _moe_layer(xf, shift, scale, gate, ln_g[1, 1], ln_b[1, 1], moe_w_router, moe_b_router,
                    moe_w_gate, moe_w_up, moe_w_down, 1, S)
    return xf.reshape(B, S, D)
```
